```python
import jax
import jax.numpy as jnp
from jax import lax
import numpy as np

D_MODEL = 1024
BATCH = 4
SEQ = 4096
DEPTH = 1

MIX_WIDTH = 2 * D_MODEL
GDN_WIDTH = MIX_WIDTH // 2
MLSTM_WIDTH = MIX_WIDTH - GDN_WIDTH
GDN_HEADS = 8
GDN_DK = GDN_WIDTH // GDN_HEADS
GDN_DV = GDN_WIDTH // GDN_HEADS
MLSTM_HEADS = 8
MLSTM_DV = MLSTM_WIDTH // MLSTM_HEADS
MLSTM_DQK = MLSTM_DV // 2
CONV_K = 5
CHUNK = 64
RMS_EPS = 1e-6
L2_EPS = 1e-6
GDN_IN_SIZES = (GDN_WIDTH, GDN_WIDTH, GDN_WIDTH, GDN_WIDTH, 2 * GDN_HEADS, 2 * GDN_HEADS)
MLSTM_IN_SIZES = (MLSTM_HEADS * MLSTM_DQK, MLSTM_HEADS * MLSTM_DQK, MLSTM_WIDTH, MLSTM_WIDTH, MLSTM_WIDTH, 4 * MLSTM_HEADS)
GDN_IN = sum(GDN_IN_SIZES)
MLSTM_IN = sum(MLSTM_IN_SIZES)
IN_WIDTH = GDN_IN + MLSTM_IN

kernel_name = 'hybrid_gdn_mlstm_bidir_block'


def _split(t, sizes):
    return jnp.split(t, np.cumsum(sizes)[:-1].tolist(), axis=-1)


def _rms_norm(x, w):
    xf = x.astype(jnp.float32)
    xf = xf * lax.rsqrt(jnp.mean(xf * xf, axis=-1, keepdims=True) + RMS_EPS)
    return (xf * w.astype(jnp.float32)).astype(x.dtype)


def _head_rms(t):
    return t * lax.rsqrt(jnp.mean(t * t, axis=-1, keepdims=True) + RMS_EPS)


def _l2norm(t):
    return t * lax.rsqrt(jnp.sum(t * t, axis=-1, keepdims=True) + L2_EPS)


def _centred_dwconv(x, w):
    c = x.shape[-1]
    k = w.shape[0]
    return lax.conv_general_dilated(
        x, w[:, None, :].astype(x.dtype), window_strides=(1,), padding=[(k // 2, k // 2)],
        dimension_numbers=('NWC', 'WIO', 'NWC'), feature_group_count=c)


def _to_heads(t, n_heads):
    b, s, c = t.shape
    return t.reshape(b, s, n_heads, c // n_heads).transpose(0, 2, 1, 3).astype(jnp.float32)


def _from_heads(t):
    b, h, s, d = t.shape
    return t.transpose(0, 2, 1, 3).reshape(b, s, h * d)


def _gates_to_heads(t, n_groups, n_heads):
    b, s, _ = t.shape
    return t.reshape(b, s, n_groups, n_heads).transpose(2, 0, 3, 1).astype(jnp.float32)


def _to_chunks(t):
    b, h, s = t.shape[:3]
    return t.reshape(b, h, s // CHUNK, CHUNK, *t.shape[3:])


def _from_chunks(t):
    n, b, h, l, d = t.shape
    return jnp.moveaxis(t, 0, 2).reshape(b, h, n * l, d)


def _flip(t):
    return jnp.flip(t, axis=2)


def _gated_delta_chunked(q, k, v, g, beta):
    b_, h_, _, dk = q.shape
    dv = v.shape[-1]
    q, k, v, g, beta = (_to_chunks(t) for t in (q, k, v, g, beta))
    g = jnp.cumsum(g, axis=-1)
    lower = jnp.tril(jnp.ones((CHUNK, CHUNK), dtype=bool))
    strict = jnp.tril(jnp.ones((CHUNK, CHUNK), dtype=bool), -1)
    decay = jnp.exp(jnp.where(lower, g[..., :, None] - g[..., None, :], -jnp.inf))
    k_beta = k * beta[..., None]
    a_mat = jnp.where(strict, jnp.einsum('bhnid,bhnjd->bhnij', k_beta, k) * decay, 0.0)
    eye = jnp.eye(CHUNK, dtype=q.dtype)
    rhs = jnp.concatenate([v * beta[..., None], k_beta * jnp.exp(g)[..., None]], axis=-1)
    sol = lax.linalg.triangular_solve(a_mat + eye, rhs, left_side=True, lower=True, unit_diagonal=True)
    u, w = sol[..., :dv], sol[..., dv:]
    qk = jnp.where(lower, jnp.einsum('bhnid,bhnjd->bhnij', q, k) * decay, 0.0)
    q_dec = q * jnp.exp(g)[..., None]
    k_tail = k * jnp.exp(g[..., -1:] - g)[..., None]
    chunk_dec = jnp.exp(g[..., -1])
    xs = tuple(jnp.moveaxis(t, 2, 0) for t in (qk, q_dec, k_tail, u, w, chunk_dec))

    def step(state, inp):
        qk_c, qd_c, kt_c, u_c, w_c, cd_c = inp
        v_new = u_c - jnp.einsum('bhld,bhde->bhle', w_c, state)
        o = jnp.einsum('bhld,bhde->bhle', qd_c, state) + jnp.einsum('bhij,bhje->bhie', qk_c, v_new)
        state = state * cd_c[..., None, None] + jnp.einsum('bhld,bhle->bhde', kt_c, v_new)
        return state, o

    state0 = jnp.zeros((b_, h_, dk, dv), jnp.float32)
    _, o = lax.scan(step, state0, xs)
    return _from_chunks(o)


def _mlstm_chunked(q, k, v, i_pre, f_pre):
    b_, h_, _, dqk = q.shape
    dv = v.shape[-1]
    q, k, v, i_pre, f_pre = (_to_chunks(t) for t in (q, k, v, i_pre, f_pre))
    cum_logf = jnp.cumsum(jax.nn.log_sigmoid(f_pre), axis=-1)
    lower = jnp.tril(jnp.ones((CHUNK, CHUNK), dtype=bool))
    log_w = jnp.where(lower, cum_logf[..., :, None] - cum_logf[..., None, :] + i_pre[..., None, :], -jnp.inf)
    m_intra = jnp.max(log_w, axis=-1)
    qk = jnp.einsum('bhnid,bhnjd->bhnij', q, k)
    xs = tuple(jnp.moveaxis(t, 2, 0) for t in (q, k, v, cum_logf, log_w, m_intra, qk))

    def step(carry, inp):
        c_state, n_state, m_state = carry
        q_c, k_c, v_c, b_c, lw_c, mi_c, qk_c = inp
        m_t = jnp.maximum(b_c + m_state[..., None], mi_c)
        inter = jnp.exp(b_c + m_state[..., None] - m_t)
        w_intra = jnp.exp(lw_c - m_t[..., None]) * qk_c
        num = inter[..., None] * jnp.einsum('bhld,bhde->bhle', q_c, c_state) + jnp.einsum('bhts,bhse->bhte', w_intra, v_c)
        den = inter * jnp.einsum('bhld,bhd->bhl', q_c, n_state) + jnp.sum(w_intra, axis=-1)
        h = num / jnp.maximum(jnp.abs(den), jnp.exp(-m_t))[..., None]
        lw_last = lw_c[..., -1, :]
        m_new = jnp.maximum(b_c[..., -1] + m_state, mi_c[..., -1])
        carry_dec = jnp.exp(b_c[..., -1] + m_state - m_new)
        wk = jnp.exp(lw_last - m_new[..., None])[..., None] * k_c
        c_state = carry_dec[..., None, None] * c_state + jnp.einsum('bhld,bhle->bhde', wk, v_c)
        n_state = carry_dec[..., None] * n_state + jnp.sum(wk, axis=2)
        return (c_state, n_state, m_new), h

    carry0 = (jnp.zeros((b_, h_, dqk, dv), jnp.float32),
              jnp.zeros((b_, h_, dqk), jnp.float32),
              jnp.zeros((b_, h_), jnp.float32))
    _, h = lax.scan(step, carry0, xs)
    return _from_chunks(h)


def _gdn_branch(cols, conv_w, a_log, dt_bias, norm_w):
    q, k, v, z, a, b = _split(cols, GDN_IN_SIZES)
    qkv = jax.nn.silu(_centred_dwconv(jnp.concatenate([q, k, v], axis=-1), conv_w))
    q, k, v = _split(qkv, (GDN_WIDTH, GDN_WIDTH, GDN_WIDTH))
    q = _l2norm(_to_heads(q, GDN_HEADS)) * (GDN_DK ** -0.5)
    k = _l2norm(_to_heads(k, GDN_HEADS))
    v = _to_heads(v, GDN_HEADS)
    a = _gates_to_heads(a, 2, GDN_HEADS)
    b = _gates_to_heads(b, 2, GDN_HEADS)
    g = -jnp.exp(a_log.astype(jnp.float32))[:, None, :, None] * jax.nn.softplus(a + dt_bias.astype(jnp.float32)[:, None, :, None])
    beta = jax.nn.sigmoid(b)
    o_fwd = _gated_delta_chunked(q, k, v, g[0], beta[0])
    o_bwd = _flip(_gated_delta_chunked(_flip(q), _flip(k), _flip(v), _flip(g[1]), _flip(beta[1])))
    o = _head_rms(o_fwd + o_bwd) * norm_w.astype(jnp.float32)
    return (_from_heads(o) * jax.nn.silu(z.astype(jnp.float32))).astype(cols.dtype)


def _mlstm_branch(cols, conv_w, gate_bias, norm_w):
    q, k, v, o, z, gates = _split(cols, MLSTM_IN_SIZES)
    qk = jax.nn.silu(_centred_dwconv(jnp.concatenate([q, k], axis=-1), conv_w))
    q, k = _split(qk, (MLSTM_HEADS * MLSTM_DQK, MLSTM_HEADS * MLSTM_DQK))
    q = _to_heads(q, MLSTM_HEADS)
    k = _to_heads(k, MLSTM_HEADS) * (MLSTM_DQK ** -0.5)
    v = _to_heads(v, MLSTM_HEADS)
    gates = _gates_to_heads(gates, 4, MLSTM_HEADS) + gate_bias.astype(jnp.float32)[:, None, :, None]
    h_fwd = _mlstm_chunked(q, k, v, gates[0], gates[2])
    h_bwd = _flip(_mlstm_chunked(_flip(q), _flip(k), _flip(v), _flip(gates[1]), _flip(gates[3])))
    h = _from_heads(_head_rms(h_fwd + h_bwd)) * norm_w.astype(jnp.float32)
    h = h * jax.nn.sigmoid(o.astype(jnp.float32)) * jax.nn.silu(z.astype(jnp.float32))
    return h.astype(cols.dtype)


def setup_inputs(seed: int = 0) -> dict:
    key = jax.random.key(seed)
    ks = jax.random.split(key, 16)
    f32 = jnp.float32
    x = jax.random.normal(ks[0], (BATCH, SEQ, D_MODEL), f32)
    norm_pre_w = 1.0 + 0.02 * jax.random.normal(ks[1], (DEPTH, D_MODEL), f32)
    w_in = jax.random.normal(ks[2], (DEPTH, D_MODEL, IN_WIDTH), f32) * (D_MODEL ** -0.5)
    gdn_conv_w = jax.random.normal(ks[3], (DEPTH, CONV_K, 3 * GDN_WIDTH), f32) * (CONV_K ** -0.5)
    gdn_a_log = jnp.log(jax.random.uniform(ks[4], (DEPTH, 2, GDN_HEADS), f32, 1.0, 16.0))
    dt = jnp.exp(jax.random.uniform(ks[5], (DEPTH, 2, GDN_HEADS), f32, float(np.log(1e-3)), float(np.log(1e-1))))
    gdn_dt_bias = dt + jnp.log(-jnp.expm1(-dt))
    gdn_norm_w = 1.0 + 0.02 * jax.random.normal(ks[6], (DEPTH, GDN_DV), f32)
    mlstm_conv_w = jax.random.normal(ks[7], (DEPTH, CONV_K, 2 * MLSTM_HEADS * MLSTM_DQK), f32) * (CONV_K ** -0.5)
    i_bias = 0.1 * jax.random.normal(ks[8], (DEPTH, 2, MLSTM_HEADS), f32)
    f_bias = jnp.linspace(3.0, 6.0, MLSTM_HEADS, dtype=f32) + 0.1 * jax.random.normal(ks[9], (DEPTH, 2, MLSTM_HEADS), f32)
    mlstm_gate_bias = jnp.concatenate([i_bias, f_bias], axis=1)
    mlstm_norm_w = 1.0 + 0.02 * jax.random.normal(ks[10], (DEPTH, MLSTM_WIDTH), f32)
    w_out = jax.random.normal(ks[11], (DEPTH, MIX_WIDTH, D_MODEL), f32) * (MIX_WIDTH ** -0.5)
    norm_post_w = 1.0 + 0.02 * jax.random.normal(ks[12], (DEPTH, D_MODEL), f32)
    return {'x': x, 'norm_pre_w': norm_pre_w, 'w_in': w_in, 'gdn_conv_w': gdn_conv_w,
            'gdn_a_log': gdn_a_log, 'gdn_dt_bias': gdn_dt_bias, 'gdn_norm_w': gdn_norm_w,
            'mlstm_conv_w': mlstm_conv_w, 'mlstm_gate_bias': mlstm_gate_bias, 'mlstm_norm_w': mlstm_norm_w,
            'w_out': w_out, 'norm_post_w': norm_post_w}


def reference(x, norm_pre_w, w_in, gdn_conv_w, gdn_a_log, gdn_dt_bias, gdn_norm_w,
              mlstm_conv_w, mlstm_gate_bias, mlstm_norm_w, w_out, norm_post_w):
    for layer in range(DEPTH):
        h = _rms_norm(x, norm_pre_w[layer])
        proj = jnp.einsum('bsd,de->bse', h, w_in[layer])
        gdn_cols, mlstm_cols = jnp.split(proj, [GDN_IN], axis=-1)
        g_out = _gdn_branch(gdn_cols, gdn_conv_w[layer], gdn_a_log[layer], gdn_dt_bias[layer], gdn_norm_w[layer])
        m_out = _mlstm_branch(mlstm_cols, mlstm_conv_w[layer], mlstm_gate_bias[layer], mlstm_norm_w[layer])
        mixed = jnp.einsum('bse,ed->bsd', jnp.concatenate([g_out, m_out], axis=-1), w_out[layer])
        x = x + _rms_norm(mixed, norm_post_w[layer])
    return x
```

```python
import functools

import jax
import jax.numpy as jnp
from jax import lax
from jax.experimental import pallas as pl
from jax.experimental.pallas import tpu as pltpu

F32 = jnp.float32
BF16 = jnp.bfloat16

N_HEADS = 8
HEAD_DIM = 128
MLSTM_DQK = 64
CONV_K = 5
CONV_HALO = 8
SCAN_CHUNK = 128
GATE_ROWS = 16
RMS_EPS = 1e-6
L2_EPS = 1e-6
NEG_BIG = -1e30
VMEM_LIMIT_BYTES = 56 * 1024 * 1024


def _dot(a, b):
    return jnp.dot(a, b, preferred_element_type=F32)


def _dot_exact(a, b):
    return jnp.dot(a, b, preferred_element_type=F32, precision=lax.Precision.HIGHEST)


def _softplus(x):
    return jnp.maximum(x, 0.0) + jnp.log1p(jnp.exp(-jnp.abs(x)))


def _sigmoid(x):
    return 1.0 / (1.0 + jnp.exp(-x))


def _silu(x):
    return x * _sigmoid(x)


def _inproj_kernel(x_ref, npw_ref, w_ref, wg_ref, p_ref, gt_ref, h_scr, *, n_sub):
    j = pl.program_id(2)

    @pl.when(j == 0)
    def _():
        x = x_ref[0]
        ms = jnp.mean(x * x, axis=-1, keepdims=True)
        h = (x * lax.rsqrt(ms + RMS_EPS) * npw_ref[...]).astype(BF16)
        h_scr[...] = h
        gt_ref[0] = lax.dot_general(wg_ref[...], h, (((1,), (1,)), ((), ())),
                                    preferred_element_type=F32)

    h = h_scr[...]
    for c in range(n_sub // 2):
        acc = _dot(h, w_ref[:, c * 256:(c + 1) * 256])
        p_ref[0, 2 * c] = acc[:, :128]
        p_ref[0, 2 * c + 1] = acc[:, 128:]


def _in_proj(x, npw, w_main, wg_t, *, tm=1024, tn=2048):
    b, s, d = x.shape
    n_blocks = w_main.shape[1] // 128
    n_sub = tn // 128
    return pl.pallas_call(
        functools.partial(_inproj_kernel, n_sub=n_sub),
        grid=(b, s // tm, w_main.shape[1] // tn),
        in_specs=[
            pl.BlockSpec((1, tm, d), lambda bi, si, j: (bi, si, 0)),
            pl.BlockSpec((1, d), lambda bi, si, j: (0, 0)),
            pl.BlockSpec((d, tn), lambda bi, si, j: (0, j)),
            pl.BlockSpec(wg_t.shape, lambda bi, si, j: (0, 0)),
        ],
        out_specs=[
            pl.BlockSpec((1, n_sub, tm, 128), lambda bi, si, j: (bi, j, si, 0)),
            pl.BlockSpec((1, wg_t.shape[0], tm), lambda bi, si, j: (bi, 0, si)),
        ],
        out_shape=[
            jax.ShapeDtypeStruct((b, n_blocks, s, 128), F32),
            jax.ShapeDtypeStruct((b, wg_t.shape[0], s), F32),
        ],
        scratch_shapes=[pltpu.VMEM((tm, d), BF16)],
        compiler_params=pltpu.CompilerParams(
            dimension_semantics=("arbitrary", "arbitrary", "arbitrary"),
            vmem_limit_bytes=VMEM_LIMIT_BYTES),
        name="in_proj",
    )(x, npw, w_main, wg_t)


def _gates_kernel(gt_ref, alog_ref, dtb_ref, gb_ref, gout_ref, mout_ref, *, seq):
    nc = seq // SCAN_CHUNK
    ch = SCAN_CHUNK
    ii = lax.broadcasted_iota(jnp.int32, (ch, ch), 0)
    jj = lax.broadcasted_iota(jnp.int32, (ch, ch), 1)
    upper = (ii <= jj).astype(F32)
    lower = (ii >= jj).astype(F32)
    lane = lax.broadcasted_iota(jnp.int32, (N_HEADS, ch), 1)
    neg_a = -jnp.exp(alog_ref[...])
    dtb = dtb_ref[...]
    gb = gb_ref[...]
    zeros8 = jnp.zeros((N_HEADS, ch), F32)

    def bcast(col):
        return jnp.broadcast_to(col, (N_HEADS, ch))

    def cummax(a, d):
        y = a
        for sh in (1, 2, 4, 8, 16, 32, 64):
            if d == 0:
                y = jnp.maximum(y, jnp.where(lane >= sh, pltpu.roll(y, sh, 1), NEG_BIG))
            else:
                y = jnp.maximum(y, jnp.where(lane < ch - sh, pltpu.roll(y, ch - sh, 1), NEG_BIG))
        return y

    def gdn_part(c):
        sl = pl.ds(pl.multiple_of(c * ch, ch), ch)
        g = neg_a * _softplus(gt_ref[0, 0:16, sl] + dtb)
        beta = _sigmoid(gt_ref[0, 16:32, sl])
        gp = _dot_exact(g, upper)
        gs = _dot_exact(g, lower)
        for d in (0, 1):
            rows = slice(8 * d, 8 * d + 8)
            gcum = (gp if d == 0 else gs)[rows]
            gend = bcast(gp[rows, ch - 1:ch])
            gout_ref[0, 3 * d + 0, :, sl] = gcum
            gout_ref[0, 3 * d + 1, :, sl] = jnp.exp(gend - gcum)
            gout_ref[0, 3 * d + 2, :, sl] = jnp.exp(gend)
            gout_ref[0, 6 + 2 * d, :, sl] = beta[rows]
            gout_ref[0, 7 + 2 * d, :, sl] = jnp.exp(gcum)
        for q in range(10, GATE_ROWS):
            gout_ref[0, q, :, sl] = zeros8

    def mlstm_part(c, carry, d):
        c_f, c_r = carry
        sl = pl.ds(pl.multiple_of(c * ch, ch), ch)
        ib = gt_ref[0, 32 + 8 * d:40 + 8 * d, sl] + gb[8 * d:8 * d + 8]
        lf = -_softplus(-(gt_ref[0, 48 + 8 * d:56 + 8 * d, sl] + gb[16 + 8 * d:24 + 8 * d]))
        fcum = _dot_exact(lf, upper if d == 0 else lower) + c_f
        a = ib - fcum
        r = jnp.maximum(cummax(a, d), c_r)
        end = ch - 1 if d == 0 else 0
        r_end = bcast(r[:, end:end + 1])
        f_end = bcast(fcum[:, end:end + 1])
        mout_ref[0, 3 * d + 0, :, sl] = a
        mout_ref[0, 3 * d + 1, :, sl] = jnp.exp(a - r_end)
        mout_ref[0, 3 * d + 2, :, sl] = jnp.exp(c_r - r_end)
        mout_ref[0, 6 + 3 * d, :, sl] = r
        mout_ref[0, 7 + 3 * d, :, sl] = jnp.exp(c_r - r)
        mout_ref[0, 8 + 3 * d, :, sl] = jnp.exp(-(fcum + r))
        if d == 0:
            for q in range(12, GATE_ROWS):
                mout_ref[0, q, :, sl] = zeros8
        return f_end, r_end

    def fwd_body(c, carry):
        gdn_part(c)
        return mlstm_part(c, carry, 0)

    def bwd_body(i, carry):
        return mlstm_part(nc - 1 - i, carry, 1)

    lax.fori_loop(0, nc, fwd_body, (zeros8, zeros8))
    lax.fori_loop(0, nc, bwd_body, (zeros8, zeros8))


def _gates(gt, alog, dtb, gb):
    b, n_rows, s = gt.shape
    out_block = pl.BlockSpec((1, GATE_ROWS, N_HEADS, s), lambda bi: (bi, 0, 0, 0))
    out_shape = jax.ShapeDtypeStruct((b, GATE_ROWS, N_HEADS, s), F32)
    return pl.pallas_call(
        functools.partial(_gates_kernel, seq=s),
        grid=(b,),
        in_specs=[
            pl.BlockSpec((1, n_rows, s), lambda bi: (bi, 0, 0)),
            pl.BlockSpec(alog.shape, lambda bi: (0, 0)),
            pl.BlockSpec(dtb.shape, lambda bi: (0, 0)),
            pl.BlockSpec(gb.shape, lambda bi: (0, 0)),
        ],
        out_specs=[out_block, out_block],
        out_shape=[out_shape, out_shape],
        compiler_params=pltpu.CompilerParams(
            dimension_semantics=("arbitrary",), vmem_limit_bytes=VMEM_LIMIT_BYTES),
        name="gates",
    )(gt, alog, dtb, gb)


def _conv_window(ref, c, nc, seq):
    ch = SCAN_CHUNK
    t0 = pl.multiple_of(c * ch, ch)
    main = ref[0, 0, pl.ds(t0, ch), :]
    prev = ref[0, 0, pl.ds(pl.multiple_of(jnp.maximum(t0 - CONV_HALO, 0), CONV_HALO), CONV_HALO), :]
    nxt = ref[0, 0, pl.ds(pl.multiple_of(jnp.minimum(t0 + ch, seq - CONV_HALO), CONV_HALO), CONV_HALO), :]
    prev = jnp.where(c > 0, prev, 0.0)
    nxt = jnp.where(c < nc - 1, nxt, 0.0)
    return jnp.concatenate([prev, main, nxt], axis=0)


def _conv_silu(win, w):
    ch = SCAN_CHUNK
    base = CONV_HALO - CONV_K // 2
    acc = win[base:base + ch] * w[0:1]
    for j in range(1, CONV_K):
        acc = acc + win[base + j:base + j + ch] * w[j:j + 1]
    return _silu(acc)


def _gate_cols(rows):
    ch = SCAN_CHUNK
    return jnp.concatenate([rows, jnp.zeros((ch - GATE_ROWS, ch), F32)], axis=0).T


def _gdn_kernel(q_ref, k_ref, v_ref, z_ref, row_ref, cwq_ref, cwk_ref, cwv_ref, nw_ref, out_ref,
                u_scr, w_scr, qd_scr, aqk_scr, ktt_scr, o_scr, *, seq):
    ch = SCAN_CHUNK
    nc = seq // ch
    ii = lax.broadcasted_iota(jnp.int32, (ch, ch), 0)
    jj = lax.broadcasted_iota(jnp.int32, (ch, ch), 1)
    eye = (ii == jj).astype(F32)
    incl = (ii >= jj, ii <= jj)
    strict = (ii > jj, ii < jj)
    xor = ii ^ jj
    n_levels = ch.bit_length() - 1

    def level_mask(p, d):
        odd = ii if d == 0 else jj
        return ((xor >> p) == 1) & (((odd >> p) & 1) == 1)

    def tri_inverse(a, d):
        t = eye - jnp.where(level_mask(0, d), a, 0.0)
        for p in range(1, n_levels):
            lp = jnp.where(level_mask(p, d), a, 0.0).astype(BF16)
            tb = t.astype(BF16)
            t = t - _dot(tb, _dot(lp, tb).astype(BF16))
        return t

    def prep_body(c, carry):
        t0 = pl.multiple_of(c * ch, ch)
        sl = pl.ds(t0, ch)
        q0 = _conv_silu(_conv_window(q_ref, c, nc, seq), cwq_ref[...])
        k0 = _conv_silu(_conv_window(k_ref, c, nc, seq), cwk_ref[...])
        v = _conv_silu(_conv_window(v_ref, c, nc, seq), cwv_ref[...])
        q = q0 * lax.rsqrt(jnp.sum(q0 * q0, axis=-1, keepdims=True) + L2_EPS) * (HEAD_DIM ** -0.5)
        k = k0 * lax.rsqrt(jnp.sum(k0 * k0, axis=-1, keepdims=True) + L2_EPS)
        kt = k.T
        ktb = kt.astype(BF16)
        kk = _dot(k.astype(BF16), ktb)
        qk = _dot(q.astype(BF16), ktb)
        rows = row_ref[0, 0, :, sl]
        cols = _gate_cols(rows)
        for d in (0, 1):
            g_row = rows[3 * d:3 * d + 1]
            et_row = rows[3 * d + 1:3 * d + 2]
            g_col = cols[:, 3 * d:3 * d + 1]
            beta = cols[:, 6 + 2 * d:7 + 2 * d]
            eg = cols[:, 7 + 2 * d:8 + 2 * d]
            decay = jnp.exp(jnp.where(incl[d], g_col - g_row, NEG_BIG))
            a = jnp.where(strict[d], kk * decay * beta, 0.0)
            t = tri_inverse(a, d)
            rhs = jnp.concatenate([v * beta, k * (beta * eg)], axis=1).astype(BF16)
            sol = _dot(t.astype(BF16), rhs)
            u_scr[d, sl, :] = sol[:, :HEAD_DIM]
            w_scr[d, sl, :] = sol[:, HEAD_DIM:].astype(BF16)
            qd_scr[d, sl, :] = (q * eg).astype(BF16)
            aqk_scr[d, sl, :] = (qk * decay).astype(BF16)
            ktt_scr[d, :, sl] = (kt * et_row).astype(BF16)
        return carry

    def scan_body(i, states):
        new_states = []
        for d, c in ((0, i), (1, nc - 1 - i)):
            sl = pl.ds(pl.multiple_of(c * ch, ch), ch)
            s = states[d]
            sb = s.astype(BF16)
            v_new = u_scr[d, sl, :] - _dot(w_scr[d, sl, :], sb)
            vb = v_new.astype(BF16)
            o_scr[d, sl, :] = _dot(qd_scr[d, sl, :], sb) + _dot(aqk_scr[d, sl, :], vb)
            chunk_decay = row_ref[0, 0, 3 * d + 2:3 * d + 3, sl]
            new_states.append(s * chunk_decay + _dot(ktt_scr[d, :, sl], vb))
        return tuple(new_states)

    def final_body(c, carry):
        sl = pl.ds(pl.multiple_of(c * ch, ch), ch)
        o = o_scr[0, sl, :] + o_scr[1, sl, :]
        y = o * lax.rsqrt(jnp.mean(o * o, axis=-1, keepdims=True) + RMS_EPS) * nw_ref[...]
        out_ref[0, sl, :] = (y * _silu(z_ref[0, 0, sl, :])).astype(out_ref.dtype)
        return carry

    lax.fori_loop(0, nc, prep_body, 0)
    zero_state = jnp.zeros((HEAD_DIM, HEAD_DIM), F32)
    lax.fori_loop(0, nc, scan_body, (zero_state, zero_state))
    lax.fori_loop(0, nc, final_body, 0)


def _gdn(p, rows, conv_w, norm_w, *, col0):
    b, _, s, _ = p.shape
    h = N_HEADS

    def pblock(off):
        return pl.BlockSpec((1, 1, s, HEAD_DIM), lambda bi, hi: (bi, col0 + off + hi, 0, 0))

    def cblock(off):
        return pl.BlockSpec((CONV_K, HEAD_DIM), lambda bi, hi: (0, off + hi))

    return pl.pallas_call(
        functools.partial(_gdn_kernel, seq=s),
        grid=(b, h),
        in_specs=[
            pblock(0), pblock(h), pblock(2 * h), pblock(3 * h),
            pl.BlockSpec((1, 1, GATE_ROWS, s), lambda bi, hi: (bi, hi, 0, 0)),
            cblock(0), cblock(h), cblock(2 * h),
            pl.BlockSpec((1, HEAD_DIM), lambda bi, hi: (0, 0)),
        ],
        out_specs=pl.BlockSpec((1, s, HEAD_DIM), lambda bi, hi: (bi, 0, hi)),
        out_shape=jax.ShapeDtypeStruct((b, s, h * HEAD_DIM), BF16),
        scratch_shapes=[
            pltpu.VMEM((2, s, HEAD_DIM), F32),
            pltpu.VMEM((2, s, HEAD_DIM), BF16),
            pltpu.VMEM((2, s, HEAD_DIM), BF16),
            pltpu.VMEM((2, s, SCAN_CHUNK), BF16),
            pltpu.VMEM((2, HEAD_DIM, s), BF16),
            pltpu.VMEM((2, s, HEAD_DIM), F32),
        ],
        compiler_params=pltpu.CompilerParams(
            dimension_semantics=("arbitrary", "arbitrary"), vmem_limit_bytes=VMEM_LIMIT_BYTES),
        name="gdn",
    )(p, p, p, p, rows, conv_w, conv_w, conv_w, norm_w)


def _mlstm_kernel(qk_ref, v_ref, o_ref, z_ref, row_ref, cw_ref, nw_ref, out_ref,
                  qm_scr, kt_scr, qkm_scr, h_scr, *, seq):
    ch = SCAN_CHUNK
    nc = seq // ch
    ii = lax.broadcasted_iota(jnp.int32, (ch, ch), 0)
    jj = lax.broadcasted_iota(jnp.int32, (ch, ch), 1)
    incl = (ii >= jj, ii <= jj)
    ones = jnp.ones((ch, HEAD_DIM), BF16)

    def prep_body(c, carry):
        sl = pl.ds(pl.multiple_of(c * ch, ch), ch)
        t = _conv_silu(_conv_window(qk_ref, c, nc, seq), cw_ref[...])
        qm = jnp.where(jj < MLSTM_DQK, t, 0.0).astype(BF16)
        tt = t.T
        kt = jnp.concatenate([tt[MLSTM_DQK:], jnp.zeros((ch - MLSTM_DQK, ch), F32)], axis=0)
        ktb = (kt * (MLSTM_DQK ** -0.5)).astype(BF16)
        qm_scr[sl, :] = qm
        kt_scr[:, sl] = ktb
        qkm_scr[sl, :] = _dot(qm, ktb)
        return carry

    def scan_body(i, states):
        new_states = []
        for d, c in ((0, i), (1, nc - 1 - i)):
            sl = pl.ds(pl.multiple_of(c * ch, ch), ch)
            st = states[d]
            rows = row_ref[0, 0, :, sl]
            cols = _gate_cols(rows)
            a_row = rows[3 * d:3 * d + 1]
            awk_row = rows[3 * d + 1:3 * d + 2]
            dec_row = rows[3 * d + 2:3 * d + 3]
            r_col = cols[:, 6 + 3 * d:7 + 3 * d]
            inter = cols[:, 7 + 3 * d:8 + 3 * d]
            enm = cols[:, 8 + 3 * d:9 + 3 * d]
            vaug = jnp.concatenate([v_ref[0, 0, sl, :].astype(BF16), ones], axis=1)
            wt = jnp.exp(jnp.where(incl[d], a_row - r_col, NEG_BIG)) * qkm_scr[sl, :]
            nd = inter * _dot(qm_scr[sl, :], st.astype(BF16)) + _dot(wt.astype(BF16), vaug)
            num = nd[:, :HEAD_DIM]
            den = nd[:, HEAD_DIM:]
            h_scr[d, sl, :] = num / jnp.maximum(jnp.abs(den), enm)
            wk = (kt_scr[:, sl].astype(F32) * awk_row).astype(BF16)
            dec = jnp.concatenate([dec_row, dec_row], axis=1)
            new_states.append(st * dec + _dot(wk, vaug))
        return tuple(new_states)

    def final_body(c, carry):
        sl = pl.ds(pl.multiple_of(c * ch, ch), ch)
        hh = h_scr[0, sl, :] + h_scr[1, sl, :]
        y = hh * lax.rsqrt(jnp.mean(hh * hh, axis=-1, keepdims=True) + RMS_EPS) * nw_ref[...]
        y = y * _sigmoid(o_ref[0, 0, sl, :]) * _silu(z_ref[0, 0, sl, :])
        out_ref[0, sl, :] = y.astype(out_ref.dtype)
        return carry

    lax.fori_loop(0, nc, prep_body, 0)
    zero_state = jnp.zeros((ch, 2 * HEAD_DIM), F32)
    lax.fori_loop(0, nc, scan_body, (zero_state, zero_state))
    lax.fori_loop(0, nc, final_body, 0)


def _mlstm(p, rows, conv_w, norm_w, *, col0):
    b, _, s, _ = p.shape
    h = N_HEADS

    def pblock(off):
        return pl.BlockSpec((1, 1, s, HEAD_DIM), lambda bi, hi: (bi, col0 + off + hi, 0, 0))

    return pl.pallas_call(
        functools.partial(_mlstm_kernel, seq=s),
        grid=(b, h),
        in_specs=[
            pblock(0), pblock(h), pblock(2 * h), pblock(3 * h),
            pl.BlockSpec((1, 1, GATE_ROWS, s), lambda bi, hi: (bi, hi, 0, 0)),
            pl.BlockSpec((CONV_K, HEAD_DIM), lambda bi, hi: (0, hi)),
            pl.BlockSpec((1, HEAD_DIM), lambda bi, hi: (0, hi)),
        ],
        out_specs=pl.BlockSpec((1, s, HEAD_DIM), lambda bi, hi: (bi, 0, hi)),
        out_shape=jax.ShapeDtypeStruct((b, s, h * HEAD_DIM), BF16),
        scratch_shapes=[
            pltpu.VMEM((s, HEAD_DIM), BF16),
            pltpu.VMEM((HEAD_DIM, s), BF16),
            pltpu.VMEM((s, SCAN_CHUNK), F32),
            pltpu.VMEM((2, s, HEAD_DIM), F32),
        ],
        compiler_params=pltpu.CompilerParams(
            dimension_semantics=("arbitrary", "arbitrary"), vmem_limit_bytes=VMEM_LIMIT_BYTES),
        name="mlstm",
    )(p, p, p, p, rows, conv_w, norm_w)


def _outproj_kernel(g_ref, m_ref, wo_ref, x_ref, npw_ref, y_ref, *, width):
    mixed = _dot(g_ref[0], wo_ref[:width]) + _dot(m_ref[0], wo_ref[width:])
    ms = jnp.mean(mixed * mixed, axis=-1, keepdims=True)
    y_ref[0] = x_ref[0] + mixed * lax.rsqrt(ms + RMS_EPS) * npw_ref[...]


def _out_proj(g, m, w_out, x, npw, *, tm=512):
    b, s, d = x.shape
    width = g.shape[-1]
    return pl.pallas_call(
        functools.partial(_outproj_kernel, width=width),
        grid=(b, s // tm),
        in_specs=[
            pl.BlockSpec((1, tm, width), lambda bi, si: (bi, si, 0)),
            pl.BlockSpec((1, tm, width), lambda bi, si: (bi, si, 0)),
            pl.BlockSpec(w_out.shape, lambda bi, si: (0, 0)),
            pl.BlockSpec((1, tm, d), lambda bi, si: (bi, si, 0)),
            pl.BlockSpec((1, d), lambda bi, si: (0, 0)),
        ],
        out_specs=pl.BlockSpec((1, tm, d), lambda bi, si: (bi, si, 0)),
        out_shape=jax.ShapeDtypeStruct((b, s, d), F32),
        compiler_params=pltpu.CompilerParams(
            dimension_semantics=("arbitrary", "arbitrary"), vmem_limit_bytes=VMEM_LIMIT_BYTES),
        name="out_proj",
    )(g, m, w_out, x, npw)


def _layer(x, norm_pre_w, w_in, gdn_conv_w, gdn_a_log, gdn_dt_bias, gdn_norm_w,
           mlstm_conv_w, mlstm_gate_bias, mlstm_norm_w, w_out, norm_post_w):
    d = x.shape[-1]
    h = N_HEADS
    gw = h * HEAD_DIM
    mqk = h * MLSTM_DQK
    gdn_in = 4 * gw + 4 * h
    m0 = gdn_in
    mq = w_in[:, m0:m0 + mqk].reshape(d, h, MLSTM_DQK)
    mk = w_in[:, m0 + mqk:m0 + 2 * mqk].reshape(d, h, MLSTM_DQK)
    m_qk = jnp.concatenate([mq, mk], axis=2).reshape(d, gw)
    m_rest = w_in[:, m0 + 2 * mqk:m0 + 2 * mqk + 3 * gw]
    w_main = jnp.concatenate([w_in[:, :4 * gw], m_qk, m_rest], axis=1).astype(BF16)
    gate_cols = jnp.concatenate([w_in[:, 4 * gw:gdn_in], w_in[:, m0 + 2 * mqk + 3 * gw:]], axis=1)
    wg_t = gate_cols.T.astype(BF16)

    p, gt = _in_proj(x, norm_pre_w.reshape(1, d), w_main, wg_t)
    g_rows, m_rows = _gates(gt, gdn_a_log.reshape(2 * h, 1), gdn_dt_bias.reshape(2 * h, 1),
                            mlstm_gate_bias.reshape(4 * h, 1))
    g_rows = jnp.transpose(g_rows, (0, 2, 1, 3))
    m_rows = jnp.transpose(m_rows, (0, 2, 1, 3))

    cq = mlstm_conv_w[:, :mqk].reshape(CONV_K, h, MLSTM_DQK)
    ck = mlstm_conv_w[:, mqk:].reshape(CONV_K, h, MLSTM_DQK)
    m_conv = jnp.concatenate([cq, ck], axis=2).reshape(CONV_K, gw)

    g_out = _gdn(p, g_rows, gdn_conv_w, gdn_norm_w.reshape(1, HEAD_DIM), col0=0)
    m_out = _mlstm(p, m_rows, m_conv, mlstm_norm_w.reshape(1, gw), col0=4 * h)
    return _out_proj(g_out, m_out, w_out.astype(BF16), x, norm_post_w.reshape(1, d))


def kernel(x, norm_pre_w, w_in, gdn_conv_w, gdn_a_log, gdn_dt_bias, gdn_norm_w,
           mlstm_conv_w, mlstm_gate_bias, mlstm_norm_w, w_out, norm_post_w):
    for layer in range(norm_pre_w.shape[0]):
        x = _layer(x, norm_pre_w[layer], w_in[layer], gdn_conv_w[layer], gdn_a_log[layer],
                   gdn_dt_bias[layer], gdn_norm_w[layer], mlstm_conv_w[layer],
                   mlstm_gate_bias[layer], mlstm_norm_w[layer], w_out[layer], norm_post_w[layer])
    return x
```

```python
import functools

import jax
import jax.numpy as jnp
import numpy as np
from jax import lax
from jax.experimental import pallas as pl
from jax.experimental.pallas import tpu as pltpu

F32 = jnp.float32
BF16 = jnp.bfloat16

N_HEADS = 8
HEAD_DIM = 128
MLSTM_DQK = 64
CONV_K = 5
CONV_HALO = 8
SCAN_CHUNK = 128
GATE_ROWS = 16
GDN_PREP_UNROLL = 8
MLSTM_UNROLL = 4
GATES_UNROLL = 4
RMS_EPS = 1e-6
L2_EPS = 1e-6
NEG_BIG = -1e30
VMEM_LIMIT_BYTES = 56 * 1024 * 1024


def _dot(a, b):
    return jnp.dot(a, b, preferred_element_type=F32)


def _dot_exact(a, b):
    return jnp.dot(a, b, preferred_element_type=F32, precision=lax.Precision.HIGHEST)


def _softplus(x):
    return jnp.maximum(x, 0.0) + jnp.log1p(jnp.exp(-jnp.abs(x)))


def _sigmoid(x):
    return 0.5 * jnp.tanh(0.5 * x) + 0.5


def _silu(x):
    return x * _sigmoid(x)


def _inproj_kernel(x_ref, npw_ref, w_ref, wg_ref, p_ref, gt_ref, h_scr, *, n_sub):
    j = pl.program_id(2)

    @pl.when(j == 0)
    def _():
        x = x_ref[0]
        ms = jnp.mean(x * x, axis=-1, keepdims=True)
        h = (x * lax.rsqrt(ms + RMS_EPS) * npw_ref[...]).astype(BF16)
        h_scr[...] = h
        gt_ref[0] = lax.dot_general(wg_ref[...], h, (((1,), (1,)), ((), ())),
                                    preferred_element_type=F32)

    h = h_scr[...]
    for c in range(n_sub // 2):
        acc = _dot(h, w_ref[:, c * 256:(c + 1) * 256])
        p_ref[0, 2 * c] = acc[:, :128]
        p_ref[0, 2 * c + 1] = acc[:, 128:]


def _in_proj(x, npw, w_main, wg_t, *, tm=1024, tn=2048):
    b, s, d = x.shape
    n_blocks = w_main.shape[1] // 128
    n_sub = tn // 128
    return pl.pallas_call(
        functools.partial(_inproj_kernel, n_sub=n_sub),
        grid=(b, s // tm, w_main.shape[1] // tn),
        in_specs=[
            pl.BlockSpec((1, tm, d), lambda bi, si, j: (bi, si, 0)),
            pl.BlockSpec((1, d), lambda bi, si, j: (0, 0)),
            pl.BlockSpec((d, tn), lambda bi, si, j: (0, j)),
            pl.BlockSpec(wg_t.shape, lambda bi, si, j: (0, 0)),
        ],
        out_specs=[
            pl.BlockSpec((1, n_sub, tm, 128), lambda bi, si, j: (bi, j, si, 0)),
            pl.BlockSpec((1, wg_t.shape[0], tm), lambda bi, si, j: (bi, 0, si)),
        ],
        out_shape=[
            jax.ShapeDtypeStruct((b, n_blocks, s, 128), F32),
            jax.ShapeDtypeStruct((b, wg_t.shape[0], s), F32),
        ],
        scratch_shapes=[pltpu.VMEM((tm, d), BF16)],
        compiler_params=pltpu.CompilerParams(
            dimension_semantics=("arbitrary", "arbitrary", "arbitrary"),
            vmem_limit_bytes=VMEM_LIMIT_BYTES),
        name="in_proj",
    )(x, npw, w_main, wg_t)


def _gates_kernel(gt_ref, alog_ref, dtb_ref, gb_ref, gout_ref, mout_ref, *, seq):
    nc = seq // SCAN_CHUNK
    ch = SCAN_CHUNK
    nh = N_HEADS
    ii = lax.broadcasted_iota(jnp.int32, (ch, ch), 0)
    jj = lax.broadcasted_iota(jnp.int32, (ch, ch), 1)
    upper = (ii <= jj).astype(F32)
    lower = (ii >= jj).astype(F32)
    lane = lax.broadcasted_iota(jnp.int32, (nh, ch), 1)
    neg_a = -jnp.exp(alog_ref[...])
    dtb = dtb_ref[...]
    gb = gb_ref[...]
    zeros8 = jnp.zeros((nh, ch), F32)

    def bcast(col):
        return jnp.broadcast_to(col, (nh, ch))

    def cummax(a, d):
        y = a
        for sh in (1, 2, 4, 8, 16, 32, 64):
            if d == 0:
                y = jnp.maximum(y, jnp.where(lane >= sh, pltpu.roll(y, sh, 1), NEG_BIG))
            else:
                y = jnp.maximum(y, jnp.where(lane < ch - sh, pltpu.roll(y, ch - sh, 1), NEG_BIG))
        return y

    def local_part(c):
        sl = pl.ds(pl.multiple_of(c * ch, ch), ch)
        g = neg_a * _softplus(gt_ref[0, 0:2 * nh, sl] + dtb)
        beta = _sigmoid(gt_ref[0, 2 * nh:4 * nh, sl])
        ib = gt_ref[0, 4 * nh:6 * nh, sl] + gb[0:2 * nh]
        lf = -_softplus(-(gt_ref[0, 6 * nh:8 * nh, sl] + gb[2 * nh:4 * nh]))
        stacked = jnp.concatenate([g, lf], axis=0)
        prefix = _dot_exact(stacked, upper)
        suffix = _dot_exact(stacked, lower)
        for d in (0, 1):
            rows = slice(nh * d, nh * d + nh)
            cum = prefix if d == 0 else suffix
            gcum = cum[rows]
            gend = bcast(prefix[rows, ch - 1:ch])
            gout_ref[0, 3 * d + 0, :, sl] = gcum
            gout_ref[0, 3 * d + 1, :, sl] = jnp.exp(gend - gcum)
            gout_ref[0, 3 * d + 2, :, sl] = jnp.exp(gend)
            gout_ref[0, 6 + 2 * d, :, sl] = beta[rows]
            gout_ref[0, 7 + 2 * d, :, sl] = jnp.exp(gcum)
            f_local = cum[2 * nh + nh * d:3 * nh + nh * d]
            a_local = ib[rows] - f_local
            mout_ref[0, 3 * d + 0, :, sl] = a_local
            mout_ref[0, 6 + 3 * d, :, sl] = cummax(a_local, d)
            mout_ref[0, 8 + 3 * d, :, sl] = f_local
        for q in range(10, GATE_ROWS):
            gout_ref[0, q, :, sl] = zeros8
        for q in range(12, GATE_ROWS):
            mout_ref[0, q, :, sl] = zeros8

    def carry_part(c, carry, d):
        c_f, c_r = carry
        sl = pl.ds(pl.multiple_of(c * ch, ch), ch)
        a = mout_ref[0, 3 * d + 0, :, sl] - c_f
        r = jnp.maximum(mout_ref[0, 6 + 3 * d, :, sl] - c_f, c_r)
        fcum = mout_ref[0, 8 + 3 * d, :, sl] + c_f
        end = ch - 1 if d == 0 else 0
        r_end = bcast(r[:, end:end + 1])
        mout_ref[0, 3 * d + 0, :, sl] = a
        mout_ref[0, 3 * d + 1, :, sl] = jnp.exp(a - r_end)
        mout_ref[0, 3 * d + 2, :, sl] = jnp.exp(c_r - r_end)
        mout_ref[0, 6 + 3 * d, :, sl] = r
        mout_ref[0, 7 + 3 * d, :, sl] = jnp.exp(c_r - r)
        mout_ref[0, 8 + 3 * d, :, sl] = jnp.exp(-(fcum + r))
        return bcast(fcum[:, end:end + 1]), r_end

    def local_body(i, carry):
        for u in range(GATES_UNROLL):
            local_part(i * GATES_UNROLL + u)
        return carry

    def carry_body(i, carry):
        return carry_part(i, carry[0], 0), carry_part(nc - 1 - i, carry[1], 1)

    lax.fori_loop(0, nc // GATES_UNROLL, local_body, 0)
    lax.fori_loop(0, nc, carry_body, ((zeros8, zeros8), (zeros8, zeros8)))


def _gates(gt, alog, dtb, gb):
    b, n_rows, s = gt.shape
    assert (s // SCAN_CHUNK) % GATES_UNROLL == 0
    out_block = pl.BlockSpec((1, GATE_ROWS, N_HEADS, s), lambda bi: (bi, 0, 0, 0))
    out_shape = jax.ShapeDtypeStruct((b, GATE_ROWS, N_HEADS, s), F32)
    return pl.pallas_call(
        functools.partial(_gates_kernel, seq=s),
        grid=(b,),
        in_specs=[
            pl.BlockSpec((1, n_rows, s), lambda bi: (bi, 0, 0)),
            pl.BlockSpec(alog.shape, lambda bi: (0, 0)),
            pl.BlockSpec(dtb.shape, lambda bi: (0, 0)),
            pl.BlockSpec(gb.shape, lambda bi: (0, 0)),
        ],
        out_specs=[out_block, out_block],
        out_shape=[out_shape, out_shape],
        compiler_params=pltpu.CompilerParams(
            dimension_semantics=("arbitrary",), vmem_limit_bytes=VMEM_LIMIT_BYTES),
        name="gates",
    )(gt, alog, dtb, gb)


def _fill_padded(pad_ref, ref, seq):
    ch = SCAN_CHUNK
    zeros = jnp.zeros((CONV_HALO, pad_ref.shape[-1]), F32)
    pad_ref[0:CONV_HALO, :] = zeros
    pad_ref[seq + CONV_HALO:seq + 2 * CONV_HALO, :] = zeros

    def body(c, carry):
        t0 = pl.multiple_of(c * ch, ch)
        pad_ref[pl.ds(t0 + CONV_HALO, ch), :] = ref[0, 0, pl.ds(t0, ch), :]
        return carry

    lax.fori_loop(0, seq // ch // 4, _unrolled(body, 4), 0)


def _conv_silu(pad_ref, c, w):
    ch = SCAN_CHUNK
    base = c * ch + (CONV_HALO - CONV_K // 2)
    acc = pad_ref[pl.ds(base, ch), :] * w[0:1]
    for j in range(1, CONV_K):
        acc = acc + pad_ref[pl.ds(base + j, ch), :] * w[j:j + 1]
    return _silu(acc)


def _gate_cols(rows):
    ch = SCAN_CHUNK
    return jnp.concatenate([rows, jnp.zeros((ch - GATE_ROWS, ch), F32)], axis=0).T


def _unrolled(body, factor):
    def wrapped(i, carry):
        for u in range(factor):
            carry = body(i * factor + u, carry)
        return carry
    return wrapped


def _level_masks():
    ch = SCAN_CHUNK
    i = np.arange(ch)[:, None]
    j = np.arange(ch)[None, :]
    n_levels = ch.bit_length() - 1
    out = np.zeros((2, n_levels, ch, ch), np.float32)
    for p in range(n_levels):
        pair = ((i ^ j) >> p) == 1
        out[0, p] = pair & (((i >> p) & 1) == 1)
        out[1, p] = pair & (((j >> p) & 1) == 1)
    return jnp.asarray(out, dtype=BF16)


def _gdn_kernel(q_ref, k_ref, v_ref, z_ref, row_ref, cwq_ref, cwk_ref, cwv_ref, nw_ref, lm_ref, out_ref,
                mq_scr, n_scr, o_scr, pad_scr, *, seq):
    ch = SCAN_CHUNK
    nc = seq // ch
    ii = lax.broadcasted_iota(jnp.int32, (ch, ch), 0)
    jj = lax.broadcasted_iota(jnp.int32, (ch, ch), 1)
    eye = (ii == jj).astype(F32)
    incl = (ii >= jj, ii <= jj)
    n_levels = ch.bit_length() - 1

    def prep_group(i, carry):
        chunks = []
        for u in range(GDN_PREP_UNROLL):
            c = i * GDN_PREP_UNROLL + u
            q0 = _conv_silu(pad_scr.at[0], c, cwq_ref[...])
            k0 = _conv_silu(pad_scr.at[1], c, cwk_ref[...])
            v = _conv_silu(pad_scr.at[2], c, cwv_ref[...])
            q = q0 * lax.rsqrt(jnp.sum(q0 * q0, axis=-1, keepdims=True) + L2_EPS) * (HEAD_DIM ** -0.5)
            k = k0 * lax.rsqrt(jnp.sum(k0 * k0, axis=-1, keepdims=True) + L2_EPS)
            kt = k.T
            chunks.append(dict(c=c, sl=pl.ds(pl.multiple_of(c * ch, ch), ch), q=q, k=k, v=v, kt=kt))
        for cd in chunks:
            ktb = cd["kt"].astype(BF16)
            cd["kk"] = _dot(cd["k"].astype(BF16), ktb)
            cd["qk"] = _dot(cd["q"].astype(BF16), ktb)
        chains = []
        for cd in chunks:
            rows = row_ref[0, 0, :, cd["sl"]]
            cols = _gate_cols(rows)
            for d in (0, 1):
                g_col = cols[:, 3 * d:3 * d + 1]
                beta = cols[:, 6 + 2 * d:7 + 2 * d]
                decay = jnp.exp(jnp.where(incl[d], g_col - rows[3 * d:3 * d + 1], NEG_BIG))
                ab = (cd["kk"] * decay * beta).astype(BF16)
                chains.append(dict(cd=cd, d=d, decay=decay, beta=beta, ab=ab,
                                   eg=cols[:, 7 + 2 * d:8 + 2 * d], et_row=rows[3 * d + 1:3 * d + 2],
                                   t=eye - (ab * lm_ref[d, 0]).astype(F32)))
        for p in range(1, n_levels):
            for cn in chains:
                cn["tb"] = cn["t"].astype(BF16)
                cn["x"] = _dot(cn["ab"] * lm_ref[cn["d"], p], cn["tb"]).astype(BF16)
            for cn in chains:
                cn["t"] = cn["t"] - _dot(cn["tb"], cn["x"])
        for cn in chains:
            cd, beta = cn["cd"], cn["beta"]
            rhs = jnp.concatenate([cd["v"] * beta, cd["k"] * (beta * cn["eg"])], axis=1).astype(BF16)
            cn["sol"] = _dot(cn["t"].astype(BF16), rhs).astype(BF16)
        for cn in chains:
            cd = cn["cd"]
            cn["ks"] = _dot((cd["kt"] * cn["et_row"]).astype(BF16), cn["sol"])
            cn["aw"] = _dot((cd["qk"] * cn["decay"]).astype(BF16), cn["sol"])
        for cn in chains:
            cd, d = cn["cd"], cn["d"]
            n_scr[d, cd["sl"], :] = cn["ks"][:, :HEAD_DIM]
            o_scr[d, cd["sl"], :] = cn["aw"][:, :HEAD_DIM]
            mq_scr[d, cd["c"], 0:ch, :] = (-cn["ks"][:, HEAD_DIM:]).astype(BF16)
            mq_scr[d, cd["c"], ch:2 * ch, :] = (cd["q"] * cn["eg"] - cn["aw"][:, HEAD_DIM:]).astype(BF16)
        return carry

    def scan_body(i, states):
        new_states = []
        for d, c in ((0, i), (1, nc - 1 - i)):
            sl = pl.ds(pl.multiple_of(c * ch, ch), ch)
            s = states[d]
            r = _dot(mq_scr[d, c], s.astype(BF16))
            o_scr[d, sl, :] = o_scr[d, sl, :] + r[ch:]
            chunk_decay = row_ref[0, 0, 3 * d + 2:3 * d + 3, sl]
            new_states.append(s * chunk_decay + r[:ch] + n_scr[d, sl, :])
        return tuple(new_states)

    def final_body(c, carry):
        sl = pl.ds(pl.multiple_of(c * ch, ch), ch)
        o = o_scr[0, sl, :] + o_scr[1, sl, :]
        y = o * lax.rsqrt(jnp.mean(o * o, axis=-1, keepdims=True) + RMS_EPS) * nw_ref[...]
        out_ref[0, sl, :] = (y * _silu(z_ref[0, 0, sl, :])).astype(out_ref.dtype)
        return carry

    for t, ref in enumerate((q_ref, k_ref, v_ref)):
        _fill_padded(pad_scr.at[t], ref, seq)
    lax.fori_loop(0, nc // GDN_PREP_UNROLL, prep_group, 0)
    zero_state = jnp.zeros((HEAD_DIM, HEAD_DIM), F32)
    lax.fori_loop(0, nc, scan_body, (zero_state, zero_state))
    lax.fori_loop(0, nc, final_body, 0)


def _gdn(p, rows, conv_w, norm_w, *, col0):
    b, _, s, _ = p.shape
    h = N_HEADS
    nc = s // SCAN_CHUNK
    assert nc % GDN_PREP_UNROLL == 0
    level_masks = _level_masks()

    def pblock(off):
        return pl.BlockSpec((1, 1, s, HEAD_DIM), lambda bi, hi: (bi, col0 + off + hi, 0, 0))

    def cblock(off):
        return pl.BlockSpec((CONV_K, HEAD_DIM), lambda bi, hi: (0, off + hi))

    return pl.pallas_call(
        functools.partial(_gdn_kernel, seq=s),
        grid=(b, h),
        in_specs=[
            pblock(0), pblock(h), pblock(2 * h), pblock(3 * h),
            pl.BlockSpec((1, 1, GATE_ROWS, s), lambda bi, hi: (bi, hi, 0, 0)),
            cblock(0), cblock(h), cblock(2 * h),
            pl.BlockSpec((1, HEAD_DIM), lambda bi, hi: (0, 0)),
            pl.BlockSpec(level_masks.shape, lambda bi, hi: (0, 0, 0, 0)),
        ],
        out_specs=pl.BlockSpec((1, s, HEAD_DIM), lambda bi, hi: (bi, 0, hi)),
        out_shape=jax.ShapeDtypeStruct((b, s, h * HEAD_DIM), BF16),
        scratch_shapes=[
            pltpu.VMEM((2, nc, 2 * SCAN_CHUNK, HEAD_DIM), BF16),
            pltpu.VMEM((2, s, HEAD_DIM), F32),
            pltpu.VMEM((2, s, HEAD_DIM), F32),
            pltpu.VMEM((3, s + 2 * CONV_HALO, HEAD_DIM), F32),
        ],
        compiler_params=pltpu.CompilerParams(
            dimension_semantics=("arbitrary", "arbitrary"), vmem_limit_bytes=VMEM_LIMIT_BYTES),
        name="gdn",
    )(p, p, p, p, rows, conv_w, conv_w, conv_w, norm_w, level_masks)


def _mlstm_kernel(qk_ref, v_ref, o_ref, z_ref, row_ref, cw_ref, nw_ref, out_ref,
                  qm_scr, kt_scr, qkm_scr, h_scr, pad_scr, *, seq):
    ch = SCAN_CHUNK
    nc = seq // ch
    ii = lax.broadcasted_iota(jnp.int32, (ch, ch), 0)
    jj = lax.broadcasted_iota(jnp.int32, (ch, ch), 1)
    incl = (ii >= jj, ii <= jj)
    ones = jnp.ones((ch, HEAD_DIM), BF16)

    def prep_body(c, carry):
        sl = pl.ds(pl.multiple_of(c * ch, ch), ch)
        t = _conv_silu(pad_scr, c, cw_ref[...])
        qm = jnp.where(jj < MLSTM_DQK, t, 0.0).astype(BF16)
        tt = t.T
        kt = jnp.concatenate([tt[MLSTM_DQK:], jnp.zeros((ch - MLSTM_DQK, ch), F32)], axis=0)
        ktb = (kt * (MLSTM_DQK ** -0.5)).astype(BF16)
        qm_scr[sl, :] = qm
        kt_scr[:, sl] = ktb
        qkm_scr[sl, :] = _dot(qm, ktb)
        return carry

    def scan_body(i, states):
        new_states = []
        for d, c in ((0, i), (1, nc - 1 - i)):
            sl = pl.ds(pl.multiple_of(c * ch, ch), ch)
            st = states[d]
            rows = row_ref[0, 0, :, sl]
            cols = _gate_cols(rows)
            a_row = rows[3 * d:3 * d + 1]
            awk_row = rows[3 * d + 1:3 * d + 2]
            dec_row = rows[3 * d + 2:3 * d + 3]
            r_col = cols[:, 6 + 3 * d:7 + 3 * d]
            inter = cols[:, 7 + 3 * d:8 + 3 * d]
            enm = cols[:, 8 + 3 * d:9 + 3 * d]
            vaug = jnp.concatenate([v_ref[0, 0, sl, :].astype(BF16), ones], axis=1)
            wt = jnp.exp(jnp.where(incl[d], a_row - r_col, NEG_BIG)) * qkm_scr[sl, :]
            nd = inter * _dot(qm_scr[sl, :], st.astype(BF16)) + _dot(wt.astype(BF16), vaug)
            num = nd[:, :HEAD_DIM]
            den = nd[:, HEAD_DIM:]
            h_scr[d, sl, :] = num / jnp.maximum(jnp.abs(den), enm)
            wk = (kt_scr[:, sl].astype(F32) * awk_row).astype(BF16)
            dec = jnp.concatenate([dec_row, dec_row], axis=1)
            new_states.append(st * dec + _dot(wk, vaug))
        return tuple(new_states)

    def final_body(c, carry):
        sl = pl.ds(pl.multiple_of(c * ch, ch), ch)
        hh = h_scr[0, sl, :] + h_scr[1, sl, :]
        y = hh * lax.rsqrt(jnp.mean(hh * hh, axis=-1, keepdims=True) + RMS_EPS) * nw_ref[...]
        y = y * _sigmoid(o_ref[0, 0, sl, :]) * _silu(z_ref[0, 0, sl, :])
        out_ref[0, sl, :] = y.astype(out_ref.dtype)
        return carry

    n_iter = nc // MLSTM_UNROLL
    _fill_padded(pad_scr, qk_ref, seq)
    lax.fori_loop(0, n_iter, _unrolled(prep_body, MLSTM_UNROLL), 0)
    zero_state = jnp.zeros((ch, 2 * HEAD_DIM), F32)
    lax.fori_loop(0, n_iter, _unrolled(scan_body, MLSTM_UNROLL), (zero_state, zero_state))
    lax.fori_loop(0, n_iter, _unrolled(final_body, MLSTM_UNROLL), 0)


def _mlstm(p, rows, conv_w, norm_w, *, col0):
    b, _, s, _ = p.shape
    h = N_HEADS
    assert (s // SCAN_CHUNK) % MLSTM_UNROLL == 0

    def pblock(off):
        return pl.BlockSpec((1, 1, s, HEAD_DIM), lambda bi, hi: (bi, col0 + off + hi, 0, 0))

    return pl.pallas_call(
        functools.partial(_mlstm_kernel, seq=s),
        grid=(b, h),
        in_specs=[
            pblock(0), pblock(h), pblock(2 * h), pblock(3 * h),
            pl.BlockSpec((1, 1, GATE_ROWS, s), lambda bi, hi: (bi, hi, 0, 0)),
            pl.BlockSpec((CONV_K, HEAD_DIM), lambda bi, hi: (0, hi)),
            pl.BlockSpec((1, HEAD_DIM), lambda bi, hi: (0, hi)),
        ],
        out_specs=pl.BlockSpec((1, s, HEAD_DIM), lambda bi, hi: (bi, 0, hi)),
        out_shape=jax.ShapeDtypeStruct((b, s, h * HEAD_DIM), BF16),
        scratch_shapes=[
            pltpu.VMEM((s, HEAD_DIM), BF16),
            pltpu.VMEM((HEAD_DIM, s), BF16),
            pltpu.VMEM((s, SCAN_CHUNK), F32),
            pltpu.VMEM((2, s, HEAD_DIM), F32),
            pltpu.VMEM((s + 2 * CONV_HALO, HEAD_DIM), F32),
        ],
        compiler_params=pltpu.CompilerParams(
            dimension_semantics=("arbitrary", "arbitrary"), vmem_limit_bytes=VMEM_LIMIT_BYTES),
        name="mlstm",
    )(p, p, p, p, rows, conv_w, norm_w)


def _outproj_kernel(g_ref, m_ref, wo_ref, x_ref, npw_ref, y_ref, *, width):
    mixed = _dot(g_ref[0], wo_ref[:width]) + _dot(m_ref[0], wo_ref[width:])
    ms = jnp.mean(mixed * mixed, axis=-1, keepdims=True)
    y_ref[0] = x_ref[0] + mixed * lax.rsqrt(ms + RMS_EPS) * npw_ref[...]


def _out_proj(g, m, w_out, x, npw, *, tm=512):
    b, s, d = x.shape
    width = g.shape[-1]
    return pl.pallas_call(
        functools.partial(_outproj_kernel, width=width),
        grid=(b, s // tm),
        in_specs=[
            pl.BlockSpec((1, tm, width), lambda bi, si: (bi, si, 0)),
            pl.BlockSpec((1, tm, width), lambda bi, si: (bi, si, 0)),
            pl.BlockSpec(w_out.shape, lambda bi, si: (0, 0)),
            pl.BlockSpec((1, tm, d), lambda bi, si: (bi, si, 0)),
            pl.BlockSpec((1, d), lambda bi, si: (0, 0)),
        ],
        out_specs=pl.BlockSpec((1, tm, d), lambda bi, si: (bi, si, 0)),
        out_shape=jax.ShapeDtypeStruct((b, s, d), F32),
        compiler_params=pltpu.CompilerParams(
            dimension_semantics=("arbitrary", "arbitrary"), vmem_limit_bytes=VMEM_LIMIT_BYTES),
        name="out_proj",
    )(g, m, w_out, x, npw)


def _layer(x, norm_pre_w, w_in, gdn_conv_w, gdn_a_log, gdn_dt_bias, gdn_norm_w,
           mlstm_conv_w, mlstm_gate_bias, mlstm_norm_w, w_out, norm_post_w):
    d = x.shape[-1]
    h = N_HEADS
    gw = h * HEAD_DIM
    mqk = h * MLSTM_DQK
    gdn_in = 4 * gw + 4 * h
    m0 = gdn_in
    mq = w_in[:, m0:m0 + mqk].reshape(d, h, MLSTM_DQK)
    mk = w_in[:, m0 + mqk:m0 + 2 * mqk].reshape(d, h, MLSTM_DQK)
    m_qk = jnp.concatenate([mq, mk], axis=2).reshape(d, gw)
    m_rest = w_in[:, m0 + 2 * mqk:m0 + 2 * mqk + 3 * gw]
    w_main = jnp.concatenate([w_in[:, :4 * gw], m_qk, m_rest], axis=1).astype(BF16)
    gate_cols = jnp.concatenate([w_in[:, 4 * gw:gdn_in], w_in[:, m0 + 2 * mqk + 3 * gw:]], axis=1)
    wg_t = gate_cols.T.astype(BF16)

    p, gt = _in_proj(x, norm_pre_w.reshape(1, d), w_main, wg_t)
    g_rows, m_rows = _gates(gt, gdn_a_log.reshape(2 * h, 1), gdn_dt_bias.reshape(2 * h, 1),
                            mlstm_gate_bias.reshape(4 * h, 1))
    g_rows = jnp.transpose(g_rows, (0, 2, 1, 3))
    m_rows = jnp.transpose(m_rows, (0, 2, 1, 3))

    cq = mlstm_conv_w[:, :mqk].reshape(CONV_K, h, MLSTM_DQK)
    ck = mlstm_conv_w[:, mqk:].reshape(CONV_K, h, MLSTM_DQK)
    m_conv = jnp.concatenate([cq, ck], axis=2).reshape(CONV_K, gw)

    g_out = _gdn(p, g_rows, gdn_conv_w, gdn_norm_w.reshape(1, HEAD_DIM), col0=0)
    m_out = _mlstm(p, m_rows, m_conv, mlstm_norm_w.reshape(1, gw), col0=4 * h)
    return _out_proj(g_out, m_out, w_out.astype(BF16), x, norm_post_w.reshape(1, d))


def kernel(x, norm_pre_w, w_in, gdn_conv_w, gdn_a_log, gdn_dt_bias, gdn_norm_w,
           mlstm_conv_w, mlstm_gate_bias, mlstm_norm_w, w_out, norm_post_w):
    for layer in range(norm_pre_w.shape[0]):
        x = _layer(x, norm_pre_w[layer], w_in[layer], gdn_conv_w[layer], gdn_a_log[layer],
                   gdn_dt_bias[layer], gdn_norm_w[layer], mlstm_conv_w[layer],
                   mlstm_gate_bias[layer], mlstm_norm_w[layer], w_out[layer], norm_post_w[layer])
    return x
```

```python
import functools

import jax
import jax.numpy as jnp
import numpy as np
from jax import lax
from jax.experimental import pallas as pl
from jax.experimental.pallas import tpu as pltpu

F32 = jnp.float32
BF16 = jnp.bfloat16

N_HEADS = 8
HEAD_DIM = 128
MLSTM_DQK = 64
CONV_K = 5
CONV_HALO = 8
SCAN_CHUNK = 128
GATE_ROWS = 16
GDN_PREP_UNROLL = 8
MLSTM_UNROLL = 4
GATES_UNROLL = 4
RMS_EPS = 1e-6
L2_EPS = 1e-6
NEG_BIG = -1e30
VMEM_LIMIT_BYTES = 56 * 1024 * 1024


def _dot(a, b):
    return jnp.dot(a, b, preferred_element_type=F32)


def _dot_exact(a, b):
    return jnp.dot(a, b, preferred_element_type=F32, precision=lax.Precision.HIGHEST)


def _softplus(x):
    return jnp.maximum(x, 0.0) + jnp.log1p(jnp.exp(-jnp.abs(x)))


def _sigmoid(x):
    return 0.5 * jnp.tanh(0.5 * x) + 0.5


def _silu(x):
    return x * _sigmoid(x)


def _inproj_kernel(x_ref, npw_ref, w_ref, wg_ref, p_ref, gt_ref, h_scr, *, n_sub):
    j = pl.program_id(2)

    @pl.when(j == 0)
    def _():
        x = x_ref[0]
        ms = jnp.mean(x * x, axis=-1, keepdims=True)
        h = (x * lax.rsqrt(ms + RMS_EPS) * npw_ref[...]).astype(BF16)
        h_scr[...] = h
        gt_ref[0] = lax.dot_general(wg_ref[...], h, (((1,), (1,)), ((), ())),
                                    preferred_element_type=F32)

    h = h_scr[...]
    for c in range(n_sub // 2):
        acc = _dot(h, w_ref[:, c * 256:(c + 1) * 256])
        p_ref[0, 2 * c] = acc[:, :128]
        p_ref[0, 2 * c + 1] = acc[:, 128:]


def _in_proj(x, npw, w_main, wg_t, *, tm=1024, tn=2048):
    b, s, d = x.shape
    n_blocks = w_main.shape[1] // 128
    n_sub = tn // 128
    return pl.pallas_call(
        functools.partial(_inproj_kernel, n_sub=n_sub),
        grid=(b, s // tm, w_main.shape[1] // tn),
        in_specs=[
            pl.BlockSpec((1, tm, d), lambda bi, si, j: (bi, si, 0)),
            pl.BlockSpec((1, d), lambda bi, si, j: (0, 0)),
            pl.BlockSpec((d, tn), lambda bi, si, j: (0, j)),
            pl.BlockSpec(wg_t.shape, lambda bi, si, j: (0, 0)),
        ],
        out_specs=[
            pl.BlockSpec((1, n_sub, tm, 128), lambda bi, si, j: (bi, j, si, 0)),
            pl.BlockSpec((1, wg_t.shape[0], tm), lambda bi, si, j: (bi, 0, si)),
        ],
        out_shape=[
            jax.ShapeDtypeStruct((b, n_blocks, s, 128), F32),
            jax.ShapeDtypeStruct((b, wg_t.shape[0], s), F32),
        ],
        scratch_shapes=[pltpu.VMEM((tm, d), BF16)],
        compiler_params=pltpu.CompilerParams(
            dimension_semantics=("arbitrary", "arbitrary", "arbitrary"),
            vmem_limit_bytes=VMEM_LIMIT_BYTES),
        name="in_proj",
    )(x, npw, w_main, wg_t)


def _gates_kernel(gt_ref, alog_ref, dtb_ref, gb_ref, gout_ref, mout_ref, *, seq):
    nc = seq // SCAN_CHUNK
    ch = SCAN_CHUNK
    nh = N_HEADS
    ii = lax.broadcasted_iota(jnp.int32, (ch, ch), 0)
    jj = lax.broadcasted_iota(jnp.int32, (ch, ch), 1)
    upper = (ii <= jj).astype(F32)
    lower = (ii >= jj).astype(F32)
    lane = lax.broadcasted_iota(jnp.int32, (nh, ch), 1)
    neg_a = -jnp.exp(alog_ref[...])
    dtb = dtb_ref[...]
    gb = gb_ref[...]
    zeros8 = jnp.zeros((nh, ch), F32)

    def bcast(col):
        return jnp.broadcast_to(col, (nh, ch))

    def cummax(a, d):
        y = a
        for sh in (1, 2, 4, 8, 16, 32, 64):
            if d == 0:
                y = jnp.maximum(y, jnp.where(lane >= sh, pltpu.roll(y, sh, 1), NEG_BIG))
            else:
                y = jnp.maximum(y, jnp.where(lane < ch - sh, pltpu.roll(y, ch - sh, 1), NEG_BIG))
        return y

    def local_part(c):
        sl = pl.ds(pl.multiple_of(c * ch, ch), ch)
        g = neg_a * _softplus(gt_ref[0, 0:2 * nh, sl] + dtb)
        beta = _sigmoid(gt_ref[0, 2 * nh:4 * nh, sl])
        ib = gt_ref[0, 4 * nh:6 * nh, sl] + gb[0:2 * nh]
        lf = -_softplus(-(gt_ref[0, 6 * nh:8 * nh, sl] + gb[2 * nh:4 * nh]))
        stacked = jnp.concatenate([g, lf], axis=0)
        prefix = _dot_exact(stacked, upper)
        suffix = _dot_exact(stacked, lower)
        for d in (0, 1):
            rows = slice(nh * d, nh * d + nh)
            cum = prefix if d == 0 else suffix
            gcum = cum[rows]
            gend = bcast(prefix[rows, ch - 1:ch])
            gout_ref[0, 3 * d + 0, :, sl] = gcum
            gout_ref[0, 3 * d + 1, :, sl] = jnp.exp(gend - gcum)
            gout_ref[0, 3 * d + 2, :, sl] = jnp.exp(gend)
            gout_ref[0, 6 + 2 * d, :, sl] = beta[rows]
            gout_ref[0, 7 + 2 * d, :, sl] = jnp.exp(gcum)
            f_local = cum[2 * nh + nh * d:3 * nh + nh * d]
            a_local = ib[rows] - f_local
            mout_ref[0, 3 * d + 0, :, sl] = a_local
            mout_ref[0, 6 + 3 * d, :, sl] = cummax(a_local, d)
            mout_ref[0, 8 + 3 * d, :, sl] = f_local
        for q in range(10, GATE_ROWS):
            gout_ref[0, q, :, sl] = zeros8
        for q in range(12, GATE_ROWS):
            mout_ref[0, q, :, sl] = zeros8

    def carry_part(c, carry, d):
        c_f, c_r = carry
        sl = pl.ds(pl.multiple_of(c * ch, ch), ch)
        a = mout_ref[0, 3 * d + 0, :, sl] - c_f
        r = jnp.maximum(mout_ref[0, 6 + 3 * d, :, sl] - c_f, c_r)
        fcum = mout_ref[0, 8 + 3 * d, :, sl] + c_f
        end = ch - 1 if d == 0 else 0
        r_end = bcast(r[:, end:end + 1])
        mout_ref[0, 3 * d + 0, :, sl] = a
        mout_ref[0, 3 * d + 1, :, sl] = jnp.exp(a - r_end)
        mout_ref[0, 3 * d + 2, :, sl] = jnp.exp(c_r - r_end)
        mout_ref[0, 6 + 3 * d, :, sl] = r
        mout_ref[0, 7 + 3 * d, :, sl] = jnp.exp(c_r - r)
        mout_ref[0, 8 + 3 * d, :, sl] = jnp.exp(-(fcum + r))
        return bcast(fcum[:, end:end + 1]), r_end

    def local_body(i, carry):
        for u in range(GATES_UNROLL):
            local_part(i * GATES_UNROLL + u)
        return carry

    def carry_body(i, carry):
        return carry_part(i, carry[0], 0), carry_part(nc - 1 - i, carry[1], 1)

    lax.fori_loop(0, nc // GATES_UNROLL, local_body, 0)
    lax.fori_loop(0, nc, carry_body, ((zeros8, zeros8), (zeros8, zeros8)))


def _gates(gt, alog, dtb, gb):
    b, n_rows, s = gt.shape
    assert (s // SCAN_CHUNK) % GATES_UNROLL == 0
    out_block = pl.BlockSpec((1, GATE_ROWS, N_HEADS, s), lambda bi: (bi, 0, 0, 0))
    out_shape = jax.ShapeDtypeStruct((b, GATE_ROWS, N_HEADS, s), F32)
    return pl.pallas_call(
        functools.partial(_gates_kernel, seq=s),
        grid=(b,),
        in_specs=[
            pl.BlockSpec((1, n_rows, s), lambda bi: (bi, 0, 0)),
            pl.BlockSpec(alog.shape, lambda bi: (0, 0)),
            pl.BlockSpec(dtb.shape, lambda bi: (0, 0)),
            pl.BlockSpec(gb.shape, lambda bi: (0, 0)),
        ],
        out_specs=[out_block, out_block],
        out_shape=[out_shape, out_shape],
        compiler_params=pltpu.CompilerParams(
            dimension_semantics=("arbitrary",), vmem_limit_bytes=VMEM_LIMIT_BYTES),
        name="gates",
    )(gt, alog, dtb, gb)


def _fill_padded(pad_ref, ref, seq):
    ch = SCAN_CHUNK
    zeros = jnp.zeros((CONV_HALO, pad_ref.shape[-1]), F32)
    pad_ref[0:CONV_HALO, :] = zeros
    pad_ref[seq + CONV_HALO:seq + 2 * CONV_HALO, :] = zeros

    def body(c, carry):
        t0 = pl.multiple_of(c * ch, ch)
        pad_ref[pl.ds(t0 + CONV_HALO, ch), :] = ref[0, 0, pl.ds(t0, ch), :]
        return carry

    lax.fori_loop(0, seq // ch // 4, _unrolled(body, 4), 0)


def _conv_silu(pad_ref, c, w):
    ch = SCAN_CHUNK
    base = c * ch + (CONV_HALO - CONV_K // 2)
    acc = pad_ref[pl.ds(base, ch), :] * w[0:1]
    for j in range(1, CONV_K):
        acc = acc + pad_ref[pl.ds(base + j, ch), :] * w[j:j + 1]
    return _silu(acc)


def _gate_cols(rows):
    ch = SCAN_CHUNK
    return jnp.concatenate([rows, jnp.zeros((ch - GATE_ROWS, ch), F32)], axis=0).T


def _gate_rows_spec(seq):
    return pl.BlockSpec((1, GATE_ROWS, 1, 1, seq), lambda bi, hi: (bi, 0, hi, 0, 0))


def _unrolled(body, factor):
    def wrapped(i, carry):
        for u in range(factor):
            carry = body(i * factor + u, carry)
        return carry
    return wrapped


def _level_masks():
    ch = SCAN_CHUNK
    i = np.arange(ch)[:, None]
    j = np.arange(ch)[None, :]
    n_levels = ch.bit_length() - 1
    out = np.zeros((2, n_levels + 2, ch, ch), np.float32)
    for p in range(n_levels):
        pair = ((i ^ j) >> p) == 1
        out[0, p] = pair & (((i >> p) & 1) == 1)
        out[1, p] = pair & (((j >> p) & 1) == 1)
    out[0, n_levels] = i >= j
    out[1, n_levels] = i <= j
    out[0, n_levels + 1] = out[0, 0] + out[1, 0]
    return jnp.asarray(out, dtype=BF16)


def _gdn_kernel(q_ref, k_ref, v_ref, z_ref, row_ref, cwq_ref, cwk_ref, cwv_ref, nw_ref, lm_ref, out_ref,
                mq_scr, n_scr, o_scr, pad_scr, *, seq):
    ch = SCAN_CHUNK
    nc = seq // ch
    ii = lax.broadcasted_iota(jnp.int32, (ch, ch), 0)
    jj = lax.broadcasted_iota(jnp.int32, (ch, ch), 1)
    eye = (ii == jj).astype(F32)
    incl = (ii >= jj, ii <= jj)
    n_levels = ch.bit_length() - 1

    def prep_group(i, carry):
        chunks = []
        for u in range(GDN_PREP_UNROLL):
            c = i * GDN_PREP_UNROLL + u
            q0 = _conv_silu(pad_scr.at[0], c, cwq_ref[...])
            k0 = _conv_silu(pad_scr.at[1], c, cwk_ref[...])
            v = _conv_silu(pad_scr.at[2], c, cwv_ref[...])
            q = q0 * lax.rsqrt(jnp.sum(q0 * q0, axis=-1, keepdims=True) + L2_EPS) * (HEAD_DIM ** -0.5)
            k = k0 * lax.rsqrt(jnp.sum(k0 * k0, axis=-1, keepdims=True) + L2_EPS)
            kt = k.T
            chunks.append(dict(c=c, sl=pl.ds(pl.multiple_of(c * ch, ch), ch), q=q, k=k, v=v, kt=kt))
        for cd in chunks:
            ktb = cd["kt"].astype(BF16)
            cd["kk"] = _dot(cd["k"].astype(BF16), ktb)
            cd["qk"] = _dot(cd["q"].astype(BF16), ktb)
        chains = []
        for cd in chunks:
            rows = row_ref[0, :, 0, 0, cd["sl"]]
            cols = _gate_cols(rows)
            weight = None
            for d in (0, 1):
                g_col = cols[:, 3 * d:3 * d + 1]
                beta = cols[:, 6 + 2 * d:7 + 2 * d]
                decay = jnp.exp(jnp.where(incl[d], g_col - rows[3 * d:3 * d + 1], NEG_BIG))
                weight = decay * beta if weight is None else weight + decay * beta
                chains.append(dict(cd=cd, d=d, decay=decay, beta=beta,
                                   eg=cols[:, 7 + 2 * d:8 + 2 * d], et_row=rows[3 * d + 1:3 * d + 2]))
            cd["ab"] = (cd["kk"] * weight).astype(BF16)
            cd["t"] = eye - (cd["ab"] * lm_ref[0, n_levels + 1]).astype(F32)
        for p in range(1, n_levels):
            for cd in chunks:
                tb = cd["t"].astype(BF16)
                t_lo, t_up = tb * lm_ref[0, n_levels], tb * lm_ref[1, n_levels]
                l_both = jnp.concatenate([cd["ab"] * lm_ref[0, p], cd["ab"] * lm_ref[1, p]], axis=1)
                x = _dot(l_both, jnp.concatenate([t_lo, t_up], axis=0)).astype(BF16)
                cd["t_both"] = jnp.concatenate([t_lo, t_up], axis=1)
                cd["x_both"] = jnp.concatenate([x * lm_ref[0, p], x * lm_ref[1, p]], axis=0)
            for cd in chunks:
                cd["t"] = cd["t"] - _dot(cd["t_both"], cd["x_both"])
        for cn in chains:
            cd, beta = cn["cd"], cn["beta"]
            rhs = jnp.concatenate([cd["v"] * beta, cd["k"] * (beta * cn["eg"])], axis=1).astype(BF16)
            t_dir = cd["t"].astype(BF16) * lm_ref[cn["d"], n_levels]
            cn["sol"] = _dot(t_dir, rhs).astype(BF16)
        for cn in chains:
            cd = cn["cd"]
            cn["ks"] = _dot((cd["kt"] * cn["et_row"]).astype(BF16), cn["sol"])
            cn["aw"] = _dot((cd["qk"] * cn["decay"]).astype(BF16), cn["sol"])
        for cn in chains:
            cd, d = cn["cd"], cn["d"]
            n_scr[d, cd["sl"], :] = cn["ks"][:, :HEAD_DIM]
            o_scr[d, cd["sl"], :] = cn["aw"][:, :HEAD_DIM]
            mq_scr[d, cd["c"], 0:ch, :] = (-cn["ks"][:, HEAD_DIM:]).astype(BF16)
            mq_scr[d, cd["c"], ch:2 * ch, :] = (cd["q"] * cn["eg"] - cn["aw"][:, HEAD_DIM:]).astype(BF16)
        return carry

    def scan_body(i, states):
        steps = []
        for d, c in ((0, i), (1, nc - 1 - i)):
            steps.append((d, c, pl.ds(pl.multiple_of(c * ch, ch), ch), states[d], states[d].astype(BF16)))
        new_states = []
        for d, c, sl, s, sb in steps:
            chunk_decay = row_ref[0, 3 * d + 2:3 * d + 3, 0, 0, sl]
            new_states.append(s * chunk_decay + _dot(mq_scr[d, c, 0:ch, :], sb) + n_scr[d, sl, :])
        for d, c, sl, s, sb in steps:
            o_scr[d, sl, :] = o_scr[d, sl, :] + _dot(mq_scr[d, c, ch:2 * ch, :], sb)
        return tuple(new_states)

    def final_body(c, carry):
        sl = pl.ds(pl.multiple_of(c * ch, ch), ch)
        o = o_scr[0, sl, :] + o_scr[1, sl, :]
        y = o * lax.rsqrt(jnp.mean(o * o, axis=-1, keepdims=True) + RMS_EPS) * nw_ref[...]
        out_ref[0, sl, :] = (y * _silu(z_ref[0, 0, sl, :])).astype(out_ref.dtype)
        return carry

    for t, ref in enumerate((q_ref, k_ref, v_ref)):
        _fill_padded(pad_scr.at[t], ref, seq)
    lax.fori_loop(0, nc // GDN_PREP_UNROLL, prep_group, 0)
    zero_state = jnp.zeros((HEAD_DIM, HEAD_DIM), F32)
    lax.fori_loop(0, nc // 2, _unrolled(scan_body, 2), (zero_state, zero_state))
    lax.fori_loop(0, nc // 4, _unrolled(final_body, 4), 0)


def _gdn(p, rows, conv_w, norm_w, *, col0):
    b, _, s, _ = p.shape
    h = N_HEADS
    nc = s // SCAN_CHUNK
    assert nc % GDN_PREP_UNROLL == 0
    level_masks = _level_masks()

    def pblock(off):
        return pl.BlockSpec((1, 1, s, HEAD_DIM), lambda bi, hi: (bi, col0 + off + hi, 0, 0))

    def cblock(off):
        return pl.BlockSpec((CONV_K, HEAD_DIM), lambda bi, hi: (0, off + hi))

    return pl.pallas_call(
        functools.partial(_gdn_kernel, seq=s),
        grid=(b, h),
        in_specs=[
            pblock(0), pblock(h), pblock(2 * h), pblock(3 * h),
            _gate_rows_spec(s),
            cblock(0), cblock(h), cblock(2 * h),
            pl.BlockSpec((1, HEAD_DIM), lambda bi, hi: (0, 0)),
            pl.BlockSpec(level_masks.shape, lambda bi, hi: (0, 0, 0, 0)),
        ],
        out_specs=pl.BlockSpec((1, s, HEAD_DIM), lambda bi, hi: (bi, 0, hi)),
        out_shape=jax.ShapeDtypeStruct((b, s, h * HEAD_DIM), BF16),
        scratch_shapes=[
            pltpu.VMEM((2, nc, 2 * SCAN_CHUNK, HEAD_DIM), BF16),
            pltpu.VMEM((2, s, HEAD_DIM), F32),
            pltpu.VMEM((2, s, HEAD_DIM), F32),
            pltpu.VMEM((3, s + 2 * CONV_HALO, HEAD_DIM), F32),
        ],
        compiler_params=pltpu.CompilerParams(
            dimension_semantics=("arbitrary", "arbitrary"), vmem_limit_bytes=VMEM_LIMIT_BYTES),
        name="gdn",
    )(p, p, p, p, rows, conv_w, conv_w, conv_w, norm_w, level_masks)


def _mlstm_kernel(qk_ref, v_ref, o_ref, z_ref, row_ref, cw_ref, nw_ref, out_ref,
                  qm_scr, kt_scr, qkm_scr, h_scr, pad_scr, *, seq):
    ch = SCAN_CHUNK
    nc = seq // ch
    ii = lax.broadcasted_iota(jnp.int32, (ch, ch), 0)
    jj = lax.broadcasted_iota(jnp.int32, (ch, ch), 1)
    incl = (ii >= jj, ii <= jj)
    ones = jnp.ones((ch, HEAD_DIM), BF16)

    def prep_body(c, carry):
        sl = pl.ds(pl.multiple_of(c * ch, ch), ch)
        t = _conv_silu(pad_scr, c, cw_ref[...])
        qm = jnp.where(jj < MLSTM_DQK, t, 0.0).astype(BF16)
        tt = t.T
        kt = jnp.concatenate([tt[MLSTM_DQK:], jnp.zeros((ch - MLSTM_DQK, ch), F32)], axis=0)
        ktb = (kt * (MLSTM_DQK ** -0.5)).astype(BF16)
        qm_scr[sl, :] = qm
        kt_scr[:, sl] = ktb
        qkm_scr[sl, :] = _dot(qm, ktb)
        return carry

    def scan_group(i, states):
        items = []
        for u in range(MLSTM_UNROLL):
            step = i * MLSTM_UNROLL + u
            for d, c in ((0, step), (1, nc - 1 - step)):
                sl = pl.ds(pl.multiple_of(c * ch, ch), ch)
                items.append(dict(d=d, sl=sl, rows=row_ref[0, :, 0, 0, sl]))
        for it in items:
            it["cols"] = _gate_cols(it["rows"])
        for it in items:
            d, sl, rows, cols = it["d"], it["sl"], it["rows"], it["cols"]
            r_col = cols[:, 6 + 3 * d:7 + 3 * d]
            it["inter"] = cols[:, 7 + 3 * d:8 + 3 * d]
            it["enm"] = cols[:, 8 + 3 * d:9 + 3 * d]
            dec_row = rows[3 * d + 2:3 * d + 3]
            it["dec"] = jnp.concatenate([dec_row, dec_row], axis=1)
            it["vaug"] = jnp.concatenate([v_ref[0, 0, sl, :].astype(BF16), ones], axis=1)
            wt = jnp.exp(jnp.where(incl[d], rows[3 * d:3 * d + 1] - r_col, NEG_BIG)) * qkm_scr[sl, :]
            it["wt"] = wt.astype(BF16)
            awk = jnp.broadcast_to(rows[3 * d + 1:3 * d + 2], (ch, ch)).astype(BF16)
            it["wk"] = kt_scr[:, sl] * awk
        for it in items:
            it["intra"] = _dot(it["wt"], it["vaug"])
            it["update"] = _dot(it["wk"], it["vaug"])
        states = list(states)
        for it in items:
            d = it["d"]
            it["qs"] = _dot(qm_scr[it["sl"], :], states[d].astype(BF16))
            states[d] = states[d] * it["dec"] + it["update"]
        for it in items:
            nd = it["inter"] * it["qs"] + it["intra"]
            num = nd[:, :HEAD_DIM]
            den = nd[:, HEAD_DIM:]
            h_scr[it["d"], it["sl"], :] = num / jnp.maximum(jnp.abs(den), it["enm"])
        return tuple(states)

    def final_body(c, carry):
        sl = pl.ds(pl.multiple_of(c * ch, ch), ch)
        hh = h_scr[0, sl, :] + h_scr[1, sl, :]
        y = hh * lax.rsqrt(jnp.mean(hh * hh, axis=-1, keepdims=True) + RMS_EPS) * nw_ref[...]
        y = y * _sigmoid(o_ref[0, 0, sl, :]) * _silu(z_ref[0, 0, sl, :])
        out_ref[0, sl, :] = y.astype(out_ref.dtype)
        return carry

    n_iter = nc // MLSTM_UNROLL
    _fill_padded(pad_scr, qk_ref, seq)
    lax.fori_loop(0, n_iter, _unrolled(prep_body, MLSTM_UNROLL), 0)
    zero_state = jnp.zeros((ch, 2 * HEAD_DIM), F32)
    lax.fori_loop(0, n_iter, scan_group, (zero_state, zero_state))
    lax.fori_loop(0, n_iter, _unrolled(final_body, MLSTM_UNROLL), 0)


def _mlstm(p, rows, conv_w, norm_w, *, col0):
    b, _, s, _ = p.shape
    h = N_HEADS
    assert (s // SCAN_CHUNK) % MLSTM_UNROLL == 0

    def pblock(off):
        return pl.BlockSpec((1, 1, s, HEAD_DIM), lambda bi, hi: (bi, col0 + off + hi, 0, 0))

    return pl.pallas_call(
        functools.partial(_mlstm_kernel, seq=s),
        grid=(b, h),
        in_specs=[
            pblock(0), pblock(h), pblock(2 * h), pblock(3 * h),
            _gate_rows_spec(s),
            pl.BlockSpec((CONV_K, HEAD_DIM), lambda bi, hi: (0, hi)),
            pl.BlockSpec((1, HEAD_DIM), lambda bi, hi: (0, hi)),
        ],
        out_specs=pl.BlockSpec((1, s, HEAD_DIM), lambda bi, hi: (bi, 0, hi)),
        out_shape=jax.ShapeDtypeStruct((b, s, h * HEAD_DIM), BF16),
        scratch_shapes=[
            pltpu.VMEM((s, HEAD_DIM), BF16),
            pltpu.VMEM((HEAD_DIM, s), BF16),
            pltpu.VMEM((s, SCAN_CHUNK), F32),
            pltpu.VMEM((2, s, HEAD_DIM), F32),
            pltpu.VMEM((s + 2 * CONV_HALO, HEAD_DIM), F32),
        ],
        compiler_params=pltpu.CompilerParams(
            dimension_semantics=("arbitrary", "arbitrary"), vmem_limit_bytes=VMEM_LIMIT_BYTES),
        name="mlstm",
    )(p, p, p, p, rows, conv_w, norm_w)


def _outproj_kernel(g_ref, m_ref, wo_ref, x_ref, npw_ref, y_ref, *, width):
    mixed = _dot(g_ref[0], wo_ref[:width]) + _dot(m_ref[0], wo_ref[width:])
    ms = jnp.mean(mixed * mixed, axis=-1, keepdims=True)
    y_ref[0] = x_ref[0] + mixed * lax.rsqrt(ms + RMS_EPS) * npw_ref[...]


def _out_proj(g, m, w_out, x, npw, *, tm=512):
    b, s, d = x.shape
    width = g.shape[-1]
    return pl.pallas_call(
        functools.partial(_outproj_kernel, width=width),
        grid=(b, s // tm),
        in_specs=[
            pl.BlockSpec((1, tm, width), lambda bi, si: (bi, si, 0)),
            pl.BlockSpec((1, tm, width), lambda bi, si: (bi, si, 0)),
            pl.BlockSpec(w_out.shape, lambda bi, si: (0, 0)),
            pl.BlockSpec((1, tm, d), lambda bi, si: (bi, si, 0)),
            pl.BlockSpec((1, d), lambda bi, si: (0, 0)),
        ],
        out_specs=pl.BlockSpec((1, tm, d), lambda bi, si: (bi, si, 0)),
        out_shape=jax.ShapeDtypeStruct((b, s, d), F32),
        compiler_params=pltpu.CompilerParams(
            dimension_semantics=("arbitrary", "arbitrary"), vmem_limit_bytes=VMEM_LIMIT_BYTES),
        name="out_proj",
    )(g, m, w_out, x, npw)


def _layer(x, norm_pre_w, w_in, gdn_conv_w, gdn_a_log, gdn_dt_bias, gdn_norm_w,
           mlstm_conv_w, mlstm_gate_bias, mlstm_norm_w, w_out, norm_post_w):
    d = x.shape[-1]
    h = N_HEADS
    gw = h * HEAD_DIM
    mqk = h * MLSTM_DQK
    gdn_in = 4 * gw + 4 * h
    m0 = gdn_in
    mq = w_in[:, m0:m0 + mqk].reshape(d, h, MLSTM_DQK)
    mk = w_in[:, m0 + mqk:m0 + 2 * mqk].reshape(d, h, MLSTM_DQK)
    m_qk = jnp.concatenate([mq, mk], axis=2).reshape(d, gw)
    m_rest = w_in[:, m0 + 2 * mqk:m0 + 2 * mqk + 3 * gw]
    w_main = jnp.concatenate([w_in[:, :4 * gw], m_qk, m_rest], axis=1).astype(BF16)
    gate_cols = jnp.concatenate([w_in[:, 4 * gw:gdn_in], w_in[:, m0 + 2 * mqk + 3 * gw:]], axis=1)
    wg_t = gate_cols.T.astype(BF16)

    p, gt = _in_proj(x, norm_pre_w.reshape(1, d), w_main, wg_t)
    g_rows, m_rows = _gates(gt, gdn_a_log.reshape(2 * h, 1), gdn_dt_bias.reshape(2 * h, 1),
                            mlstm_gate_bias.reshape(4 * h, 1))
    g_rows = g_rows[:, :, :, None, :]
    m_rows = m_rows[:, :, :, None, :]

    cq = mlstm_conv_w[:, :mqk].reshape(CONV_K, h, MLSTM_DQK)
    ck = mlstm_conv_w[:, mqk:].reshape(CONV_K, h, MLSTM_DQK)
    m_conv = jnp.concatenate([cq, ck], axis=2).reshape(CONV_K, gw)

    g_out = _gdn(p, g_rows, gdn_conv_w, gdn_norm_w.reshape(1, HEAD_DIM), col0=0)
    m_out = _mlstm(p, m_rows, m_conv, mlstm_norm_w.reshape(1, gw), col0=4 * h)
    return _out_proj(g_out, m_out, w_out.astype(BF16), x, norm_post_w.reshape(1, d))


def kernel(x, norm_pre_w, w_in, gdn_conv_w, gdn_a_log, gdn_dt_bias, gdn_norm_w,
           mlstm_conv_w, mlstm_gate_bias, mlstm_norm_w, w_out, norm_post_w):
    for layer in range(norm_pre_w.shape[0]):
        x = _layer(x, norm_pre_w[layer], w_in[layer], gdn_conv_w[layer], gdn_a_log[layer],
                   gdn_dt_bias[layer], gdn_norm_w[layer], mlstm_conv_w[layer],
                   mlstm_gate_bias[layer], mlstm_norm_w[layer], w_out[layer], norm_post_w[layer])
    return x
```

```python
import functools

import jax
import jax.numpy as jnp
import numpy as np
from jax import lax
from jax.experimental import pallas as pl
from jax.experimental.pallas import tpu as pltpu

F32 = jnp.float32
BF16 = jnp.bfloat16

N_HEADS = 8
HEAD_DIM = 128
MLSTM_DQK = 64
CONV_K = 5
CONV_HALO = 8
SCAN_CHUNK = 128
GATE_ROWS = 16
GDN_PREP_UNROLL = 8
MLSTM_UNROLL = 4
GATES_UNROLL = 4
RMS_EPS = 1e-6
L2_EPS = 1e-6
NEG_BIG = -1e30
VMEM_LIMIT_BYTES = 56 * 1024 * 1024


def _dot(a, b):
    return jnp.dot(a, b, preferred_element_type=F32)


def _dot_exact(a, b):
    return jnp.dot(a, b, preferred_element_type=F32, precision=lax.Precision.HIGHEST)


def _softplus(x):
    return jnp.maximum(x, 0.0) + jnp.log1p(jnp.exp(-jnp.abs(x)))


def _sigmoid(x):
    return 0.5 * jnp.tanh(0.5 * x) + 0.5


def _silu(x):
    h = 0.5 * x
    return h * jnp.tanh(h) + h


def _inproj_kernel(x_ref, npw_ref, w_ref, wg_ref, p_ref, gt_ref, h_scr, *, n_sub):
    j = pl.program_id(2)

    @pl.when(j == 0)
    def _():
        x = x_ref[0]
        ms = jnp.mean(x * x, axis=-1, keepdims=True)
        h = (x * lax.rsqrt(ms + RMS_EPS) * npw_ref[...]).astype(BF16)
        h_scr[...] = h
        gt_ref[0] = lax.dot_general(wg_ref[...], h, (((1,), (1,)), ((), ())),
                                    preferred_element_type=F32)

    h = h_scr[...]
    for c in range(n_sub // 2):
        acc = _dot(h, w_ref[:, c * 256:(c + 1) * 256])
        p_ref[0, 2 * c] = acc[:, :128].astype(p_ref.dtype)
        p_ref[0, 2 * c + 1] = acc[:, 128:].astype(p_ref.dtype)


def _in_proj(x, npw, w_main, wg_t, *, tm=1024, tn=2048):
    b, s, d = x.shape
    n_blocks = w_main.shape[1] // 128
    n_sub = tn // 128
    return pl.pallas_call(
        functools.partial(_inproj_kernel, n_sub=n_sub),
        grid=(b, s // tm, w_main.shape[1] // tn),
        in_specs=[
            pl.BlockSpec((1, tm, d), lambda bi, si, j: (bi, si, 0)),
            pl.BlockSpec((1, d), lambda bi, si, j: (0, 0)),
            pl.BlockSpec((d, tn), lambda bi, si, j: (0, j)),
            pl.BlockSpec(wg_t.shape, lambda bi, si, j: (0, 0)),
        ],
        out_specs=[
            pl.BlockSpec((1, n_sub, tm, 128), lambda bi, si, j: (bi, j, si, 0)),
            pl.BlockSpec((1, wg_t.shape[0], tm), lambda bi, si, j: (bi, 0, si)),
        ],
        out_shape=[
            jax.ShapeDtypeStruct((b, n_blocks, s, 128), BF16),
            jax.ShapeDtypeStruct((b, wg_t.shape[0], s), F32),
        ],
        scratch_shapes=[pltpu.VMEM((tm, d), BF16)],
        compiler_params=pltpu.CompilerParams(
            dimension_semantics=("arbitrary", "arbitrary", "arbitrary"),
            vmem_limit_bytes=VMEM_LIMIT_BYTES),
        name="in_proj",
    )(x, npw, w_main, wg_t)


def _gates_kernel(gt_ref, alog_ref, dtb_ref, gb_ref, gout_ref, mout_ref, *, seq):
    nc = seq // SCAN_CHUNK
    ch = SCAN_CHUNK
    nh = N_HEADS
    ii = lax.broadcasted_iota(jnp.int32, (ch, ch), 0)
    jj = lax.broadcasted_iota(jnp.int32, (ch, ch), 1)
    upper = (ii <= jj).astype(F32)
    lower = (ii >= jj).astype(F32)
    lane = lax.broadcasted_iota(jnp.int32, (nh, ch), 1)
    neg_a = -jnp.exp(alog_ref[...])
    dtb = dtb_ref[...]
    gb = gb_ref[...]
    zeros8 = jnp.zeros((nh, ch), F32)

    def bcast(col):
        return jnp.broadcast_to(col, (nh, ch))

    def cummax(a, d):
        y = a
        for sh in (1, 2, 4, 8, 16, 32, 64):
            if d == 0:
                y = jnp.maximum(y, jnp.where(lane >= sh, pltpu.roll(y, sh, 1), NEG_BIG))
            else:
                y = jnp.maximum(y, jnp.where(lane < ch - sh, pltpu.roll(y, ch - sh, 1), NEG_BIG))
        return y

    def local_part(c):
        sl = pl.ds(pl.multiple_of(c * ch, ch), ch)
        g = neg_a * _softplus(gt_ref[0, 0:2 * nh, sl] + dtb)
        beta = _sigmoid(gt_ref[0, 2 * nh:4 * nh, sl])
        ib = gt_ref[0, 4 * nh:6 * nh, sl] + gb[0:2 * nh]
        lf = -_softplus(-(gt_ref[0, 6 * nh:8 * nh, sl] + gb[2 * nh:4 * nh]))
        stacked = jnp.concatenate([g, lf], axis=0)
        prefix = _dot_exact(stacked, upper)
        suffix = _dot_exact(stacked, lower)
        for d in (0, 1):
            rows = slice(nh * d, nh * d + nh)
            cum = prefix if d == 0 else suffix
            gcum = cum[rows]
            gend = bcast(prefix[rows, ch - 1:ch])
            gout_ref[0, 3 * d + 0, :, sl] = gcum
            gout_ref[0, 3 * d + 1, :, sl] = jnp.exp(gend - gcum)
            gout_ref[0, 3 * d + 2, :, sl] = jnp.exp(gend)
            gout_ref[0, 6 + 2 * d, :, sl] = beta[rows]
            gout_ref[0, 7 + 2 * d, :, sl] = jnp.exp(gcum)
            f_local = cum[2 * nh + nh * d:3 * nh + nh * d]
            a_local = ib[rows] - f_local
            mout_ref[0, 3 * d + 0, :, sl] = a_local
            mout_ref[0, 6 + 3 * d, :, sl] = cummax(a_local, d)
            mout_ref[0, 8 + 3 * d, :, sl] = f_local
        for q in range(10, GATE_ROWS):
            gout_ref[0, q, :, sl] = zeros8
        for q in range(12, GATE_ROWS):
            mout_ref[0, q, :, sl] = zeros8

    def carry_part(c, carry, d):
        c_f, c_r = carry
        sl = pl.ds(pl.multiple_of(c * ch, ch), ch)
        a = mout_ref[0, 3 * d + 0, :, sl] - c_f
        r = jnp.maximum(mout_ref[0, 6 + 3 * d, :, sl] - c_f, c_r)
        fcum = mout_ref[0, 8 + 3 * d, :, sl] + c_f
        end = ch - 1 if d == 0 else 0
        r_end = bcast(r[:, end:end + 1])
        mout_ref[0, 3 * d + 0, :, sl] = a
        mout_ref[0, 3 * d + 1, :, sl] = jnp.exp(a - r_end)
        mout_ref[0, 3 * d + 2, :, sl] = jnp.exp(c_r - r_end)
        mout_ref[0, 6 + 3 * d, :, sl] = r
        mout_ref[0, 7 + 3 * d, :, sl] = jnp.exp(c_r - r)
        mout_ref[0, 8 + 3 * d, :, sl] = jnp.exp(-(fcum + r))
        return bcast(fcum[:, end:end + 1]), r_end

    def local_body(i, carry):
        for u in range(GATES_UNROLL):
            local_part(i * GATES_UNROLL + u)
        return carry

    def carry_body(i, carry):
        return carry_part(i, carry[0], 0), carry_part(nc - 1 - i, carry[1], 1)

    lax.fori_loop(0, nc // GATES_UNROLL, local_body, 0)
    lax.fori_loop(0, nc, carry_body, ((zeros8, zeros8), (zeros8, zeros8)))


def _gates(gt, alog, dtb, gb):
    b, n_rows, s = gt.shape
    assert (s // SCAN_CHUNK) % GATES_UNROLL == 0
    out_block = pl.BlockSpec((1, GATE_ROWS, N_HEADS, s), lambda bi: (bi, 0, 0, 0))
    out_shape = jax.ShapeDtypeStruct((b, GATE_ROWS, N_HEADS, s), F32)
    return pl.pallas_call(
        functools.partial(_gates_kernel, seq=s),
        grid=(b,),
        in_specs=[
            pl.BlockSpec((1, n_rows, s), lambda bi: (bi, 0, 0)),
            pl.BlockSpec(alog.shape, lambda bi: (0, 0)),
            pl.BlockSpec(dtb.shape, lambda bi: (0, 0)),
            pl.BlockSpec(gb.shape, lambda bi: (0, 0)),
        ],
        out_specs=[out_block, out_block],
        out_shape=[out_shape, out_shape],
        compiler_params=pltpu.CompilerParams(
            dimension_semantics=("arbitrary",), vmem_limit_bytes=VMEM_LIMIT_BYTES),
        name="gates",
    )(gt, alog, dtb, gb)


def _fill_padded(pad_ref, ref, seq):
    ch = SCAN_CHUNK
    zeros = jnp.zeros((CONV_HALO, pad_ref.shape[-1]), F32)
    pad_ref[0:CONV_HALO, :] = zeros
    pad_ref[seq + CONV_HALO:seq + 2 * CONV_HALO, :] = zeros

    def body(c, carry):
        t0 = pl.multiple_of(c * ch, ch)
        pad_ref[pl.ds(t0 + CONV_HALO, ch), :] = ref[0, 0, pl.ds(t0, ch), :].astype(F32)
        return carry

    lax.fori_loop(0, seq // ch // 4, _unrolled(body, 4), 0)


def _conv_silu(pad_ref, c, w):
    ch = SCAN_CHUNK
    base = c * ch + (CONV_HALO - CONV_K // 2)
    acc = pad_ref[pl.ds(base, ch), :] * w[0:1]
    for j in range(1, CONV_K):
        acc = acc + pad_ref[pl.ds(base + j, ch), :] * w[j:j + 1]
    return _silu(acc)


def _gate_cols(rows):
    ch = SCAN_CHUNK
    return jnp.concatenate([rows, jnp.zeros((ch - GATE_ROWS, ch), F32)], axis=0).T


def _gate_rows_spec(seq):
    return pl.BlockSpec((1, GATE_ROWS, 1, 1, seq), lambda bi, hi: (bi, 0, hi, 0, 0))


def _unrolled(body, factor):
    def wrapped(i, carry):
        for u in range(factor):
            carry = body(i * factor + u, carry)
        return carry
    return wrapped


def _level_masks():
    ch = SCAN_CHUNK
    i = np.arange(ch)[:, None]
    j = np.arange(ch)[None, :]
    n_levels = ch.bit_length() - 1
    out = np.zeros((2, n_levels + 2, ch, ch), np.float32)
    for p in range(n_levels):
        pair = ((i ^ j) >> p) == 1
        out[0, p] = pair & (((i >> p) & 1) == 1)
        out[1, p] = pair & (((j >> p) & 1) == 1)
    out[0, n_levels] = i >= j
    out[1, n_levels] = i <= j
    out[0, n_levels + 1] = out[0, 0] + out[1, 0]
    return jnp.asarray(out, dtype=BF16)


def _gdn_kernel(q_ref, k_ref, v_ref, z_ref, row_ref, cwq_ref, cwk_ref, cwv_ref, nw_ref, lm_ref, out_ref,
                mq_scr, n_scr, o_scr, pad_scr, *, seq):
    ch = SCAN_CHUNK
    nc = seq // ch
    ii = lax.broadcasted_iota(jnp.int32, (ch, ch), 0)
    jj = lax.broadcasted_iota(jnp.int32, (ch, ch), 1)
    eye = (ii == jj).astype(F32)
    incl = (ii >= jj, ii <= jj)
    n_levels = ch.bit_length() - 1

    def prep_group(i, carry):
        chunks = []
        for u in range(GDN_PREP_UNROLL):
            c = i * GDN_PREP_UNROLL + u
            q0 = _conv_silu(pad_scr.at[0], c, cwq_ref[...])
            k0 = _conv_silu(pad_scr.at[1], c, cwk_ref[...])
            v = _conv_silu(pad_scr.at[2], c, cwv_ref[...])
            q = q0 * lax.rsqrt(jnp.sum(q0 * q0, axis=-1, keepdims=True) + L2_EPS) * (HEAD_DIM ** -0.5)
            k = k0 * lax.rsqrt(jnp.sum(k0 * k0, axis=-1, keepdims=True) + L2_EPS)
            kt = k.T
            chunks.append(dict(c=c, sl=pl.ds(pl.multiple_of(c * ch, ch), ch), q=q, k=k, v=v, kt=kt))
        for cd in chunks:
            ktb = cd["kt"].astype(BF16)
            cd["kk"] = _dot(cd["k"].astype(BF16), ktb)
            cd["qk"] = _dot(cd["q"].astype(BF16), ktb)
        chains = []
        for cd in chunks:
            rows = row_ref[0, :, 0, 0, cd["sl"]]
            cols = _gate_cols(rows)
            weight = None
            for d in (0, 1):
                g_col = cols[:, 3 * d:3 * d + 1]
                beta = cols[:, 6 + 2 * d:7 + 2 * d]
                decay = jnp.exp(jnp.where(incl[d], g_col - rows[3 * d:3 * d + 1], NEG_BIG))
                weight = decay * beta if weight is None else weight + decay * beta
                chains.append(dict(cd=cd, d=d, decay=decay, beta=beta,
                                   eg=cols[:, 7 + 2 * d:8 + 2 * d], et_row=rows[3 * d + 1:3 * d + 2]))
            cd["ab"] = (cd["kk"] * weight).astype(BF16)
            cd["t"] = eye - (cd["ab"] * lm_ref[0, n_levels + 1]).astype(F32)
        for p in range(1, n_levels):
            for cd in chunks:
                tb = cd["t"].astype(BF16)
                t_lo, t_up = tb * lm_ref[0, n_levels], tb * lm_ref[1, n_levels]
                l_both = jnp.concatenate([cd["ab"] * lm_ref[0, p], cd["ab"] * lm_ref[1, p]], axis=1)
                x = _dot(l_both, jnp.concatenate([t_lo, t_up], axis=0)).astype(BF16)
                cd["t_both"] = jnp.concatenate([t_lo, t_up], axis=1)
                cd["x_both"] = jnp.concatenate([x * lm_ref[0, p], x * lm_ref[1, p]], axis=0)
            for cd in chunks:
                cd["t"] = cd["t"] - _dot(cd["t_both"], cd["x_both"])
        for cn in chains:
            cd, beta = cn["cd"], cn["beta"]
            rhs = jnp.concatenate([cd["v"] * beta, cd["k"] * (beta * cn["eg"])], axis=1).astype(BF16)
            t_dir = cd["t"].astype(BF16) * lm_ref[cn["d"], n_levels]
            cn["sol"] = _dot(t_dir, rhs).astype(BF16)
        for cn in chains:
            cd = cn["cd"]
            cn["ks"] = _dot((cd["kt"] * cn["et_row"]).astype(BF16), cn["sol"])
            cn["aw"] = _dot((cd["qk"] * cn["decay"]).astype(BF16), cn["sol"])
        for cn in chains:
            cd, d = cn["cd"], cn["d"]
            n_scr[d, cd["sl"], :] = cn["ks"][:, :HEAD_DIM]
            o_scr[d, cd["sl"], :] = cn["aw"][:, :HEAD_DIM]
            mq_scr[d, cd["c"], 0:ch, :] = (-cn["ks"][:, HEAD_DIM:]).astype(BF16)
            mq_scr[d, cd["c"], ch:2 * ch, :] = (cd["q"] * cn["eg"] - cn["aw"][:, HEAD_DIM:]).astype(BF16)
        return carry

    def scan_body(i, states):
        steps = []
        for d, c in ((0, i), (1, nc - 1 - i)):
            steps.append((d, c, pl.ds(pl.multiple_of(c * ch, ch), ch), states[d], states[d].astype(BF16)))
        new_states = []
        for d, c, sl, s, sb in steps:
            chunk_decay = row_ref[0, 3 * d + 2:3 * d + 3, 0, 0, sl]
            new_states.append(s * chunk_decay + _dot(mq_scr[d, c, 0:ch, :], sb) + n_scr[d, sl, :])
        for d, c, sl, s, sb in steps:
            o_scr[d, sl, :] = o_scr[d, sl, :] + _dot(mq_scr[d, c, ch:2 * ch, :], sb)
        return tuple(new_states)

    def final_body(c, carry):
        sl = pl.ds(pl.multiple_of(c * ch, ch), ch)
        o = o_scr[0, sl, :] + o_scr[1, sl, :]
        y = o * lax.rsqrt(jnp.mean(o * o, axis=-1, keepdims=True) + RMS_EPS) * nw_ref[...]
        out_ref[0, sl, :] = (y * _silu(z_ref[0, 0, sl, :].astype(F32))).astype(out_ref.dtype)
        return carry

    for t, ref in enumerate((q_ref, k_ref, v_ref)):
        _fill_padded(pad_scr.at[t], ref, seq)
    lax.fori_loop(0, nc // GDN_PREP_UNROLL, prep_group, 0)
    zero_state = jnp.zeros((HEAD_DIM, HEAD_DIM), F32)
    lax.fori_loop(0, nc // 2, _unrolled(scan_body, 2), (zero_state, zero_state))
    lax.fori_loop(0, nc // 4, _unrolled(final_body, 4), 0)


def _gdn(p, rows, conv_w, norm_w, *, col0):
    b, _, s, _ = p.shape
    h = N_HEADS
    nc = s // SCAN_CHUNK
    assert nc % GDN_PREP_UNROLL == 0
    level_masks = _level_masks()

    def pblock(off):
        return pl.BlockSpec((1, 1, s, HEAD_DIM), lambda bi, hi: (bi, col0 + off + hi, 0, 0))

    def cblock(off):
        return pl.BlockSpec((CONV_K, HEAD_DIM), lambda bi, hi: (0, off + hi))

    return pl.pallas_call(
        functools.partial(_gdn_kernel, seq=s),
        grid=(b, h),
        in_specs=[
            pblock(0), pblock(h), pblock(2 * h), pblock(3 * h),
            _gate_rows_spec(s),
            cblock(0), cblock(h), cblock(2 * h),
            pl.BlockSpec((1, HEAD_DIM), lambda bi, hi: (0, 0)),
            pl.BlockSpec(level_masks.shape, lambda bi, hi: (0, 0, 0, 0)),
        ],
        out_specs=pl.BlockSpec((1, s, HEAD_DIM), lambda bi, hi: (bi, 0, hi)),
        out_shape=jax.ShapeDtypeStruct((b, s, h * HEAD_DIM), BF16),
        scratch_shapes=[
            pltpu.VMEM((2, nc, 2 * SCAN_CHUNK, HEAD_DIM), BF16),
            pltpu.VMEM((2, s, HEAD_DIM), F32),
            pltpu.VMEM((2, s, HEAD_DIM), F32),
            pltpu.VMEM((3, s + 2 * CONV_HALO, HEAD_DIM), F32),
        ],
        compiler_params=pltpu.CompilerParams(
            dimension_semantics=("arbitrary", "arbitrary"), vmem_limit_bytes=VMEM_LIMIT_BYTES),
        name="gdn",
    )(p, p, p, p, rows, conv_w, conv_w, conv_w, norm_w, level_masks)


def _mlstm_kernel(qk_ref, v_ref, o_ref, z_ref, row_ref, cw_ref, nw_ref, out_ref,
                  qm_scr, kt_scr, qkm_scr, cols_scr, h_scr, pad_scr, *, seq):
    ch = SCAN_CHUNK
    nc = seq // ch
    ii = lax.broadcasted_iota(jnp.int32, (ch, ch), 0)
    jj = lax.broadcasted_iota(jnp.int32, (ch, ch), 1)
    incl = (ii >= jj, ii <= jj)
    ones = jnp.ones((ch, HEAD_DIM), BF16)

    def prep_body(c, carry):
        sl = pl.ds(pl.multiple_of(c * ch, ch), ch)
        t = _conv_silu(pad_scr, c, cw_ref[...])
        qm = jnp.where(jj < MLSTM_DQK, t, 0.0).astype(BF16)
        tt = t.T
        kt = jnp.concatenate([tt[MLSTM_DQK:], jnp.zeros((ch - MLSTM_DQK, ch), F32)], axis=0)
        ktb = (kt * (MLSTM_DQK ** -0.5)).astype(BF16)
        qm_scr[sl, :] = qm
        kt_scr[:, sl] = ktb
        qkm_scr[sl, :] = _dot(qm, ktb)
        cols_scr[sl, :] = _gate_cols(row_ref[0, :, 0, 0, sl])
        return carry

    def scan_group(i, states):
        items = []
        for u in range(MLSTM_UNROLL):
            step = i * MLSTM_UNROLL + u
            for d, c in ((0, step), (1, nc - 1 - step)):
                sl = pl.ds(pl.multiple_of(c * ch, ch), ch)
                items.append(dict(d=d, sl=sl, rows=row_ref[0, :, 0, 0, sl]))
        for it in items:
            d, sl, rows, cols = it["d"], it["sl"], it["rows"], cols_scr[it["sl"], :]
            r_col = cols[:, 6 + 3 * d:7 + 3 * d]
            it["inter"] = cols[:, 7 + 3 * d:8 + 3 * d]
            it["enm"] = cols[:, 8 + 3 * d:9 + 3 * d]
            dec_row = rows[3 * d + 2:3 * d + 3]
            it["dec"] = jnp.concatenate([dec_row, dec_row], axis=1)
            it["vaug"] = jnp.concatenate([v_ref[0, 0, sl, :], ones], axis=1)
            wt = jnp.exp(jnp.where(incl[d], rows[3 * d:3 * d + 1] - r_col, NEG_BIG)) * qkm_scr[sl, :]
            it["wt"] = wt.astype(BF16)
            awk = jnp.broadcast_to(rows[3 * d + 1:3 * d + 2], (ch, ch)).astype(BF16)
            it["wk"] = kt_scr[:, sl] * awk
        for it in items:
            it["intra"] = _dot(it["wt"], it["vaug"])
            it["update"] = _dot(it["wk"], it["vaug"])
        states = list(states)
        for it in items:
            d = it["d"]
            it["qs"] = _dot(qm_scr[it["sl"], :], states[d].astype(BF16))
            states[d] = states[d] * it["dec"] + it["update"]
        for it in items:
            nd = it["inter"] * it["qs"] + it["intra"]
            num = nd[:, :HEAD_DIM]
            den = nd[:, HEAD_DIM:]
            h_scr[it["d"], it["sl"], :] = num / jnp.maximum(jnp.abs(den), it["enm"])
        return tuple(states)

    def final_body(c, carry):
        sl = pl.ds(pl.multiple_of(c * ch, ch), ch)
        hh = h_scr[0, sl, :] + h_scr[1, sl, :]
        y = hh * lax.rsqrt(jnp.mean(hh * hh, axis=-1, keepdims=True) + RMS_EPS) * nw_ref[...]
        y = y * _sigmoid(o_ref[0, 0, sl, :].astype(F32)) * _silu(z_ref[0, 0, sl, :].astype(F32))
        out_ref[0, sl, :] = y.astype(out_ref.dtype)
        return carry

    n_iter = nc // MLSTM_UNROLL
    _fill_padded(pad_scr, qk_ref, seq)
    lax.fori_loop(0, n_iter, _unrolled(prep_body, MLSTM_UNROLL), 0)
    zero_state = jnp.zeros((ch, 2 * HEAD_DIM), F32)
    lax.fori_loop(0, n_iter, scan_group, (zero_state, zero_state))
    lax.fori_loop(0, n_iter, _unrolled(final_body, MLSTM_UNROLL), 0)


def _mlstm(p, rows, conv_w, norm_w, *, col0):
    b, _, s, _ = p.shape
    h = N_HEADS
    assert (s // SCAN_CHUNK) % MLSTM_UNROLL == 0

    def pblock(off):
        return pl.BlockSpec((1, 1, s, HEAD_DIM), lambda bi, hi: (bi, col0 + off + hi, 0, 0))

    return pl.pallas_call(
        functools.partial(_mlstm_kernel, seq=s),
        grid=(b, h),
        in_specs=[
            pblock(0), pblock(h), pblock(2 * h), pblock(3 * h),
            _gate_rows_spec(s),
            pl.BlockSpec((CONV_K, HEAD_DIM), lambda bi, hi: (0, hi)),
            pl.BlockSpec((1, HEAD_DIM), lambda bi, hi: (0, hi)),
        ],
        out_specs=pl.BlockSpec((1, s, HEAD_DIM), lambda bi, hi: (bi, 0, hi)),
        out_shape=jax.ShapeDtypeStruct((b, s, h * HEAD_DIM), BF16),
        scratch_shapes=[
            pltpu.VMEM((s, HEAD_DIM), BF16),
            pltpu.VMEM((HEAD_DIM, s), BF16),
            pltpu.VMEM((s, SCAN_CHUNK), F32),
            pltpu.VMEM((s, SCAN_CHUNK), F32),
            pltpu.VMEM((2, s, HEAD_DIM), F32),
            pltpu.VMEM((s + 2 * CONV_HALO, HEAD_DIM), F32),
        ],
        compiler_params=pltpu.CompilerParams(
            dimension_semantics=("arbitrary", "arbitrary"), vmem_limit_bytes=VMEM_LIMIT_BYTES),
        name="mlstm",
    )(p, p, p, p, rows, conv_w, norm_w)


def _outproj_kernel(g_ref, m_ref, wo_ref, x_ref, npw_ref, y_ref, *, width):
    mixed = _dot(g_ref[0], wo_ref[:width]) + _dot(m_ref[0], wo_ref[width:])
    ms = jnp.mean(mixed * mixed, axis=-1, keepdims=True)
    y_ref[0] = x_ref[0] + mixed * lax.rsqrt(ms + RMS_EPS) * npw_ref[...]


def _out_proj(g, m, w_out, x, npw, *, tm=512):
    b, s, d = x.shape
    width = g.shape[-1]
    return pl.pallas_call(
        functools.partial(_outproj_kernel, width=width),
        grid=(b, s // tm),
        in_specs=[
            pl.BlockSpec((1, tm, width), lambda bi, si: (bi, si, 0)),
            pl.BlockSpec((1, tm, width), lambda bi, si: (bi, si, 0)),
            pl.BlockSpec(w_out.shape, lambda bi, si: (0, 0)),
            pl.BlockSpec((1, tm, d), lambda bi, si: (bi, si, 0)),
            pl.BlockSpec((1, d), lambda bi, si: (0, 0)),
        ],
        out_specs=pl.BlockSpec((1, tm, d), lambda bi, si: (bi, si, 0)),
        out_shape=jax.ShapeDtypeStruct((b, s, d), F32),
        compiler_params=pltpu.CompilerParams(
            dimension_semantics=("arbitrary", "arbitrary"), vmem_limit_bytes=VMEM_LIMIT_BYTES),
        name="out_proj",
    )(g, m, w_out, x, npw)


def _layer(x, norm_pre_w, w_in, gdn_conv_w, gdn_a_log, gdn_dt_bias, gdn_norm_w,
           mlstm_conv_w, mlstm_gate_bias, mlstm_norm_w, w_out, norm_post_w):
    d = x.shape[-1]
    h = N_HEADS
    gw = h * HEAD_DIM
    mqk = h * MLSTM_DQK
    gdn_in = 4 * gw + 4 * h
    m0 = gdn_in
    wb = w_in.astype(BF16)
    mq = wb[:, m0:m0 + mqk].reshape(d, h, MLSTM_DQK)
    mk = wb[:, m0 + mqk:m0 + 2 * mqk].reshape(d, h, MLSTM_DQK)
    m_qk = jnp.concatenate([mq, mk], axis=2).reshape(d, gw)
    m_rest = wb[:, m0 + 2 * mqk:m0 + 2 * mqk + 3 * gw]
    w_main = jnp.concatenate([wb[:, :4 * gw], m_qk, m_rest], axis=1)
    gate_cols = jnp.concatenate([wb[:, 4 * gw:gdn_in], wb[:, m0 + 2 * mqk + 3 * gw:]], axis=1)
    wg_t = gate_cols.T

    p, gt = _in_proj(x, norm_pre_w.reshape(1, d), w_main, wg_t)
    g_rows, m_rows = _gates(gt, gdn_a_log.reshape(2 * h, 1), gdn_dt_bias.reshape(2 * h, 1),
                            mlstm_gate_bias.reshape(4 * h, 1))
    g_rows = g_rows[:, :, :, None, :]
    m_rows = m_rows[:, :, :, None, :]

    cq = mlstm_conv_w[:, :mqk].reshape(CONV_K, h, MLSTM_DQK)
    ck = mlstm_conv_w[:, mqk:].reshape(CONV_K, h, MLSTM_DQK)
    m_conv = jnp.concatenate([cq, ck], axis=2).reshape(CONV_K, gw)

    g_out = _gdn(p, g_rows, gdn_conv_w, gdn_norm_w.reshape(1, HEAD_DIM), col0=0)
    m_out = _mlstm(p, m_rows, m_conv, mlstm_norm_w.reshape(1, gw), col0=4 * h)
    return _out_proj(g_out, m_out, w_out.astype(BF16), x, norm_post_w.reshape(1, d))


def kernel(x, norm_pre_w, w_in, gdn_conv_w, gdn_a_log, gdn_dt_bias, gdn_norm_w,
           mlstm_conv_w, mlstm_gate_bias, mlstm_norm_w, w_out, norm_post_w):
    for layer in range(norm_pre_w.shape[0]):
        x = _layer(x, norm_pre_w[layer], w_in[layer], gdn_conv_w[layer], gdn_a_log[layer],
                   gdn_dt_bias[layer], gdn_norm_w[layer], mlstm_conv_w[layer],
                   mlstm_gate_bias[layer], mlstm_norm_w[layer], w_out[layer], norm_post_w[layer])
    return x
```

```python
import functools

import jax
import jax.numpy as jnp
import numpy as np
from jax import lax
from jax.experimental import pallas as pl
from jax.experimental.pallas import tpu as pltpu

F32 = jnp.float32
BF16 = jnp.bfloat16

N_HEADS = 8
HEAD_DIM = 128
MLSTM_DQK = 64
CONV_K = 5
CONV_HALO = 8
SCAN_CHUNK = 128
GATE_ROWS = 16
GDN_PREP_UNROLL = 8
MLSTM_UNROLL = 4
GATES_UNROLL = 4
RMS_EPS = 1e-6
L2_EPS = 1e-6
NEG_BIG = -1e30
VMEM_LIMIT_BYTES = 56 * 1024 * 1024


def _dot(a, b):
    return jnp.dot(a, b, preferred_element_type=F32)


def _dot_exact(a, b):
    return jnp.dot(a, b, preferred_element_type=F32, precision=lax.Precision.HIGHEST)


def _softplus(x):
    return jnp.maximum(x, 0.0) + jnp.log1p(jnp.exp(-jnp.abs(x)))


def _sigmoid(x):
    return 0.5 * jnp.tanh(0.5 * x) + 0.5


def _silu(x):
    h = 0.5 * x
    return h * jnp.tanh(h) + h


def _inproj_kernel(x_ref, npw_ref, wa_ref, wb_ref, wg_ref, p_ref, gt_ref, h_scr, *, n_sub, n_first):
    j = pl.program_id(2)

    @pl.when(j == 0)
    def _():
        x = x_ref[0]
        ms = jnp.mean(x * x, axis=-1, keepdims=True)
        h = (x * lax.rsqrt(ms + RMS_EPS) * npw_ref[...]).astype(BF16)
        h_scr[...] = h
        gt_ref[0] = lax.dot_general(wg_ref[...], h, (((1,), (1,)), ((), ())),
                                    preferred_element_type=F32)

    def project(w_ref):
        h = h_scr[...]
        for c in range(n_sub // 2):
            acc = _dot(h, w_ref[:, c * 256:(c + 1) * 256])
            p_ref[0, 2 * c] = acc[:, :128].astype(p_ref.dtype)
            p_ref[0, 2 * c + 1] = acc[:, 128:].astype(p_ref.dtype)

    @pl.when(j < n_first)
    def _():
        project(wa_ref)

    @pl.when(j >= n_first)
    def _():
        project(wb_ref)


def _in_proj(x, npw, w_first, n_first_cols, w_rest, wg_t, *, tm=1024, tn=2048):
    b, s, d = x.shape
    n_first = n_first_cols // tn
    n_steps = n_first + w_rest.shape[1] // tn
    n_blocks = n_steps * tn // 128
    n_sub = tn // 128
    return pl.pallas_call(
        functools.partial(_inproj_kernel, n_sub=n_sub, n_first=n_first),
        grid=(b, s // tm, n_steps),
        in_specs=[
            pl.BlockSpec((1, tm, d), lambda bi, si, j: (bi, si, 0)),
            pl.BlockSpec((1, d), lambda bi, si, j: (0, 0)),
            pl.BlockSpec((d, tn), lambda bi, si, j: (0, jnp.minimum(j, n_first - 1))),
            pl.BlockSpec((d, tn), lambda bi, si, j: (0, jnp.maximum(j - n_first, 0))),
            pl.BlockSpec(wg_t.shape, lambda bi, si, j: (0, 0)),
        ],
        out_specs=[
            pl.BlockSpec((1, n_sub, tm, 128), lambda bi, si, j: (bi, j, si, 0)),
            pl.BlockSpec((1, wg_t.shape[0], tm), lambda bi, si, j: (bi, 0, si)),
        ],
        out_shape=[
            jax.ShapeDtypeStruct((b, n_blocks, s, 128), BF16),
            jax.ShapeDtypeStruct((b, wg_t.shape[0], s), F32),
        ],
        scratch_shapes=[pltpu.VMEM((tm, d), BF16)],
        compiler_params=pltpu.CompilerParams(
            dimension_semantics=("arbitrary", "arbitrary", "arbitrary"),
            vmem_limit_bytes=VMEM_LIMIT_BYTES),
        name="in_proj",
    )(x, npw, w_first, w_rest, wg_t)


def _gates_kernel(gt_ref, alog_ref, dtb_ref, gb_ref, gout_ref, mout_ref, *, seq):
    nc = seq // SCAN_CHUNK
    ch = SCAN_CHUNK
    nh = N_HEADS
    ii = lax.broadcasted_iota(jnp.int32, (ch, ch), 0)
    jj = lax.broadcasted_iota(jnp.int32, (ch, ch), 1)
    upper = (ii <= jj).astype(F32)
    lower = (ii >= jj).astype(F32)
    lane = lax.broadcasted_iota(jnp.int32, (nh, ch), 1)
    neg_a = -jnp.exp(alog_ref[...])
    dtb = dtb_ref[...]
    gb = gb_ref[...]
    zeros8 = jnp.zeros((nh, ch), F32)

    def bcast(col):
        return jnp.broadcast_to(col, (nh, ch))

    def cummax(a, d):
        y = a
        for sh in (1, 2, 4, 8, 16, 32, 64):
            if d == 0:
                y = jnp.maximum(y, jnp.where(lane >= sh, pltpu.roll(y, sh, 1), NEG_BIG))
            else:
                y = jnp.maximum(y, jnp.where(lane < ch - sh, pltpu.roll(y, ch - sh, 1), NEG_BIG))
        return y

    def local_part(c):
        sl = pl.ds(pl.multiple_of(c * ch, ch), ch)
        g = neg_a * _softplus(gt_ref[0, 0:2 * nh, sl] + dtb)
        beta = _sigmoid(gt_ref[0, 2 * nh:4 * nh, sl])
        ib = gt_ref[0, 4 * nh:6 * nh, sl] + gb[0:2 * nh]
        lf = -_softplus(-(gt_ref[0, 6 * nh:8 * nh, sl] + gb[2 * nh:4 * nh]))
        stacked = jnp.concatenate([g, lf], axis=0)
        prefix = _dot_exact(stacked, upper)
        suffix = _dot_exact(stacked, lower)
        for d in (0, 1):
            rows = slice(nh * d, nh * d + nh)
            cum = prefix if d == 0 else suffix
            gcum = cum[rows]
            gend = bcast(prefix[rows, ch - 1:ch])
            gout_ref[0, 3 * d + 0, :, sl] = gcum
            gout_ref[0, 3 * d + 1, :, sl] = jnp.exp(gend - gcum)
            gout_ref[0, 3 * d + 2, :, sl] = jnp.exp(gend)
            gout_ref[0, 6 + 2 * d, :, sl] = beta[rows]
            gout_ref[0, 7 + 2 * d, :, sl] = jnp.exp(gcum)
            f_local = cum[2 * nh + nh * d:3 * nh + nh * d]
            a_local = ib[rows] - f_local
            mout_ref[0, 3 * d + 0, :, sl] = a_local
            mout_ref[0, 6 + 3 * d, :, sl] = cummax(a_local, d)
            mout_ref[0, 8 + 3 * d, :, sl] = f_local
        for q in range(10, GATE_ROWS):
            gout_ref[0, q, :, sl] = zeros8
        for q in range(12, GATE_ROWS):
            mout_ref[0, q, :, sl] = zeros8

    def carry_part(c, carry, d):
        c_f, c_r = carry
        sl = pl.ds(pl.multiple_of(c * ch, ch), ch)
        a = mout_ref[0, 3 * d + 0, :, sl] - c_f
        r = jnp.maximum(mout_ref[0, 6 + 3 * d, :, sl] - c_f, c_r)
        fcum = mout_ref[0, 8 + 3 * d, :, sl] + c_f
        end = ch - 1 if d == 0 else 0
        r_end = bcast(r[:, end:end + 1])
        mout_ref[0, 3 * d + 0, :, sl] = a
        mout_ref[0, 3 * d + 1, :, sl] = jnp.exp(a - r_end)
        mout_ref[0, 3 * d + 2, :, sl] = jnp.exp(c_r - r_end)
        mout_ref[0, 6 + 3 * d, :, sl] = r
        mout_ref[0, 7 + 3 * d, :, sl] = jnp.exp(c_r - r)
        mout_ref[0, 8 + 3 * d, :, sl] = jnp.exp(-(fcum + r))
        return bcast(fcum[:, end:end + 1]), r_end

    def local_body(i, carry):
        for u in range(GATES_UNROLL):
            local_part(i * GATES_UNROLL + u)
        return carry

    def carry_body(i, carry):
        return carry_part(i, carry[0], 0), carry_part(nc - 1 - i, carry[1], 1)

    lax.fori_loop(0, nc // GATES_UNROLL, local_body, 0)
    lax.fori_loop(0, nc, carry_body, ((zeros8, zeros8), (zeros8, zeros8)))


def _gates(gt, alog, dtb, gb):
    b, n_rows, s = gt.shape
    assert (s // SCAN_CHUNK) % GATES_UNROLL == 0
    out_block = pl.BlockSpec((1, GATE_ROWS, N_HEADS, s), lambda bi: (bi, 0, 0, 0))
    out_shape = jax.ShapeDtypeStruct((b, GATE_ROWS, N_HEADS, s), F32)
    return pl.pallas_call(
        functools.partial(_gates_kernel, seq=s),
        grid=(b,),
        in_specs=[
            pl.BlockSpec((1, n_rows, s), lambda bi: (bi, 0, 0)),
            pl.BlockSpec(alog.shape, lambda bi: (0, 0)),
            pl.BlockSpec(dtb.shape, lambda bi: (0, 0)),
            pl.BlockSpec(gb.shape, lambda bi: (0, 0)),
        ],
        out_specs=[out_block, out_block],
        out_shape=[out_shape, out_shape],
        compiler_params=pltpu.CompilerParams(
            dimension_semantics=("arbitrary",), vmem_limit_bytes=VMEM_LIMIT_BYTES),
        name="gates",
    )(gt, alog, dtb, gb)


def _fill_padded(pad_ref, ref, seq):
    ch = SCAN_CHUNK
    zeros = jnp.zeros((CONV_HALO, pad_ref.shape[-1]), F32)
    pad_ref[0:CONV_HALO, :] = zeros
    pad_ref[seq + CONV_HALO:seq + 2 * CONV_HALO, :] = zeros

    def body(c, carry):
        t0 = pl.multiple_of(c * ch, ch)
        pad_ref[pl.ds(t0 + CONV_HALO, ch), :] = ref[0, 0, pl.ds(t0, ch), :].astype(F32)
        return carry

    lax.fori_loop(0, seq // ch // 4, _unrolled(body, 4), 0)


def _conv_silu(pad_ref, c, w):
    ch = SCAN_CHUNK
    base = c * ch + (CONV_HALO - CONV_K // 2)
    acc = pad_ref[pl.ds(base, ch), :] * w[0:1]
    for j in range(1, CONV_K):
        acc = acc + pad_ref[pl.ds(base + j, ch), :] * w[j:j + 1]
    return _silu(acc)


def _gate_cols(rows):
    ch = SCAN_CHUNK
    return jnp.concatenate([rows, jnp.zeros((ch - GATE_ROWS, ch), F32)], axis=0).T


def _gate_rows_spec(seq):
    return pl.BlockSpec((1, GATE_ROWS, 1, 1, seq), lambda bi, hi: (bi, 0, hi, 0, 0))


def _unrolled(body, factor):
    def wrapped(i, carry):
        for u in range(factor):
            carry = body(i * factor + u, carry)
        return carry
    return wrapped


def _level_masks():
    ch = SCAN_CHUNK
    i = np.arange(ch)[:, None]
    j = np.arange(ch)[None, :]
    n_levels = ch.bit_length() - 1
    out = np.zeros((2, n_levels + 2, ch, ch), np.float32)
    for p in range(n_levels):
        pair = ((i ^ j) >> p) == 1
        out[0, p] = pair & (((i >> p) & 1) == 1)
        out[1, p] = pair & (((j >> p) & 1) == 1)
    out[0, n_levels] = i >= j
    out[1, n_levels] = i <= j
    out[0, n_levels + 1] = out[0, 0] + out[1, 0]
    return jnp.asarray(out, dtype=BF16)


def _gdn_kernel(q_ref, k_ref, v_ref, z_ref, row_ref, cwq_ref, cwk_ref, cwv_ref, nw_ref, lm_ref, out_ref,
                mq_scr, n_scr, o_scr, pad_scr, *, seq):
    ch = SCAN_CHUNK
    nc = seq // ch
    ii = lax.broadcasted_iota(jnp.int32, (ch, ch), 0)
    jj = lax.broadcasted_iota(jnp.int32, (ch, ch), 1)
    eye = (ii == jj).astype(F32)
    incl = (ii >= jj, ii <= jj)
    n_levels = ch.bit_length() - 1

    def prep_group(i, carry):
        chunks = []
        for u in range(GDN_PREP_UNROLL):
            c = i * GDN_PREP_UNROLL + u
            q0 = _conv_silu(pad_scr.at[0], c, cwq_ref[...])
            k0 = _conv_silu(pad_scr.at[1], c, cwk_ref[...])
            v = _conv_silu(pad_scr.at[2], c, cwv_ref[...])
            q = q0 * lax.rsqrt(jnp.sum(q0 * q0, axis=-1, keepdims=True) + L2_EPS) * (HEAD_DIM ** -0.5)
            k = k0 * lax.rsqrt(jnp.sum(k0 * k0, axis=-1, keepdims=True) + L2_EPS)
            kt = k.T
            chunks.append(dict(c=c, sl=pl.ds(pl.multiple_of(c * ch, ch), ch), q=q, k=k, v=v, kt=kt))
        for cd in chunks:
            ktb = cd["kt"].astype(BF16)
            cd["kk"] = _dot(cd["k"].astype(BF16), ktb)
            cd["qk"] = _dot(cd["q"].astype(BF16), ktb)
        chains = []
        for cd in chunks:
            rows = row_ref[0, :, 0, 0, cd["sl"]]
            cols = _gate_cols(rows)
            weight = None
            for d in (0, 1):
                g_col = cols[:, 3 * d:3 * d + 1]
                beta = cols[:, 6 + 2 * d:7 + 2 * d]
                decay = jnp.exp(jnp.where(incl[d], g_col - rows[3 * d:3 * d + 1], NEG_BIG))
                weight = decay * beta if weight is None else weight + decay * beta
                chains.append(dict(cd=cd, d=d, decay=decay, beta=beta,
                                   eg=cols[:, 7 + 2 * d:8 + 2 * d], et_row=rows[3 * d + 1:3 * d + 2]))
            cd["ab"] = (cd["kk"] * weight).astype(BF16)
            cd["t"] = eye - (cd["ab"] * lm_ref[0, n_levels + 1]).astype(F32)
        for p in range(1, n_levels):
            for cd in chunks:
                tb = cd["t"].astype(BF16)
                t_lo, t_up = tb * lm_ref[0, n_levels], tb * lm_ref[1, n_levels]
                l_both = jnp.concatenate([cd["ab"] * lm_ref[0, p], cd["ab"] * lm_ref[1, p]], axis=1)
                x = _dot(l_both, jnp.concatenate([t_lo, t_up], axis=0)).astype(BF16)
                cd["t_both"] = jnp.concatenate([t_lo, t_up], axis=1)
                cd["x_both"] = jnp.concatenate([x * lm_ref[0, p], x * lm_ref[1, p]], axis=0)
            for cd in chunks:
                cd["t"] = cd["t"] - _dot(cd["t_both"], cd["x_both"])
        for cn in chains:
            cd, beta = cn["cd"], cn["beta"]
            rhs = jnp.concatenate([cd["v"] * beta, cd["k"] * (beta * cn["eg"])], axis=1).astype(BF16)
            t_dir = cd["t"].astype(BF16) * lm_ref[cn["d"], n_levels]
            cn["sol"] = _dot(t_dir, rhs).astype(BF16)
        for cn in chains:
            cd = cn["cd"]
            cn["ks"] = _dot((cd["kt"] * cn["et_row"]).astype(BF16), cn["sol"])
            cn["aw"] = _dot((cd["qk"] * cn["decay"]).astype(BF16), cn["sol"])
        for cn in chains:
            cd, d = cn["cd"], cn["d"]
            n_scr[d, cd["sl"], :] = cn["ks"][:, :HEAD_DIM]
            o_scr[d, cd["sl"], :] = cn["aw"][:, :HEAD_DIM]
            mq_scr[d, cd["c"], 0:ch, :] = (-cn["ks"][:, HEAD_DIM:]).astype(BF16)
            mq_scr[d, cd["c"], ch:2 * ch, :] = (cd["q"] * cn["eg"] - cn["aw"][:, HEAD_DIM:]).astype(BF16)
        return carry

    def finish(sl, o):
        y = o * lax.rsqrt(jnp.mean(o * o, axis=-1, keepdims=True) + RMS_EPS) * nw_ref[...]
        out_ref[0, sl, :] = (y * _silu(z_ref[0, 0, sl, :].astype(F32))).astype(out_ref.dtype)

    def scan_body(i, states, *, finalize):
        steps = []
        for d, c in ((0, i), (1, nc - 1 - i)):
            steps.append((d, c, pl.ds(pl.multiple_of(c * ch, ch), ch), states[d], states[d].astype(BF16)))
        new_states = []
        for d, c, sl, s, sb in steps:
            chunk_decay = row_ref[0, 3 * d + 2:3 * d + 3, 0, 0, sl]
            new_states.append(s * chunk_decay + _dot(mq_scr[d, c, 0:ch, :], sb) + n_scr[d, sl, :])
        for d, c, sl, s, sb in steps:
            o = o_scr[d, sl, :] + _dot(mq_scr[d, c, ch:2 * ch, :], sb)
            if finalize:
                finish(sl, o + o_scr[1 - d, sl, :])
            else:
                o_scr[d, sl, :] = o
        return tuple(new_states)

    for t, ref in enumerate((q_ref, k_ref, v_ref)):
        _fill_padded(pad_scr.at[t], ref, seq)
    lax.fori_loop(0, nc // GDN_PREP_UNROLL, prep_group, 0)
    zero_state = jnp.zeros((HEAD_DIM, HEAD_DIM), F32)
    states = lax.fori_loop(0, nc // 4, _unrolled(functools.partial(scan_body, finalize=False), 2),
                           (zero_state, zero_state))
    lax.fori_loop(nc // 4, nc // 2, _unrolled(functools.partial(scan_body, finalize=True), 2), states)


def _gdn(p, rows, conv_w, norm_w, *, col0):
    b, _, s, _ = p.shape
    h = N_HEADS
    nc = s // SCAN_CHUNK
    assert nc % GDN_PREP_UNROLL == 0 and nc % 4 == 0
    level_masks = _level_masks()

    def pblock(off):
        return pl.BlockSpec((1, 1, s, HEAD_DIM), lambda bi, hi: (bi, col0 + off + hi, 0, 0))

    def cblock(off):
        return pl.BlockSpec((CONV_K, HEAD_DIM), lambda bi, hi: (0, off + hi))

    return pl.pallas_call(
        functools.partial(_gdn_kernel, seq=s),
        grid=(b, h),
        in_specs=[
            pblock(0), pblock(h), pblock(2 * h), pblock(3 * h),
            _gate_rows_spec(s),
            cblock(0), cblock(h), cblock(2 * h),
            pl.BlockSpec((1, HEAD_DIM), lambda bi, hi: (0, 0)),
            pl.BlockSpec(level_masks.shape, lambda bi, hi: (0, 0, 0, 0)),
        ],
        out_specs=pl.BlockSpec((1, s, HEAD_DIM), lambda bi, hi: (bi, 0, hi)),
        out_shape=jax.ShapeDtypeStruct((b, s, h * HEAD_DIM), BF16),
        scratch_shapes=[
            pltpu.VMEM((2, nc, 2 * SCAN_CHUNK, HEAD_DIM), BF16),
            pltpu.VMEM((2, s, HEAD_DIM), F32),
            pltpu.VMEM((2, s, HEAD_DIM), F32),
            pltpu.VMEM((3, s + 2 * CONV_HALO, HEAD_DIM), F32),
        ],
        compiler_params=pltpu.CompilerParams(
            dimension_semantics=("arbitrary", "arbitrary"), vmem_limit_bytes=VMEM_LIMIT_BYTES),
        name="gdn",
    )(p, p, p, p, rows, conv_w, conv_w, conv_w, norm_w, level_masks)


def _mlstm_kernel(qk_ref, v_ref, o_ref, z_ref, row_ref, cw_ref, nw_ref, out_ref,
                  qm_scr, kt_scr, qkm_scr, cols_scr, h_scr, pad_scr, *, seq):
    ch = SCAN_CHUNK
    nc = seq // ch
    ii = lax.broadcasted_iota(jnp.int32, (ch, ch), 0)
    jj = lax.broadcasted_iota(jnp.int32, (ch, ch), 1)
    incl = (ii >= jj, ii <= jj)
    ones = jnp.ones((ch, HEAD_DIM), BF16)

    def prep_body(c, carry):
        sl = pl.ds(pl.multiple_of(c * ch, ch), ch)
        t = _conv_silu(pad_scr, c, cw_ref[...])
        qm = jnp.where(jj < MLSTM_DQK, t, 0.0).astype(BF16)
        tt = t.T
        kt = jnp.concatenate([tt[MLSTM_DQK:], jnp.zeros((ch - MLSTM_DQK, ch), F32)], axis=0)
        ktb = (kt * (MLSTM_DQK ** -0.5)).astype(BF16)
        qm_scr[sl, :] = qm
        kt_scr[:, sl] = ktb
        qkm_scr[sl, :] = _dot(qm, ktb)
        cols_scr[sl, :] = _gate_cols(row_ref[0, :, 0, 0, sl])
        return carry

    def scan_group(i, states):
        items = []
        for u in range(MLSTM_UNROLL):
            step = i * MLSTM_UNROLL + u
            for d, c in ((0, step), (1, nc - 1 - step)):
                sl = pl.ds(pl.multiple_of(c * ch, ch), ch)
                items.append(dict(d=d, sl=sl, rows=row_ref[0, :, 0, 0, sl]))
        for it in items:
            d, sl, rows, cols = it["d"], it["sl"], it["rows"], cols_scr[it["sl"], :]
            r_col = cols[:, 6 + 3 * d:7 + 3 * d]
            it["inter"] = cols[:, 7 + 3 * d:8 + 3 * d]
            it["enm"] = cols[:, 8 + 3 * d:9 + 3 * d]
            dec_row = rows[3 * d + 2:3 * d + 3]
            it["dec"] = jnp.concatenate([dec_row, dec_row], axis=1)
            it["vaug"] = jnp.concatenate([v_ref[0, 0, sl, :], ones], axis=1)
            wt = jnp.exp(jnp.where(incl[d], rows[3 * d:3 * d + 1] - r_col, NEG_BIG)) * qkm_scr[sl, :]
            it["wt"] = wt.astype(BF16)
            awk = jnp.broadcast_to(rows[3 * d + 1:3 * d + 2], (ch, ch)).astype(BF16)
            it["wk"] = kt_scr[:, sl] * awk
        for it in items:
            it["intra"] = _dot(it["wt"], it["vaug"])
            it["update"] = _dot(it["wk"], it["vaug"])
        states = list(states)
        for it in items:
            d = it["d"]
            it["qs"] = _dot(qm_scr[it["sl"], :], states[d].astype(BF16))
            states[d] = states[d] * it["dec"] + it["update"]
        for it in items:
            nd = it["inter"] * it["qs"] + it["intra"]
            num = nd[:, :HEAD_DIM]
            den = nd[:, HEAD_DIM:]
            h_scr[it["d"], it["sl"], :] = num / jnp.maximum(jnp.abs(den), it["enm"])
        return tuple(states)

    def final_body(c, carry):
        sl = pl.ds(pl.multiple_of(c * ch, ch), ch)
        hh = h_scr[0, sl, :] + h_scr[1, sl, :]
        y = hh * lax.rsqrt(jnp.mean(hh * hh, axis=-1, keepdims=True) + RMS_EPS) * nw_ref[...]
        y = y * _sigmoid(o_ref[0, 0, sl, :].astype(F32)) * _silu(z_ref[0, 0, sl, :].astype(F32))
        out_ref[0, sl, :] = y.astype(out_ref.dtype)
        return carry

    n_iter = nc // MLSTM_UNROLL
    _fill_padded(pad_scr, qk_ref, seq)
    lax.fori_loop(0, n_iter, _unrolled(prep_body, MLSTM_UNROLL), 0)
    zero_state = jnp.zeros((ch, 2 * HEAD_DIM), F32)
    lax.fori_loop(0, n_iter, scan_group, (zero_state, zero_state))
    lax.fori_loop(0, n_iter, _unrolled(final_body, MLSTM_UNROLL), 0)


def _mlstm(p, rows, conv_w, norm_w, *, col0):
    b, _, s, _ = p.shape
    h = N_HEADS
    assert (s // SCAN_CHUNK) % MLSTM_UNROLL == 0

    def pblock(off):
        return pl.BlockSpec((1, 1, s, HEAD_DIM), lambda bi, hi: (bi, col0 + off + hi, 0, 0))

    return pl.pallas_call(
        functools.partial(_mlstm_kernel, seq=s),
        grid=(b, h),
        in_specs=[
            pblock(0), pblock(h), pblock(2 * h), pblock(3 * h),
            _gate_rows_spec(s),
            pl.BlockSpec((CONV_K, HEAD_DIM), lambda bi, hi: (0, hi)),
            pl.BlockSpec((1, HEAD_DIM), lambda bi, hi: (0, hi)),
        ],
        out_specs=pl.BlockSpec((1, s, HEAD_DIM), lambda bi, hi: (bi, 0, hi)),
        out_shape=jax.ShapeDtypeStruct((b, s, h * HEAD_DIM), BF16),
        scratch_shapes=[
            pltpu.VMEM((s, HEAD_DIM), BF16),
            pltpu.VMEM((HEAD_DIM, s), BF16),
            pltpu.VMEM((s, SCAN_CHUNK), F32),
            pltpu.VMEM((s, SCAN_CHUNK), F32),
            pltpu.VMEM((2, s, HEAD_DIM), F32),
            pltpu.VMEM((s + 2 * CONV_HALO, HEAD_DIM), F32),
        ],
        compiler_params=pltpu.CompilerParams(
            dimension_semantics=("arbitrary", "arbitrary"), vmem_limit_bytes=VMEM_LIMIT_BYTES),
        name="mlstm",
    )(p, p, p, p, rows, conv_w, norm_w)


def _outproj_kernel(g_ref, m_ref, wo_ref, x_ref, npw_ref, y_ref, *, width):
    mixed = _dot(g_ref[0], wo_ref[:width]) + _dot(m_ref[0], wo_ref[width:])
    ms = jnp.mean(mixed * mixed, axis=-1, keepdims=True)
    y_ref[0] = x_ref[0] + mixed * lax.rsqrt(ms + RMS_EPS) * npw_ref[...]


def _out_proj(g, m, w_out, x, npw, *, tm=512):
    b, s, d = x.shape
    width = g.shape[-1]
    return pl.pallas_call(
        functools.partial(_outproj_kernel, width=width),
        grid=(b, s // tm),
        in_specs=[
            pl.BlockSpec((1, tm, width), lambda bi, si: (bi, si, 0)),
            pl.BlockSpec((1, tm, width), lambda bi, si: (bi, si, 0)),
            pl.BlockSpec(w_out.shape, lambda bi, si: (0, 0)),
            pl.BlockSpec((1, tm, d), lambda bi, si: (bi, si, 0)),
            pl.BlockSpec((1, d), lambda bi, si: (0, 0)),
        ],
        out_specs=pl.BlockSpec((1, tm, d), lambda bi, si: (bi, si, 0)),
        out_shape=jax.ShapeDtypeStruct((b, s, d), F32),
        compiler_params=pltpu.CompilerParams(
            dimension_semantics=("arbitrary", "arbitrary"), vmem_limit_bytes=VMEM_LIMIT_BYTES),
        name="out_proj",
    )(g, m, w_out, x, npw)


def _layer(x, norm_pre_w, w_in, gdn_conv_w, gdn_a_log, gdn_dt_bias, gdn_norm_w,
           mlstm_conv_w, mlstm_gate_bias, mlstm_norm_w, w_out, norm_post_w):
    d = x.shape[-1]
    h = N_HEADS
    gw = h * HEAD_DIM
    mqk = h * MLSTM_DQK
    gdn_in = 4 * gw + 4 * h
    m0 = gdn_in
    wb = w_in.astype(BF16)
    mq = wb[:, m0:m0 + mqk].reshape(d, h, MLSTM_DQK)
    mk = wb[:, m0 + mqk:m0 + 2 * mqk].reshape(d, h, MLSTM_DQK)
    m_qk = jnp.concatenate([mq, mk], axis=2).reshape(d, gw)
    m_rest = wb[:, m0 + 2 * mqk:m0 + 2 * mqk + 3 * gw]
    w_mlstm = jnp.concatenate([m_qk, m_rest], axis=1)
    gate_cols = jnp.concatenate([wb[:, 4 * gw:gdn_in], wb[:, m0 + 2 * mqk + 3 * gw:]], axis=1)
    wg_t = gate_cols.T

    p, gt = _in_proj(x, norm_pre_w.reshape(1, d), wb, 4 * gw, w_mlstm, wg_t)
    g_rows, m_rows = _gates(gt, gdn_a_log.reshape(2 * h, 1), gdn_dt_bias.reshape(2 * h, 1),
                            mlstm_gate_bias.reshape(4 * h, 1))
    g_rows = g_rows[:, :, :, None, :]
    m_rows = m_rows[:, :, :, None, :]

    cq = mlstm_conv_w[:, :mqk].reshape(CONV_K, h, MLSTM_DQK)
    ck = mlstm_conv_w[:, mqk:].reshape(CONV_K, h, MLSTM_DQK)
    m_conv = jnp.concatenate([cq, ck], axis=2).reshape(CONV_K, gw)

    g_out = _gdn(p, g_rows, gdn_conv_w, gdn_norm_w.reshape(1, HEAD_DIM), col0=0)
    m_out = _mlstm(p, m_rows, m_conv, mlstm_norm_w.reshape(1, gw), col0=4 * h)
    return _out_proj(g_out, m_out, w_out.astype(BF16), x, norm_post_w.reshape(1, d))


def kernel(x, norm_pre_w, w_in, gdn_conv_w, gdn_a_log, gdn_dt_bias, gdn_norm_w,
           mlstm_conv_w, mlstm_gate_bias, mlstm_norm_w, w_out, norm_post_w):
    for layer in range(norm_pre_w.shape[0]):
        x = _layer(x, norm_pre_w[layer], w_in[layer], gdn_conv_w[layer], gdn_a_log[layer],
                   gdn_dt_bias[layer], gdn_norm_w[layer], mlstm_conv_w[layer],
                   mlstm_gate_bias[layer], mlstm_norm_w[layer], w_out[layer], norm_post_w[layer])
    return x
```

```python
import functools

import jax
import jax.numpy as jnp
import numpy as np
from jax import lax
from jax.experimental import pallas as pl
from jax.experimental.pallas import tpu as pltpu

F32 = jnp.float32
BF16 = jnp.bfloat16

N_HEADS = 8
HEAD_DIM = 128
MLSTM_DQK = 64
CONV_K = 5
CONV_HALO = 8
SCAN_CHUNK = 128
GATE_ROWS = 16
GDN_PREP_UNROLL = 8
MLSTM_UNROLL = 4
GATES_UNROLL = 8
RMS_EPS = 1e-6
L2_EPS = 1e-6
NEG_BIG = -1e30
VMEM_LIMIT_BYTES = 56 * 1024 * 1024


def _dot(a, b):
    return jnp.dot(a, b, preferred_element_type=F32)


def _dot_exact(a, b):
    return jnp.dot(a, b, preferred_element_type=F32, precision=lax.Precision.HIGHEST)


def _softplus(x):
    return jnp.maximum(x, 0.0) + jnp.log1p(jnp.exp(-jnp.abs(x)))


def _sigmoid(x):
    return 0.5 * jnp.tanh(0.5 * x) + 0.5


def _silu(x):
    h = 0.5 * x
    return h * jnp.tanh(h) + h


def _inproj_kernel(x_ref, npw_ref, w_ref, wg_ref, p_ref, gt_ref, h_scr, *, n_sub):
    j = pl.program_id(2)

    @pl.when(j == 0)
    def _():
        x = x_ref[0]
        ms = jnp.mean(x * x, axis=-1, keepdims=True)
        h = (x * lax.rsqrt(ms + RMS_EPS) * npw_ref[...]).astype(BF16)
        h_scr[...] = h
        gt_ref[0] = lax.dot_general(wg_ref[...], h, (((1,), (1,)), ((), ())),
                                    preferred_element_type=F32)

    h = h_scr[...]
    for c in range(n_sub // 2):
        acc = _dot(h, w_ref[:, c * 256:(c + 1) * 256])
        p_ref[0, 2 * c] = acc[:, :128].astype(p_ref.dtype)
        p_ref[0, 2 * c + 1] = acc[:, 128:].astype(p_ref.dtype)


def _in_proj(x, npw, w_main, wg_t, *, tm=1024, tn=2048):
    b, s, d = x.shape
    n_blocks = w_main.shape[1] // 128
    n_sub = tn // 128
    return pl.pallas_call(
        functools.partial(_inproj_kernel, n_sub=n_sub),
        grid=(b, s // tm, w_main.shape[1] // tn),
        in_specs=[
            pl.BlockSpec((1, tm, d), lambda bi, si, j: (bi, si, 0)),
            pl.BlockSpec((1, d), lambda bi, si, j: (0, 0)),
            pl.BlockSpec((d, tn), lambda bi, si, j: (0, j)),
            pl.BlockSpec(wg_t.shape, lambda bi, si, j: (0, 0)),
        ],
        out_specs=[
            pl.BlockSpec((1, n_sub, tm, 128), lambda bi, si, j: (bi, j, si, 0)),
            pl.BlockSpec((1, wg_t.shape[0], tm), lambda bi, si, j: (bi, 0, si)),
        ],
        out_shape=[
            jax.ShapeDtypeStruct((b, n_blocks, s, 128), BF16),
            jax.ShapeDtypeStruct((b, wg_t.shape[0], s), F32),
        ],
        scratch_shapes=[pltpu.VMEM((tm, d), BF16)],
        compiler_params=pltpu.CompilerParams(
            dimension_semantics=("arbitrary", "arbitrary", "arbitrary"),
            vmem_limit_bytes=VMEM_LIMIT_BYTES),
        name="in_proj",
    )(x, npw, w_main, wg_t)


def _gates_kernel(gt_ref, alog_ref, dtb_ref, gb_ref, gout_ref, mout_ref, *, seq):
    nc = seq // SCAN_CHUNK
    ch = SCAN_CHUNK
    nh = N_HEADS
    ii = lax.broadcasted_iota(jnp.int32, (ch, ch), 0)
    jj = lax.broadcasted_iota(jnp.int32, (ch, ch), 1)
    upper = (ii <= jj).astype(F32)
    lower = (ii >= jj).astype(F32)
    lane = lax.broadcasted_iota(jnp.int32, (nh, ch), 1)
    neg_a = -jnp.exp(alog_ref[...])
    dtb = dtb_ref[...]
    gb = gb_ref[...]
    zeros8 = jnp.zeros((nh, ch), F32)

    def bcast(col):
        return jnp.broadcast_to(col, (nh, ch))

    def local_body(i, carry):
        tiles = []
        for u in range(GATES_UNROLL):
            sl = pl.ds(pl.multiple_of((i * GATES_UNROLL + u) * ch, ch), ch)
            g = neg_a * _softplus(gt_ref[0, 0:2 * nh, sl] + dtb)
            lf = -_softplus(-(gt_ref[0, 6 * nh:8 * nh, sl] + gb[2 * nh:4 * nh]))
            tiles.append(dict(sl=sl, stacked=jnp.concatenate([g, lf], axis=0),
                              beta=_sigmoid(gt_ref[0, 2 * nh:4 * nh, sl]),
                              ib=gt_ref[0, 4 * nh:6 * nh, sl] + gb[0:2 * nh]))
        for t in tiles:
            t["cum"] = (_dot_exact(t["stacked"], upper), _dot_exact(t["stacked"], lower))
        for t in tiles:
            t["f_local"] = [t["cum"][d][2 * nh + nh * d:3 * nh + nh * d] for d in (0, 1)]
            t["a_local"] = [t["ib"][nh * d:nh * d + nh] - t["f_local"][d] for d in (0, 1)]
            t["a_max"] = list(t["a_local"])
        for sh in (1, 2, 4, 8, 16, 32, 64):
            for t in tiles:
                fwd, bwd = t["a_max"]
                t["a_max"] = [
                    jnp.maximum(fwd, jnp.where(lane >= sh, pltpu.roll(fwd, sh, 1), NEG_BIG)),
                    jnp.maximum(bwd, jnp.where(lane < ch - sh, pltpu.roll(bwd, ch - sh, 1), NEG_BIG)),
                ]
        for t in tiles:
            sl = t["sl"]
            for d in (0, 1):
                rows = slice(nh * d, nh * d + nh)
                gcum = t["cum"][d][rows]
                gend = bcast(t["cum"][0][rows, ch - 1:ch])
                gout_ref[0, 3 * d + 0, :, sl] = gcum
                gout_ref[0, 3 * d + 1, :, sl] = jnp.exp(gend - gcum)
                gout_ref[0, 3 * d + 2, :, sl] = jnp.exp(gend)
                gout_ref[0, 6 + 2 * d, :, sl] = t["beta"][rows]
                gout_ref[0, 7 + 2 * d, :, sl] = jnp.exp(gcum)
                end = ch - 1 if d == 0 else 0
                mout_ref[0, 3 * d + 0, :, sl] = t["a_local"][d]
                mout_ref[0, 6 + 3 * d, :, sl] = t["a_max"][d]
                mout_ref[0, 8 + 3 * d, :, sl] = t["f_local"][d]
                mout_ref[0, 12 + 2 * d, :, sl] = bcast(t["a_max"][d][:, end:end + 1])
                mout_ref[0, 13 + 2 * d, :, sl] = bcast(t["f_local"][d][:, end:end + 1])
            for q in range(10, GATE_ROWS):
                gout_ref[0, q, :, sl] = zeros8
        return carry

    def carry_part(c, carry, d):
        c_f, c_r = carry
        sl = pl.ds(pl.multiple_of(c * ch, ch), ch)
        a = mout_ref[0, 3 * d + 0, :, sl] - c_f
        r = jnp.maximum(mout_ref[0, 6 + 3 * d, :, sl] - c_f, c_r)
        fcum = mout_ref[0, 8 + 3 * d, :, sl] + c_f
        r_end = jnp.maximum(mout_ref[0, 12 + 2 * d, :, sl] - c_f, c_r)
        f_end = mout_ref[0, 13 + 2 * d, :, sl] + c_f
        mout_ref[0, 3 * d + 0, :, sl] = a
        mout_ref[0, 3 * d + 1, :, sl] = jnp.exp(a - r_end)
        mout_ref[0, 3 * d + 2, :, sl] = jnp.exp(c_r - r_end)
        mout_ref[0, 6 + 3 * d, :, sl] = r
        mout_ref[0, 7 + 3 * d, :, sl] = jnp.exp(c_r - r)
        mout_ref[0, 8 + 3 * d, :, sl] = jnp.exp(-(fcum + r))
        return f_end, r_end

    def carry_body(i, carry):
        return carry_part(i, carry[0], 0), carry_part(nc - 1 - i, carry[1], 1)

    lax.fori_loop(0, nc // GATES_UNROLL, local_body, 0)
    lax.fori_loop(0, nc, carry_body, ((zeros8, zeros8), (zeros8, zeros8)))


def _gates(gt, alog, dtb, gb):
    b, n_rows, s = gt.shape
    assert (s // SCAN_CHUNK) % GATES_UNROLL == 0
    out_block = pl.BlockSpec((1, GATE_ROWS, N_HEADS, s), lambda bi: (bi, 0, 0, 0))
    out_shape = jax.ShapeDtypeStruct((b, GATE_ROWS, N_HEADS, s), F32)
    return pl.pallas_call(
        functools.partial(_gates_kernel, seq=s),
        grid=(b,),
        in_specs=[
            pl.BlockSpec((1, n_rows, s), lambda bi: (bi, 0, 0)),
            pl.BlockSpec(alog.shape, lambda bi: (0, 0)),
            pl.BlockSpec(dtb.shape, lambda bi: (0, 0)),
            pl.BlockSpec(gb.shape, lambda bi: (0, 0)),
        ],
        out_specs=[out_block, out_block],
        out_shape=[out_shape, out_shape],
        compiler_params=pltpu.CompilerParams(
            dimension_semantics=("arbitrary",), vmem_limit_bytes=VMEM_LIMIT_BYTES),
        name="gates",
    )(gt, alog, dtb, gb)


def _fill_padded(pad_ref, ref, seq):
    ch = SCAN_CHUNK
    zeros = jnp.zeros((CONV_HALO, pad_ref.shape[-1]), F32)
    pad_ref[0:CONV_HALO, :] = zeros
    pad_ref[seq + CONV_HALO:seq + 2 * CONV_HALO, :] = zeros

    def body(c, carry):
        t0 = pl.multiple_of(c * ch, ch)
        pad_ref[pl.ds(t0 + CONV_HALO, ch), :] = ref[0, 0, pl.ds(t0, ch), :].astype(F32)
        return carry

    lax.fori_loop(0, seq // ch // 4, _unrolled(body, 4), 0)


def _conv_silu(pad_ref, c, w):
    ch = SCAN_CHUNK
    base = c * ch + (CONV_HALO - CONV_K // 2)
    acc = pad_ref[pl.ds(base, ch), :] * w[0:1]
    for j in range(1, CONV_K):
        acc = acc + pad_ref[pl.ds(base + j, ch), :] * w[j:j + 1]
    return _silu(acc)


def _gate_cols(rows):
    ch = SCAN_CHUNK
    return jnp.concatenate([rows, jnp.zeros((ch - GATE_ROWS, ch), F32)], axis=0).T


def _gate_rows_spec(seq):
    return pl.BlockSpec((1, GATE_ROWS, 1, 1, seq), lambda bi, hi: (bi, 0, hi, 0, 0))


def _unrolled(body, factor):
    def wrapped(i, carry):
        for u in range(factor):
            carry = body(i * factor + u, carry)
        return carry
    return wrapped


def _level_masks():
    ch = SCAN_CHUNK
    i = np.arange(ch)[:, None]
    j = np.arange(ch)[None, :]
    n_levels = ch.bit_length() - 1
    out = np.zeros((2, n_levels + 2, ch, ch), np.float32)
    for p in range(n_levels):
        pair = ((i ^ j) >> p) == 1
        out[0, p] = pair & (((i >> p) & 1) == 1)
        out[1, p] = pair & (((j >> p) & 1) == 1)
    out[0, n_levels] = i >= j
    out[1, n_levels] = i <= j
    out[0, n_levels + 1] = out[0, 0] + out[1, 0]
    return jnp.asarray(out, dtype=BF16)


def _gdn_kernel(q_ref, k_ref, v_ref, z_ref, row_ref, cwq_ref, cwk_ref, cwv_ref, nw_ref, lm_ref, out_ref,
                mq_scr, n_scr, o_scr, pad_scr, *, seq):
    ch = SCAN_CHUNK
    nc = seq // ch
    ii = lax.broadcasted_iota(jnp.int32, (ch, ch), 0)
    jj = lax.broadcasted_iota(jnp.int32, (ch, ch), 1)
    eye = (ii == jj).astype(F32)
    incl = (ii >= jj, ii <= jj)
    n_levels = ch.bit_length() - 1

    def prep_group(i, carry):
        chunks = []
        for u in range(GDN_PREP_UNROLL):
            c = i * GDN_PREP_UNROLL + u
            q0 = _conv_silu(pad_scr.at[0], c, cwq_ref[...])
            k0 = _conv_silu(pad_scr.at[1], c, cwk_ref[...])
            v = _conv_silu(pad_scr.at[2], c, cwv_ref[...])
            q = q0 * lax.rsqrt(jnp.sum(q0 * q0, axis=-1, keepdims=True) + L2_EPS) * (HEAD_DIM ** -0.5)
            k = k0 * lax.rsqrt(jnp.sum(k0 * k0, axis=-1, keepdims=True) + L2_EPS)
            kt = k.T
            chunks.append(dict(c=c, sl=pl.ds(pl.multiple_of(c * ch, ch), ch), q=q, k=k, v=v, kt=kt))
        for cd in chunks:
            ktb = cd["kt"].astype(BF16)
            cd["kk"] = _dot(cd["k"].astype(BF16), ktb)
            cd["qk"] = _dot(cd["q"].astype(BF16), ktb)
        chains = []
        for cd in chunks:
            rows = row_ref[0, :, 0, 0, cd["sl"]]
            cols = _gate_cols(rows)
            weight = None
            for d in (0, 1):
                g_col = cols[:, 3 * d:3 * d + 1]
                beta = cols[:, 6 + 2 * d:7 + 2 * d]
                decay = jnp.exp(jnp.where(incl[d], g_col - rows[3 * d:3 * d + 1], NEG_BIG))
                weight = decay * beta if weight is None else weight + decay * beta
                chains.append(dict(cd=cd, d=d, decay=decay, beta=beta,
                                   eg=cols[:, 7 + 2 * d:8 + 2 * d], et_row=rows[3 * d + 1:3 * d + 2]))
            cd["ab"] = (cd["kk"] * weight).astype(BF16)
            cd["t"] = eye - (cd["ab"] * lm_ref[0, n_levels + 1]).astype(F32)
        for p in range(1, n_levels):
            for cd in chunks:
                tb = cd["t"].astype(BF16)
                t_lo, t_up = tb * lm_ref[0, n_levels], tb * lm_ref[1, n_levels]
                l_both = jnp.concatenate([cd["ab"] * lm_ref[0, p], cd["ab"] * lm_ref[1, p]], axis=1)
                x = _dot(l_both, jnp.concatenate([t_lo, t_up], axis=0)).astype(BF16)
                cd["t_both"] = jnp.concatenate([t_lo, t_up], axis=1)
                cd["x_both"] = jnp.concatenate([x * lm_ref[0, p], x * lm_ref[1, p]], axis=0)
            for cd in chunks:
                cd["t"] = cd["t"] - _dot(cd["t_both"], cd["x_both"])
        for cn in chains:
            cd, beta = cn["cd"], cn["beta"]
            rhs = jnp.concatenate([cd["v"] * beta, cd["k"] * (beta * cn["eg"])], axis=1).astype(BF16)
            t_dir = cd["t"].astype(BF16) * lm_ref[cn["d"], n_levels]
            cn["sol"] = _dot(t_dir, rhs).astype(BF16)
        for cn in chains:
            cd = cn["cd"]
            cn["ks"] = _dot((cd["kt"] * cn["et_row"]).astype(BF16), cn["sol"])
            cn["aw"] = _dot((cd["qk"] * cn["decay"]).astype(BF16), cn["sol"])
        for cn in chains:
            cd, d = cn["cd"], cn["d"]
            n_scr[d, cd["sl"], :] = cn["ks"][:, :HEAD_DIM]
            o_scr[d, cd["sl"], :] = cn["aw"][:, :HEAD_DIM]
            mq_scr[d, cd["c"], 0:ch, :] = (-cn["ks"][:, HEAD_DIM:]).astype(BF16)
            mq_scr[d, cd["c"], ch:2 * ch, :] = (cd["q"] * cn["eg"] - cn["aw"][:, HEAD_DIM:]).astype(BF16)
        return carry

    def finish(sl, o):
        y = o * lax.rsqrt(jnp.mean(o * o, axis=-1, keepdims=True) + RMS_EPS) * nw_ref[...]
        out_ref[0, sl, :] = (y * _silu(z_ref[0, 0, sl, :].astype(F32))).astype(out_ref.dtype)

    def scan_body(i, states, *, finalize):
        steps = []
        for d, c in ((0, i), (1, nc - 1 - i)):
            steps.append((d, c, pl.ds(pl.multiple_of(c * ch, ch), ch), states[d], states[d].astype(BF16)))
        new_states = []
        for d, c, sl, s, sb in steps:
            chunk_decay = row_ref[0, 3 * d + 2:3 * d + 3, 0, 0, sl]
            new_states.append(s * chunk_decay + _dot(mq_scr[d, c, 0:ch, :], sb) + n_scr[d, sl, :])
        for d, c, sl, s, sb in steps:
            o = o_scr[d, sl, :] + _dot(mq_scr[d, c, ch:2 * ch, :], sb)
            if finalize:
                finish(sl, o + o_scr[1 - d, sl, :])
            else:
                o_scr[d, sl, :] = o
        return tuple(new_states)

    for t, ref in enumerate((q_ref, k_ref, v_ref)):
        _fill_padded(pad_scr.at[t], ref, seq)
    lax.fori_loop(0, nc // GDN_PREP_UNROLL, prep_group, 0)
    zero_state = jnp.zeros((HEAD_DIM, HEAD_DIM), F32)
    states = lax.fori_loop(0, nc // 4, _unrolled(functools.partial(scan_body, finalize=False), 2),
                           (zero_state, zero_state))
    lax.fori_loop(nc // 4, nc // 2, _unrolled(functools.partial(scan_body, finalize=True), 2), states)


def _gdn(p, rows, conv_w, norm_w, *, col0):
    b, _, s, _ = p.shape
    h = N_HEADS
    nc = s // SCAN_CHUNK
    assert nc % GDN_PREP_UNROLL == 0 and nc % 4 == 0
    level_masks = _level_masks()

    def pblock(off):
        return pl.BlockSpec((1, 1, s, HEAD_DIM), lambda bi, hi: (bi, col0 + off + hi, 0, 0))

    def cblock(off):
        return pl.BlockSpec((CONV_K, HEAD_DIM), lambda bi, hi: (0, off + hi))

    return pl.pallas_call(
        functools.partial(_gdn_kernel, seq=s),
        grid=(b, h),
        in_specs=[
            pblock(0), pblock(h), pblock(2 * h), pblock(3 * h),
            _gate_rows_spec(s),
            cblock(0), cblock(h), cblock(2 * h),
            pl.BlockSpec((1, HEAD_DIM), lambda bi, hi: (0, 0)),
            pl.BlockSpec(level_masks.shape, lambda bi, hi: (0, 0, 0, 0)),
        ],
        out_specs=pl.BlockSpec((1, s, HEAD_DIM), lambda bi, hi: (bi, 0, hi)),
        out_shape=jax.ShapeDtypeStruct((b, s, h * HEAD_DIM), BF16),
        scratch_shapes=[
            pltpu.VMEM((2, nc, 2 * SCAN_CHUNK, HEAD_DIM), BF16),
            pltpu.VMEM((2, s, HEAD_DIM), F32),
            pltpu.VMEM((2, s, HEAD_DIM), F32),
            pltpu.VMEM((3, s + 2 * CONV_HALO, HEAD_DIM), F32),
        ],
        compiler_params=pltpu.CompilerParams(
            dimension_semantics=("arbitrary", "arbitrary"), vmem_limit_bytes=VMEM_LIMIT_BYTES),
        name="gdn",
    )(p, p, p, p, rows, conv_w, conv_w, conv_w, norm_w, level_masks)


def _mlstm_kernel(qk_ref, v_ref, o_ref, z_ref, row_ref, cw_ref, nw_ref, out_ref,
                  qm_scr, kt_scr, qkm_scr, cols_scr, h_scr, pad_scr, *, seq):
    ch = SCAN_CHUNK
    nc = seq // ch
    ii = lax.broadcasted_iota(jnp.int32, (ch, ch), 0)
    jj = lax.broadcasted_iota(jnp.int32, (ch, ch), 1)
    incl = (ii >= jj, ii <= jj)
    ones = jnp.ones((ch, HEAD_DIM), BF16)

    def prep_body(c, carry):
        sl = pl.ds(pl.multiple_of(c * ch, ch), ch)
        t = _conv_silu(pad_scr, c, cw_ref[...])
        qm = jnp.where(jj < MLSTM_DQK, t, 0.0).astype(BF16)
        tt = t.T
        kt = jnp.concatenate([tt[MLSTM_DQK:], jnp.zeros((ch - MLSTM_DQK, ch), F32)], axis=0)
        ktb = (kt * (MLSTM_DQK ** -0.5)).astype(BF16)
        qm_scr[sl, :] = qm
        kt_scr[:, sl] = ktb
        qkm_scr[sl, :] = _dot(qm, ktb)
        cols_scr[sl, :] = _gate_cols(row_ref[0, :, 0, 0, sl])
        return carry

    def scan_group(i, states):
        items = []
        for u in range(MLSTM_UNROLL):
            step = i * MLSTM_UNROLL + u
            for d, c in ((0, step), (1, nc - 1 - step)):
                sl = pl.ds(pl.multiple_of(c * ch, ch), ch)
                items.append(dict(d=d, sl=sl, rows=row_ref[0, :, 0, 0, sl]))
        for it in items:
            d, sl, rows, cols = it["d"], it["sl"], it["rows"], cols_scr[it["sl"], :]
            r_col = cols[:, 6 + 3 * d:7 + 3 * d]
            it["inter"] = cols[:, 7 + 3 * d:8 + 3 * d]
            it["enm"] = cols[:, 8 + 3 * d:9 + 3 * d]
            dec_row = rows[3 * d + 2:3 * d + 3]
            it["dec"] = jnp.concatenate([dec_row, dec_row], axis=1)
            it["vaug"] = jnp.concatenate([v_ref[0, 0, sl, :], ones], axis=1)
            wt = jnp.exp(jnp.where(incl[d], rows[3 * d:3 * d + 1] - r_col, NEG_BIG)) * qkm_scr[sl, :]
            it["wt"] = wt.astype(BF16)
            awk = jnp.broadcast_to(rows[3 * d + 1:3 * d + 2], (ch, ch)).astype(BF16)
            it["wk"] = kt_scr[:, sl] * awk
        for it in items:
            it["intra"] = _dot(it["wt"], it["vaug"])
            it["update"] = _dot(it["wk"], it["vaug"])
        states = list(states)
        for it in items:
            d = it["d"]
            it["qs"] = _dot(qm_scr[it["sl"], :], states[d].astype(BF16))
            states[d] = states[d] * it["dec"] + it["update"]
        for it in items:
            nd = it["inter"] * it["qs"] + it["intra"]
            num = nd[:, :HEAD_DIM]
            den = nd[:, HEAD_DIM:]
            h_scr[it["d"], it["sl"], :] = num / jnp.maximum(jnp.abs(den), it["enm"])
        return tuple(states)

    def final_body(c, carry):
        sl = pl.ds(pl.multiple_of(c * ch, ch), ch)
        hh = h_scr[0, sl, :] + h_scr[1, sl, :]
        y = hh * lax.rsqrt(jnp.mean(hh * hh, axis=-1, keepdims=True) + RMS_EPS) * nw_ref[...]
        y = y * _sigmoid(o_ref[0, 0, sl, :].astype(F32)) * _silu(z_ref[0, 0, sl, :].astype(F32))
        out_ref[0, sl, :] = y.astype(out_ref.dtype)
        return carry

    n_iter = nc // MLSTM_UNROLL
    _fill_padded(pad_scr, qk_ref, seq)
    lax.fori_loop(0, n_iter, _unrolled(prep_body, MLSTM_UNROLL), 0)
    zero_state = jnp.zeros((ch, 2 * HEAD_DIM), F32)
    lax.fori_loop(0, n_iter, scan_group, (zero_state, zero_state))
    lax.fori_loop(0, n_iter, _unrolled(final_body, MLSTM_UNROLL), 0)


def _mlstm(p, rows, conv_w, norm_w, *, col0):
    b, _, s, _ = p.shape
    h = N_HEADS
    assert (s // SCAN_CHUNK) % MLSTM_UNROLL == 0

    def pblock(off):
        return pl.BlockSpec((1, 1, s, HEAD_DIM), lambda bi, hi: (bi, col0 + off + hi, 0, 0))

    return pl.pallas_call(
        functools.partial(_mlstm_kernel, seq=s),
        grid=(b, h),
        in_specs=[
            pblock(0), pblock(h), pblock(2 * h), pblock(3 * h),
            _gate_rows_spec(s),
            pl.BlockSpec((CONV_K, HEAD_DIM), lambda bi, hi: (0, hi)),
            pl.BlockSpec((1, HEAD_DIM), lambda bi, hi: (0, hi)),
        ],
        out_specs=pl.BlockSpec((1, s, HEAD_DIM), lambda bi, hi: (bi, 0, hi)),
        out_shape=jax.ShapeDtypeStruct((b, s, h * HEAD_DIM), BF16),
        scratch_shapes=[
            pltpu.VMEM((s, HEAD_DIM), BF16),
            pltpu.VMEM((HEAD_DIM, s), BF16),
            pltpu.VMEM((s, SCAN_CHUNK), F32),
            pltpu.VMEM((s, SCAN_CHUNK), F32),
            pltpu.VMEM((2, s, HEAD_DIM), F32),
            pltpu.VMEM((s + 2 * CONV_HALO, HEAD_DIM), F32),
        ],
        compiler_params=pltpu.CompilerParams(
            dimension_semantics=("arbitrary", "arbitrary"), vmem_limit_bytes=VMEM_LIMIT_BYTES),
        name="mlstm",
    )(p, p, p, p, rows, conv_w, norm_w)


def _outproj_kernel(g_ref, m_ref, wo_ref, x_ref, npw_ref, y_ref, *, width):
    mixed = _dot(g_ref[0], wo_ref[:width]) + _dot(m_ref[0], wo_ref[width:])
    ms = jnp.mean(mixed * mixed, axis=-1, keepdims=True)
    y_ref[0] = x_ref[0] + mixed * lax.rsqrt(ms + RMS_EPS) * npw_ref[...]


def _out_proj(g, m, w_out, x, npw, *, tm=512):
    b, s, d = x.shape
    width = g.shape[-1]
    return pl.pallas_call(
        functools.partial(_outproj_kernel, width=width),
        grid=(b, s // tm),
        in_specs=[
            pl.BlockSpec((1, tm, width), lambda bi, si: (bi, si, 0)),
            pl.BlockSpec((1, tm, width), lambda bi, si: (bi, si, 0)),
            pl.BlockSpec(w_out.shape, lambda bi, si: (0, 0)),
            pl.BlockSpec((1, tm, d), lambda bi, si: (bi, si, 0)),
            pl.BlockSpec((1, d), lambda bi, si: (0, 0)),
        ],
        out_specs=pl.BlockSpec((1, tm, d), lambda bi, si: (bi, si, 0)),
        out_shape=jax.ShapeDtypeStruct((b, s, d), F32),
        compiler_params=pltpu.CompilerParams(
            dimension_semantics=("arbitrary", "arbitrary"), vmem_limit_bytes=VMEM_LIMIT_BYTES),
        name="out_proj",
    )(g, m, w_out, x, npw)


def _layer(x, norm_pre_w, w_in, gdn_conv_w, gdn_a_log, gdn_dt_bias, gdn_norm_w,
           mlstm_conv_w, mlstm_gate_bias, mlstm_norm_w, w_out, norm_post_w):
    d = x.shape[-1]
    h = N_HEADS
    gw = h * HEAD_DIM
    mqk = h * MLSTM_DQK
    gdn_in = 4 * gw + 4 * h
    m0 = gdn_in
    wb = w_in.astype(BF16)
    mq = wb[:, m0:m0 + mqk].reshape(d, h, MLSTM_DQK)
    mk = wb[:, m0 + mqk:m0 + 2 * mqk].reshape(d, h, MLSTM_DQK)
    m_qk = jnp.concatenate([mq, mk], axis=2).reshape(d, gw)
    m_rest = wb[:, m0 + 2 * mqk:m0 + 2 * mqk + 3 * gw]
    w_main = jnp.concatenate([wb[:, :4 * gw], m_qk, m_rest], axis=1)
    gate_cols = jnp.concatenate([wb[:, 4 * gw:gdn_in], wb[:, m0 + 2 * mqk + 3 * gw:]], axis=1)
    wg_t = gate_cols.T

    p, gt = _in_proj(x, norm_pre_w.reshape(1, d), w_main, wg_t)
    g_rows, m_rows = _gates(gt, gdn_a_log.reshape(2 * h, 1), gdn_dt_bias.reshape(2 * h, 1),
                            mlstm_gate_bias.reshape(4 * h, 1))
    g_rows = g_rows[:, :, :, None, :]
    m_rows = m_rows[:, :, :, None, :]

    cq = mlstm_conv_w[:, :mqk].reshape(CONV_K, h, MLSTM_DQK)
    ck = mlstm_conv_w[:, mqk:].reshape(CONV_K, h, MLSTM_DQK)
    m_conv = jnp.concatenate([cq, ck], axis=2).reshape(CONV_K, gw)

    g_out = _gdn(p, g_rows, gdn_conv_w, gdn_norm_w.reshape(1, HEAD_DIM), col0=0)
    m_out = _mlstm(p, m_rows, m_conv, mlstm_norm_w.reshape(1, gw), col0=4 * h)
    return _out_proj(g_out, m_out, w_out.astype(BF16), x, norm_post_w.reshape(1, d))


def kernel(x, norm_pre_w, w_in, gdn_conv_w, gdn_a_log, gdn_dt_bias, gdn_norm_w,
           mlstm_conv_w, mlstm_gate_bias, mlstm_norm_w, w_out, norm_post_w):
    for layer in range(norm_pre_w.shape[0]):
        x = _layer(x, norm_pre_w[layer], w_in[layer], gdn_conv_w[layer], gdn_a_log[layer],
                   gdn_dt_bias[layer], gdn_norm_w[layer], mlstm_conv_w[layer],
                   mlstm_gate_bias[layer], mlstm_norm_w[layer], w_out[layer], norm_post_w[layer])
    return x
```

```python
import functools

import jax
import jax.numpy as jnp
import numpy as np
from jax import lax
from jax.experimental import pallas as pl
from jax.experimental.pallas import tpu as pltpu

F32 = jnp.float32
BF16 = jnp.bfloat16

N_HEADS = 8
HEAD_DIM = 128
MLSTM_DQK = 64
CONV_K = 5
CONV_HALO = 8
SCAN_CHUNK = 128
GATE_ROWS = 16
GDN_PREP_UNROLL = 8
MLSTM_UNROLL = 4
GATES_UNROLL = 8
RMS_EPS = 1e-6
L2_EPS = 1e-6
NEG_BIG = -1e30
VMEM_LIMIT_BYTES = 56 * 1024 * 1024


def _dot(a, b):
    return jnp.dot(a, b, preferred_element_type=F32)


def _dot_exact(a, b):
    return jnp.dot(a, b, preferred_element_type=F32, precision=lax.Precision.HIGHEST)


def _softplus(x):
    return jnp.maximum(x, 0.0) + jnp.log1p(jnp.exp(-jnp.abs(x)))


def _sigmoid(x):
    return 0.5 * jnp.tanh(0.5 * x) + 0.5


def _silu(x):
    h = 0.5 * x
    return h * jnp.tanh(h) + h


def _weight_layout_kernel(w_ref, main_ref, gate_ref, *, gw, mqk, n_gate):
    m0 = 4 * gw + n_gate // 2
    tr = w_ref.shape[1]
    main_ref[:, 0:4 * gw] = w_ref[0, :, 0:4 * gw].astype(BF16)
    for hd in range(N_HEADS):
        q = w_ref[0, :, m0 + MLSTM_DQK * hd:m0 + MLSTM_DQK * (hd + 1)]
        k = w_ref[0, :, m0 + mqk + MLSTM_DQK * hd:m0 + mqk + MLSTM_DQK * (hd + 1)]
        main_ref[:, 4 * gw + HEAD_DIM * hd:4 * gw + HEAD_DIM * (hd + 1)] = (
            jnp.concatenate([q, k], axis=1).astype(BF16))
    main_ref[:, 5 * gw:8 * gw] = w_ref[0, :, m0 + 2 * mqk:m0 + 2 * mqk + 3 * gw].astype(BF16)
    gates = jnp.concatenate([w_ref[0, :, 4 * gw:m0], w_ref[0, :, m0 + 2 * mqk + 3 * gw:],
                             jnp.zeros((tr, HEAD_DIM - n_gate), F32)], axis=1)
    gate_ref[...] = gates.astype(BF16)


def _weight_layout(w_in, *, tr=128):
    _, d, width = w_in.shape
    gw = N_HEADS * HEAD_DIM
    mqk = N_HEADS * MLSTM_DQK
    n_gate = width - 8 * gw
    return pl.pallas_call(
        functools.partial(_weight_layout_kernel, gw=gw, mqk=mqk, n_gate=n_gate),
        grid=(d // tr,),
        in_specs=[pl.BlockSpec((1, tr, width), lambda i: (0, i, 0))],
        out_specs=[pl.BlockSpec((tr, 8 * gw), lambda i: (i, 0)),
                   pl.BlockSpec((tr, HEAD_DIM), lambda i: (i, 0))],
        out_shape=[jax.ShapeDtypeStruct((d, 8 * gw), BF16),
                   jax.ShapeDtypeStruct((d, HEAD_DIM), BF16)],
        compiler_params=pltpu.CompilerParams(
            dimension_semantics=("arbitrary",), vmem_limit_bytes=VMEM_LIMIT_BYTES),
        name="weight_layout",
    )(w_in)


def _inproj_kernel(x_ref, npw_ref, w_ref, wg_ref, p_ref, gt_ref, h_scr, *, n_sub):
    j = pl.program_id(2)

    @pl.when(j == 0)
    def _():
        x = x_ref[0]
        ms = jnp.mean(x * x, axis=-1, keepdims=True)
        h = (x * lax.rsqrt(ms + RMS_EPS) * npw_ref[...]).astype(BF16)
        h_scr[...] = h
        gt_ref[0] = _dot(h, wg_ref[...]).T[:gt_ref.shape[1]]

    h = h_scr[...]
    for c in range(n_sub // 2):
        acc = _dot(h, w_ref[:, c * 256:(c + 1) * 256])
        p_ref[0, 2 * c] = acc[:, :128].astype(p_ref.dtype)
        p_ref[0, 2 * c + 1] = acc[:, 128:].astype(p_ref.dtype)


def _in_proj(x, npw, w_main, w_gate, n_gate, *, tm=1024, tn=2048):
    b, s, d = x.shape
    n_blocks = w_main.shape[1] // 128
    n_sub = tn // 128
    return pl.pallas_call(
        functools.partial(_inproj_kernel, n_sub=n_sub),
        grid=(b, s // tm, w_main.shape[1] // tn),
        in_specs=[
            pl.BlockSpec((1, tm, d), lambda bi, si, j: (bi, si, 0)),
            pl.BlockSpec((1, d), lambda bi, si, j: (0, 0)),
            pl.BlockSpec((d, tn), lambda bi, si, j: (0, j)),
            pl.BlockSpec(w_gate.shape, lambda bi, si, j: (0, 0)),
        ],
        out_specs=[
            pl.BlockSpec((1, n_sub, tm, 128), lambda bi, si, j: (bi, j, si, 0)),
            pl.BlockSpec((1, n_gate, tm), lambda bi, si, j: (bi, 0, si)),
        ],
        out_shape=[
            jax.ShapeDtypeStruct((b, n_blocks, s, 128), BF16),
            jax.ShapeDtypeStruct((b, n_gate, s), F32),
        ],
        scratch_shapes=[pltpu.VMEM((tm, d), BF16)],
        compiler_params=pltpu.CompilerParams(
            dimension_semantics=("arbitrary", "arbitrary", "arbitrary"),
            vmem_limit_bytes=VMEM_LIMIT_BYTES),
        name="in_proj",
    )(x, npw, w_main, w_gate)


def _gates_kernel(gt_ref, alog_ref, dtb_ref, gb_ref, gout_ref, mout_ref, *, seq):
    nc = seq // SCAN_CHUNK
    ch = SCAN_CHUNK
    nh = N_HEADS
    ii = lax.broadcasted_iota(jnp.int32, (ch, ch), 0)
    jj = lax.broadcasted_iota(jnp.int32, (ch, ch), 1)
    upper = (ii <= jj).astype(F32)
    lower = (ii >= jj).astype(F32)
    lane = lax.broadcasted_iota(jnp.int32, (nh, ch), 1)
    neg_a = -jnp.exp(alog_ref[...])
    dtb = dtb_ref[...]
    gb = gb_ref[...]
    zeros8 = jnp.zeros((nh, ch), F32)

    def bcast(col):
        return jnp.broadcast_to(col, (nh, ch))

    def local_body(i, carry):
        tiles = []
        for u in range(GATES_UNROLL):
            sl = pl.ds(pl.multiple_of((i * GATES_UNROLL + u) * ch, ch), ch)
            g = neg_a * _softplus(gt_ref[0, 0:2 * nh, sl] + dtb)
            lf = -_softplus(-(gt_ref[0, 6 * nh:8 * nh, sl] + gb[2 * nh:4 * nh]))
            tiles.append(dict(sl=sl, stacked=jnp.concatenate([g, lf], axis=0),
                              beta=_sigmoid(gt_ref[0, 2 * nh:4 * nh, sl]),
                              ib=gt_ref[0, 4 * nh:6 * nh, sl] + gb[0:2 * nh]))
        for t in tiles:
            t["cum"] = (_dot_exact(t["stacked"], upper), _dot_exact(t["stacked"], lower))
        for t in tiles:
            t["f_local"] = [t["cum"][d][2 * nh + nh * d:3 * nh + nh * d] for d in (0, 1)]
            t["a_local"] = [t["ib"][nh * d:nh * d + nh] - t["f_local"][d] for d in (0, 1)]
            t["a_max"] = list(t["a_local"])
        for sh in (1, 2, 4, 8, 16, 32, 64):
            for t in tiles:
                fwd, bwd = t["a_max"]
                t["a_max"] = [
                    jnp.maximum(fwd, jnp.where(lane >= sh, pltpu.roll(fwd, sh, 1), NEG_BIG)),
                    jnp.maximum(bwd, jnp.where(lane < ch - sh, pltpu.roll(bwd, ch - sh, 1), NEG_BIG)),
                ]
        for t in tiles:
            sl = t["sl"]
            for d in (0, 1):
                rows = slice(nh * d, nh * d + nh)
                gcum = t["cum"][d][rows]
                gend = bcast(t["cum"][0][rows, ch - 1:ch])
                gout_ref[0, 3 * d + 0, :, sl] = gcum
                gout_ref[0, 3 * d + 1, :, sl] = jnp.exp(gend - gcum)
                gout_ref[0, 3 * d + 2, :, sl] = jnp.exp(gend)
                gout_ref[0, 6 + 2 * d, :, sl] = t["beta"][rows]
                gout_ref[0, 7 + 2 * d, :, sl] = jnp.exp(gcum)
                end = ch - 1 if d == 0 else 0
                mout_ref[0, 3 * d + 0, :, sl] = t["a_local"][d]
                mout_ref[0, 6 + 3 * d, :, sl] = t["a_max"][d]
                mout_ref[0, 8 + 3 * d, :, sl] = t["f_local"][d]
                mout_ref[0, 12 + 2 * d, :, sl] = bcast(t["a_max"][d][:, end:end + 1])
                mout_ref[0, 13 + 2 * d, :, sl] = bcast(t["f_local"][d][:, end:end + 1])
            for q in range(10, GATE_ROWS):
                gout_ref[0, q, :, sl] = zeros8
        return carry

    def carry_part(c, carry, d):
        c_f, c_r = carry
        sl = pl.ds(pl.multiple_of(c * ch, ch), ch)
        a = mout_ref[0, 3 * d + 0, :, sl] - c_f
        r = jnp.maximum(mout_ref[0, 6 + 3 * d, :, sl] - c_f, c_r)
        fcum = mout_ref[0, 8 + 3 * d, :, sl] + c_f
        r_end = jnp.maximum(mout_ref[0, 12 + 2 * d, :, sl] - c_f, c_r)
        f_end = mout_ref[0, 13 + 2 * d, :, sl] + c_f
        mout_ref[0, 3 * d + 0, :, sl] = a
        mout_ref[0, 3 * d + 1, :, sl] = jnp.exp(a - r_end)
        mout_ref[0, 3 * d + 2, :, sl] = jnp.exp(c_r - r_end)
        mout_ref[0, 6 + 3 * d, :, sl] = r
        mout_ref[0, 7 + 3 * d, :, sl] = jnp.exp(c_r - r)
        mout_ref[0, 8 + 3 * d, :, sl] = jnp.exp(-(fcum + r))
        return f_end, r_end

    def carry_body(i, carry):
        return carry_part(i, carry[0], 0), carry_part(nc - 1 - i, carry[1], 1)

    lax.fori_loop(0, nc // GATES_UNROLL, local_body, 0)
    lax.fori_loop(0, nc, carry_body, ((zeros8, zeros8), (zeros8, zeros8)))


def _gates(gt, alog, dtb, gb):
    b, n_rows, s = gt.shape
    assert (s // SCAN_CHUNK) % GATES_UNROLL == 0
    out_block = pl.BlockSpec((1, GATE_ROWS, N_HEADS, s), lambda bi: (bi, 0, 0, 0))
    out_shape = jax.ShapeDtypeStruct((b, GATE_ROWS, N_HEADS, s), F32)
    return pl.pallas_call(
        functools.partial(_gates_kernel, seq=s),
        grid=(b,),
        in_specs=[
            pl.BlockSpec((1, n_rows, s), lambda bi: (bi, 0, 0)),
            pl.BlockSpec(alog.shape, lambda bi: (0, 0)),
            pl.BlockSpec(dtb.shape, lambda bi: (0, 0)),
            pl.BlockSpec(gb.shape, lambda bi: (0, 0)),
        ],
        out_specs=[out_block, out_block],
        out_shape=[out_shape, out_shape],
        compiler_params=pltpu.CompilerParams(
            dimension_semantics=("arbitrary",), vmem_limit_bytes=VMEM_LIMIT_BYTES),
        name="gates",
    )(gt, alog, dtb, gb)


def _fill_padded(pad_ref, ref, seq):
    ch = SCAN_CHUNK
    zeros = jnp.zeros((CONV_HALO, pad_ref.shape[-1]), F32)
    pad_ref[0:CONV_HALO, :] = zeros
    pad_ref[seq + CONV_HALO:seq + 2 * CONV_HALO, :] = zeros

    def body(c, carry):
        t0 = pl.multiple_of(c * ch, ch)
        pad_ref[pl.ds(t0 + CONV_HALO, ch), :] = ref[0, 0, pl.ds(t0, ch), :].astype(F32)
        return carry

    lax.fori_loop(0, seq // ch // 4, _unrolled(body, 4), 0)


def _conv_silu(pad_ref, c, w):
    ch = SCAN_CHUNK
    base = c * ch + (CONV_HALO - CONV_K // 2)
    acc = pad_ref[pl.ds(base, ch), :] * w[0:1]
    for j in range(1, CONV_K):
        acc = acc + pad_ref[pl.ds(base + j, ch), :] * w[j:j + 1]
    return _silu(acc)


def _gate_cols(rows):
    ch = SCAN_CHUNK
    return jnp.concatenate([rows, jnp.zeros((ch - GATE_ROWS, ch), F32)], axis=0).T


def _gate_rows_spec(seq):
    return pl.BlockSpec((1, GATE_ROWS, 1, 1, seq), lambda bi, hi: (bi, 0, hi, 0, 0))


def _unrolled(body, factor):
    def wrapped(i, carry):
        for u in range(factor):
            carry = body(i * factor + u, carry)
        return carry
    return wrapped


def _level_masks():
    ch = SCAN_CHUNK
    i = np.arange(ch)[:, None]
    j = np.arange(ch)[None, :]
    n_levels = ch.bit_length() - 1
    out = np.zeros((2, n_levels + 2, ch, ch), np.float32)
    for p in range(n_levels):
        pair = ((i ^ j) >> p) == 1
        out[0, p] = pair & (((i >> p) & 1) == 1)
        out[1, p] = pair & (((j >> p) & 1) == 1)
    out[0, n_levels] = i >= j
    out[1, n_levels] = i <= j
    out[0, n_levels + 1] = out[0, 0] + out[1, 0]
    return jnp.asarray(out, dtype=BF16)


def _gdn_kernel(q_ref, k_ref, v_ref, z_ref, row_ref, cwq_ref, cwk_ref, cwv_ref, nw_ref, lm_ref, out_ref,
                mq_scr, n_scr, o_scr, pad_scr, *, seq):
    ch = SCAN_CHUNK
    nc = seq // ch
    ii = lax.broadcasted_iota(jnp.int32, (ch, ch), 0)
    jj = lax.broadcasted_iota(jnp.int32, (ch, ch), 1)
    eye = (ii == jj).astype(F32)
    incl = (ii >= jj, ii <= jj)
    n_levels = ch.bit_length() - 1

    def prep_group(i, carry):
        chunks = []
        for u in range(GDN_PREP_UNROLL):
            c = i * GDN_PREP_UNROLL + u
            q0 = _conv_silu(pad_scr.at[0], c, cwq_ref[...])
            k0 = _conv_silu(pad_scr.at[1], c, cwk_ref[...])
            v = _conv_silu(pad_scr.at[2], c, cwv_ref[...])
            q = q0 * lax.rsqrt(jnp.sum(q0 * q0, axis=-1, keepdims=True) + L2_EPS) * (HEAD_DIM ** -0.5)
            k = k0 * lax.rsqrt(jnp.sum(k0 * k0, axis=-1, keepdims=True) + L2_EPS)
            kt = k.T
            chunks.append(dict(c=c, sl=pl.ds(pl.multiple_of(c * ch, ch), ch), q=q, k=k, v=v, kt=kt))
        for cd in chunks:
            ktb = cd["kt"].astype(BF16)
            cd["kk"] = _dot(cd["k"].astype(BF16), ktb)
            cd["qk"] = _dot(cd["q"].astype(BF16), ktb)
        chains = []
        for cd in chunks:
            rows = row_ref[0, :, 0, 0, cd["sl"]]
            cols = _gate_cols(rows)
            weight = None
            for d in (0, 1):
                g_col = cols[:, 3 * d:3 * d + 1]
                beta = cols[:, 6 + 2 * d:7 + 2 * d]
                decay = jnp.exp(jnp.where(incl[d], g_col - rows[3 * d:3 * d + 1], NEG_BIG))
                weight = decay * beta if weight is None else weight + decay * beta
                chains.append(dict(cd=cd, d=d, decay=decay, beta=beta,
                                   eg=cols[:, 7 + 2 * d:8 + 2 * d], et_row=rows[3 * d + 1:3 * d + 2]))
            cd["ab"] = (cd["kk"] * weight).astype(BF16)
            cd["t"] = eye - (cd["ab"] * lm_ref[0, n_levels + 1]).astype(F32)
        for p in range(1, n_levels):
            for cd in chunks:
                tb = cd["t"].astype(BF16)
                t_lo, t_up = tb * lm_ref[0, n_levels], tb * lm_ref[1, n_levels]
                l_both = jnp.concatenate([cd["ab"] * lm_ref[0, p], cd["ab"] * lm_ref[1, p]], axis=1)
                x = _dot(l_both, jnp.concatenate([t_lo, t_up], axis=0)).astype(BF16)
                cd["t_both"] = jnp.concatenate([t_lo, t_up], axis=1)
                cd["x_both"] = jnp.concatenate([x * lm_ref[0, p], x * lm_ref[1, p]], axis=0)
            for cd in chunks:
                cd["t"] = cd["t"] - _dot(cd["t_both"], cd["x_both"])
        for cn in chains:
            cd, beta = cn["cd"], cn["beta"]
            rhs = jnp.concatenate([cd["v"] * beta, cd["k"] * (beta * cn["eg"])], axis=1).astype(BF16)
            t_dir = cd["t"].astype(BF16) * lm_ref[cn["d"], n_levels]
            cn["sol"] = _dot(t_dir, rhs).astype(BF16)
        for cn in chains:
            cd = cn["cd"]
            cn["ks"] = _dot((cd["kt"] * cn["et_row"]).astype(BF16), cn["sol"])
            cn["aw"] = _dot((cd["qk"] * cn["decay"]).astype(BF16), cn["sol"])
        for cn in chains:
            cd, d = cn["cd"], cn["d"]
            n_scr[d, cd["sl"], :] = cn["ks"][:, :HEAD_DIM]
            o_scr[d, cd["sl"], :] = cn["aw"][:, :HEAD_DIM]
            mq_scr[d, cd["c"], 0:ch, :] = (-cn["ks"][:, HEAD_DIM:]).astype(BF16)
            mq_scr[d, cd["c"], ch:2 * ch, :] = (cd["q"] * cn["eg"] - cn["aw"][:, HEAD_DIM:]).astype(BF16)
        return carry

    def finish(sl, o):
        y = o * lax.rsqrt(jnp.mean(o * o, axis=-1, keepdims=True) + RMS_EPS) * nw_ref[...]
        out_ref[0, sl, :] = (y * _silu(z_ref[0, 0, sl, :].astype(F32))).astype(out_ref.dtype)

    def scan_body(i, states, *, finalize):
        steps = []
        for d, c in ((0, i), (1, nc - 1 - i)):
            steps.append((d, c, pl.ds(pl.multiple_of(c * ch, ch), ch), states[d], states[d].astype(BF16)))
        new_states = []
        for d, c, sl, s, sb in steps:
            chunk_decay = row_ref[0, 3 * d + 2:3 * d + 3, 0, 0, sl]
            new_states.append(s * chunk_decay + _dot(mq_scr[d, c, 0:ch, :], sb) + n_scr[d, sl, :])
        for d, c, sl, s, sb in steps:
            o = o_scr[d, sl, :] + _dot(mq_scr[d, c, ch:2 * ch, :], sb)
            if finalize:
                finish(sl, o + o_scr[1 - d, sl, :])
            else:
                o_scr[d, sl, :] = o
        return tuple(new_states)

    for t, ref in enumerate((q_ref, k_ref, v_ref)):
        _fill_padded(pad_scr.at[t], ref, seq)
    lax.fori_loop(0, nc // GDN_PREP_UNROLL, prep_group, 0)
    zero_state = jnp.zeros((HEAD_DIM, HEAD_DIM), F32)
    states = lax.fori_loop(0, nc // 4, _unrolled(functools.partial(scan_body, finalize=False), 2),
                           (zero_state, zero_state))
    lax.fori_loop(nc // 4, nc // 2, _unrolled(functools.partial(scan_body, finalize=True), 2), states)


def _gdn(p, rows, conv_w, norm_w, *, col0):
    b, _, s, _ = p.shape
    h = N_HEADS
    nc = s // SCAN_CHUNK
    assert nc % GDN_PREP_UNROLL == 0 and nc % 4 == 0
    level_masks = _level_masks()

    def pblock(off):
        return pl.BlockSpec((1, 1, s, HEAD_DIM), lambda bi, hi: (bi, col0 + off + hi, 0, 0))

    def cblock(off):
        return pl.BlockSpec((CONV_K, HEAD_DIM), lambda bi, hi: (0, off + hi))

    return pl.pallas_call(
        functools.partial(_gdn_kernel, seq=s),
        grid=(b, h),
        in_specs=[
            pblock(0), pblock(h), pblock(2 * h), pblock(3 * h),
            _gate_rows_spec(s),
            cblock(0), cblock(h), cblock(2 * h),
            pl.BlockSpec((1, HEAD_DIM), lambda bi, hi: (0, 0)),
            pl.BlockSpec(level_masks.shape, lambda bi, hi: (0, 0, 0, 0)),
        ],
        out_specs=pl.BlockSpec((1, s, HEAD_DIM), lambda bi, hi: (bi, 0, hi)),
        out_shape=jax.ShapeDtypeStruct((b, s, h * HEAD_DIM), BF16),
        scratch_shapes=[
            pltpu.VMEM((2, nc, 2 * SCAN_CHUNK, HEAD_DIM), BF16),
            pltpu.VMEM((2, s, HEAD_DIM), F32),
            pltpu.VMEM((2, s, HEAD_DIM), F32),
            pltpu.VMEM((3, s + 2 * CONV_HALO, HEAD_DIM), F32),
        ],
        compiler_params=pltpu.CompilerParams(
            dimension_semantics=("arbitrary", "arbitrary"), vmem_limit_bytes=VMEM_LIMIT_BYTES),
        name="gdn",
    )(p, p, p, p, rows, conv_w, conv_w, conv_w, norm_w, level_masks)


def _mlstm_kernel(qk_ref, v_ref, o_ref, z_ref, row_ref, cw_ref, nw_ref, out_ref,
                  qm_scr, kt_scr, qkm_scr, cols_scr, h_scr, pad_scr, *, seq):
    ch = SCAN_CHUNK
    nc = seq // ch
    ii = lax.broadcasted_iota(jnp.int32, (ch, ch), 0)
    jj = lax.broadcasted_iota(jnp.int32, (ch, ch), 1)
    incl = (ii >= jj, ii <= jj)
    ones = jnp.ones((ch, HEAD_DIM), BF16)

    def prep_body(c, carry):
        sl = pl.ds(pl.multiple_of(c * ch, ch), ch)
        t = _conv_silu(pad_scr, c, cw_ref[...])
        qm = jnp.where(jj < MLSTM_DQK, t, 0.0).astype(BF16)
        tt = t.T
        kt = jnp.concatenate([tt[MLSTM_DQK:], jnp.zeros((ch - MLSTM_DQK, ch), F32)], axis=0)
        ktb = (kt * (MLSTM_DQK ** -0.5)).astype(BF16)
        qm_scr[sl, :] = qm
        kt_scr[:, sl] = ktb
        qkm_scr[sl, :] = _dot(qm, ktb)
        cols_scr[sl, :] = _gate_cols(row_ref[0, :, 0, 0, sl])
        return carry

    def scan_group(i, states):
        items = []
        for u in range(MLSTM_UNROLL):
            step = i * MLSTM_UNROLL + u
            for d, c in ((0, step), (1, nc - 1 - step)):
                sl = pl.ds(pl.multiple_of(c * ch, ch), ch)
                items.append(dict(d=d, sl=sl, rows=row_ref[0, :, 0, 0, sl]))
        for it in items:
            d, sl, rows, cols = it["d"], it["sl"], it["rows"], cols_scr[it["sl"], :]
            r_col = cols[:, 6 + 3 * d:7 + 3 * d]
            it["inter"] = cols[:, 7 + 3 * d:8 + 3 * d]
            it["enm"] = cols[:, 8 + 3 * d:9 + 3 * d]
            dec_row = rows[3 * d + 2:3 * d + 3]
            it["dec"] = jnp.concatenate([dec_row, dec_row], axis=1)
            it["vaug"] = jnp.concatenate([v_ref[0, 0, sl, :], ones], axis=1)
            wt = jnp.exp(jnp.where(incl[d], rows[3 * d:3 * d + 1] - r_col, NEG_BIG)) * qkm_scr[sl, :]
            it["wt"] = wt.astype(BF16)
            awk = jnp.broadcast_to(rows[3 * d + 1:3 * d + 2], (ch, ch)).astype(BF16)
            it["wk"] = kt_scr[:, sl] * awk
        for it in items:
            it["intra"] = _dot(it["wt"], it["vaug"])
            it["update"] = _dot(it["wk"], it["vaug"])
        states = list(states)
        for it in items:
            d = it["d"]
            it["qs"] = _dot(qm_scr[it["sl"], :], states[d].astype(BF16))
            states[d] = states[d] * it["dec"] + it["update"]
        for it in items:
            nd = it["inter"] * it["qs"] + it["intra"]
            num = nd[:, :HEAD_DIM]
            den = nd[:, HEAD_DIM:]
            h_scr[it["d"], it["sl"], :] = num / jnp.maximum(jnp.abs(den), it["enm"])
        return tuple(states)

    def final_body(c, carry):
        sl = pl.ds(pl.multiple_of(c * ch, ch), ch)
        hh = h_scr[0, sl, :] + h_scr[1, sl, :]
        y = hh * lax.rsqrt(jnp.mean(hh * hh, axis=-1, keepdims=True) + RMS_EPS) * nw_ref[...]
        y = y * _sigmoid(o_ref[0, 0, sl, :].astype(F32)) * _silu(z_ref[0, 0, sl, :].astype(F32))
        out_ref[0, sl, :] = y.astype(out_ref.dtype)
        return carry

    n_iter = nc // MLSTM_UNROLL
    _fill_padded(pad_scr, qk_ref, seq)
    lax.fori_loop(0, n_iter, _unrolled(prep_body, MLSTM_UNROLL), 0)
    zero_state = jnp.zeros((ch, 2 * HEAD_DIM), F32)
    lax.fori_loop(0, n_iter, scan_group, (zero_state, zero_state))
    lax.fori_loop(0, n_iter, _unrolled(final_body, MLSTM_UNROLL), 0)


def _mlstm(p, rows, conv_w, norm_w, *, col0):
    b, _, s, _ = p.shape
    h = N_HEADS
    assert (s // SCAN_CHUNK) % MLSTM_UNROLL == 0

    def pblock(off):
        return pl.BlockSpec((1, 1, s, HEAD_DIM), lambda bi, hi: (bi, col0 + off + hi, 0, 0))

    return pl.pallas_call(
        functools.partial(_mlstm_kernel, seq=s),
        grid=(b, h),
        in_specs=[
            pblock(0), pblock(h), pblock(2 * h), pblock(3 * h),
            _gate_rows_spec(s),
            pl.BlockSpec((CONV_K, HEAD_DIM), lambda bi, hi: (0, hi)),
            pl.BlockSpec((1, HEAD_DIM), lambda bi, hi: (0, hi)),
        ],
        out_specs=pl.BlockSpec((1, s, HEAD_DIM), lambda bi, hi: (bi, 0, hi)),
        out_shape=jax.ShapeDtypeStruct((b, s, h * HEAD_DIM), BF16),
        scratch_shapes=[
            pltpu.VMEM((s, HEAD_DIM), BF16),
            pltpu.VMEM((HEAD_DIM, s), BF16),
            pltpu.VMEM((s, SCAN_CHUNK), F32),
            pltpu.VMEM((s, SCAN_CHUNK), F32),
            pltpu.VMEM((2, s, HEAD_DIM), F32),
            pltpu.VMEM((s + 2 * CONV_HALO, HEAD_DIM), F32),
        ],
        compiler_params=pltpu.CompilerParams(
            dimension_semantics=("arbitrary", "arbitrary"), vmem_limit_bytes=VMEM_LIMIT_BYTES),
        name="mlstm",
    )(p, p, p, p, rows, conv_w, norm_w)


def _outproj_kernel(g_ref, m_ref, wo_ref, x_ref, npw_ref, y_ref, *, width):
    mixed = _dot(g_ref[0], wo_ref[:width]) + _dot(m_ref[0], wo_ref[width:])
    ms = jnp.mean(mixed * mixed, axis=-1, keepdims=True)
    y_ref[0] = x_ref[0] + mixed * lax.rsqrt(ms + RMS_EPS) * npw_ref[...]


def _out_proj(g, m, w_out, x, npw, *, tm=512):
    b, s, d = x.shape
    width = g.shape[-1]
    return pl.pallas_call(
        functools.partial(_outproj_kernel, width=width),
        grid=(b, s // tm),
        in_specs=[
            pl.BlockSpec((1, tm, width), lambda bi, si: (bi, si, 0)),
            pl.BlockSpec((1, tm, width), lambda bi, si: (bi, si, 0)),
            pl.BlockSpec(w_out.shape, lambda bi, si: (0, 0)),
            pl.BlockSpec((1, tm, d), lambda bi, si: (bi, si, 0)),
            pl.BlockSpec((1, d), lambda bi, si: (0, 0)),
        ],
        out_specs=pl.BlockSpec((1, tm, d), lambda bi, si: (bi, si, 0)),
        out_shape=jax.ShapeDtypeStruct((b, s, d), F32),
        compiler_params=pltpu.CompilerParams(
            dimension_semantics=("arbitrary", "arbitrary"), vmem_limit_bytes=VMEM_LIMIT_BYTES),
        name="out_proj",
    )(g, m, w_out, x, npw)


def _layer(x, norm_pre_w, w_in, gdn_conv_w, gdn_a_log, gdn_dt_bias, gdn_norm_w,
           mlstm_conv_w, mlstm_gate_bias, mlstm_norm_w, w_out, norm_post_w):
    d = x.shape[-1]
    h = N_HEADS
    gw = h * HEAD_DIM
    mqk = h * MLSTM_DQK
    w_main, w_gate = _weight_layout(w_in)

    p, gt = _in_proj(x, norm_pre_w.reshape(1, d), w_main, w_gate, 8 * h)
    g_rows, m_rows = _gates(gt, gdn_a_log.reshape(2 * h, 1), gdn_dt_bias.reshape(2 * h, 1),
                            mlstm_gate_bias.reshape(4 * h, 1))
    g_rows = g_rows[:, :, :, None, :]
    m_rows = m_rows[:, :, :, None, :]

    cq = mlstm_conv_w[:, :mqk].reshape(CONV_K, h, MLSTM_DQK)
    ck = mlstm_conv_w[:, mqk:].reshape(CONV_K, h, MLSTM_DQK)
    m_conv = jnp.concatenate([cq, ck], axis=2).reshape(CONV_K, gw)

    g_out = _gdn(p, g_rows, gdn_conv_w, gdn_norm_w.reshape(1, HEAD_DIM), col0=0)
    m_out = _mlstm(p, m_rows, m_conv, mlstm_norm_w.reshape(1, gw), col0=4 * h)
    return _out_proj(g_out, m_out, w_out.astype(BF16), x, norm_post_w.reshape(1, d))


def kernel(x, norm_pre_w, w_in, gdn_conv_w, gdn_a_log, gdn_dt_bias, gdn_norm_w,
           mlstm_conv_w, mlstm_gate_bias, mlstm_norm_w, w_out, norm_post_w):
    for layer in range(norm_pre_w.shape[0]):
        x = _layer(x, norm_pre_w[layer], w_in[layer:layer + 1], gdn_conv_w[layer], gdn_a_log[layer],
                   gdn_dt_bias[layer], gdn_norm_w[layer], mlstm_conv_w[layer],
                   mlstm_gate_bias[layer], mlstm_norm_w[layer], w_out[layer], norm_post_w[layer])
    return x
```

```python
import functools

import jax
import jax.numpy as jnp
import numpy as np
from jax import lax
from jax.experimental import pallas as pl
from jax.experimental.pallas import tpu as pltpu

F32 = jnp.float32
BF16 = jnp.bfloat16

N_HEADS = 8
HEAD_DIM = 128
MLSTM_DQK = 64
CONV_K = 5
CONV_HALO = 8
SCAN_CHUNK = 128
GATE_ROWS = 16
WT_ROWS = 32
WT_PIECES = 16
GDN_PREP_UNROLL = 8
MLSTM_UNROLL = 4
GATES_UNROLL = 8
RMS_EPS = 1e-6
L2_EPS = 1e-6
NEG_BIG = -1e30
VMEM_LIMIT_BYTES = 56 * 1024 * 1024


def _dot(a, b):
    return jnp.dot(a, b, preferred_element_type=F32)


def _dot_exact(a, b):
    return jnp.dot(a, b, preferred_element_type=F32, precision=lax.Precision.HIGHEST)


def _softplus(x):
    return jnp.maximum(x, 0.0) + jnp.log1p(jnp.exp(-jnp.abs(x)))


def _sigmoid(x):
    return 0.5 * jnp.tanh(0.5 * x) + 0.5


def _silu(x):
    h = 0.5 * x
    return h * jnp.tanh(h) + h


def _weight_layout_kernel(*refs):
    parts, gate_parts, (main_ref, gate_ref) = refs[:WT_PIECES], refs[WT_PIECES:-2], refs[-2:]
    for i, part in enumerate(parts):
        main_ref[WT_ROWS * i:WT_ROWS * (i + 1), :] = part[0].astype(BF16)
    for i, part in enumerate(gate_parts):
        gate_ref[WT_ROWS * i:WT_ROWS * (i + 1), :] = part[0].astype(BF16)


def _weight_layout(w_t):
    _, width, d = w_t.shape
    gw = N_HEADS * HEAD_DIM
    mqk = N_HEADS * MLSTM_DQK
    n_gate = width - 8 * gw
    m0 = 4 * gw + n_gate // 2
    assert n_gate == 2 * WT_ROWS and m0 % WT_ROWS == 0 and MLSTM_DQK == 2 * WT_ROWS
    per_head = HEAD_DIM // WT_ROWS
    assert WT_PIECES % per_head == 0 and (8 * gw) % (WT_PIECES * WT_ROWS) == 0
    q0, k0, rest0 = m0 // WT_ROWS, (m0 + mqk) // WT_ROWS, (m0 + 2 * mqk) // WT_ROWS
    qk_begin, qk_end = 4 * gw // WT_ROWS, 5 * gw // WT_ROWS

    def piece(part):
        sub = part % per_head
        def index(r):
            g = r * WT_PIECES + part
            hd = (g - qk_begin) // per_head
            qk = (q0 if sub < per_head // 2 else k0 - per_head // 2) + (per_head // 2) * hd + sub
            return 0, jnp.where(g < qk_begin, g, jnp.where(g < qk_end, qk, rest0 + g - qk_end)), 0
        return pl.BlockSpec((1, WT_ROWS, d), index)

    def gate_piece(row0):
        return pl.BlockSpec((1, WT_ROWS, d), lambda r: (0, row0 // WT_ROWS, 0))

    return pl.pallas_call(
        _weight_layout_kernel,
        grid=(8 * gw // (WT_PIECES * WT_ROWS),),
        in_specs=[piece(p) for p in range(WT_PIECES)] + [gate_piece(4 * gw), gate_piece(width - WT_ROWS)],
        out_specs=[pl.BlockSpec((WT_PIECES * WT_ROWS, d), lambda r: (r, 0)),
                   pl.BlockSpec((n_gate, d), lambda r: (0, 0))],
        out_shape=[jax.ShapeDtypeStruct((8 * gw, d), BF16),
                   jax.ShapeDtypeStruct((n_gate, d), BF16)],
        compiler_params=pltpu.CompilerParams(
            dimension_semantics=("arbitrary",), vmem_limit_bytes=VMEM_LIMIT_BYTES),
        name="weight_layout",
    )(*([w_t] * (WT_PIECES + 2)))


_CONTRACT_MINOR = (((1,), (1,)), ((), ()))


def _inproj_kernel(x_ref, npw_ref, w_ref, wg_ref, p_ref, gt_ref, h_scr, *, n_sub):
    j = pl.program_id(2)

    @pl.when(j == 0)
    def _():
        x = x_ref[0]
        ms = jnp.mean(x * x, axis=-1, keepdims=True)
        h = (x * lax.rsqrt(ms + RMS_EPS) * npw_ref[...]).astype(BF16)
        h_scr[...] = h
        gt_ref[0] = lax.dot_general(wg_ref[...], h, _CONTRACT_MINOR, preferred_element_type=F32)

    h = h_scr[...]
    for c in range(n_sub // 2):
        acc = lax.dot_general(h, w_ref[c * 256:(c + 1) * 256, :], _CONTRACT_MINOR,
                              preferred_element_type=F32)
        p_ref[0, 2 * c] = acc[:, :128].astype(p_ref.dtype)
        p_ref[0, 2 * c + 1] = acc[:, 128:].astype(p_ref.dtype)


def _in_proj(x, npw, w_main_t, w_gate_t, *, tm=1024, tn=2048):
    b, s, d = x.shape
    n_blocks = w_main_t.shape[0] // 128
    n_gate = w_gate_t.shape[0]
    n_sub = tn // 128
    return pl.pallas_call(
        functools.partial(_inproj_kernel, n_sub=n_sub),
        grid=(b, s // tm, w_main_t.shape[0] // tn),
        in_specs=[
            pl.BlockSpec((1, tm, d), lambda bi, si, j: (bi, si, 0)),
            pl.BlockSpec((1, d), lambda bi, si, j: (0, 0)),
            pl.BlockSpec((tn, d), lambda bi, si, j: (j, 0)),
            pl.BlockSpec(w_gate_t.shape, lambda bi, si, j: (0, 0)),
        ],
        out_specs=[
            pl.BlockSpec((1, n_sub, tm, 128), lambda bi, si, j: (bi, j, si, 0)),
            pl.BlockSpec((1, n_gate, tm), lambda bi, si, j: (bi, 0, si)),
        ],
        out_shape=[
            jax.ShapeDtypeStruct((b, n_blocks, s, 128), BF16),
            jax.ShapeDtypeStruct((b, n_gate, s), F32),
        ],
        scratch_shapes=[pltpu.VMEM((tm, d), BF16)],
        compiler_params=pltpu.CompilerParams(
            dimension_semantics=("arbitrary", "arbitrary", "arbitrary"),
            vmem_limit_bytes=VMEM_LIMIT_BYTES),
        name="in_proj",
    )(x, npw, w_main_t, w_gate_t)


def _gates_kernel(gt_ref, alog_ref, dtb_ref, gb_ref, gout_ref, mout_ref, *, seq):
    nc = seq // SCAN_CHUNK
    ch = SCAN_CHUNK
    nh = N_HEADS
    ii = lax.broadcasted_iota(jnp.int32, (ch, ch), 0)
    jj = lax.broadcasted_iota(jnp.int32, (ch, ch), 1)
    upper = (ii <= jj).astype(F32)
    lower = (ii >= jj).astype(F32)
    lane = lax.broadcasted_iota(jnp.int32, (nh, ch), 1)
    neg_a = -jnp.exp(alog_ref[...])
    dtb = dtb_ref[...]
    gb = gb_ref[...]
    zeros8 = jnp.zeros((nh, ch), F32)

    def bcast(col):
        return jnp.broadcast_to(col, (nh, ch))

    def local_body(i, carry):
        tiles = []
        for u in range(GATES_UNROLL):
            sl = pl.ds(pl.multiple_of((i * GATES_UNROLL + u) * ch, ch), ch)
            g = neg_a * _softplus(gt_ref[0, 0:2 * nh, sl] + dtb)
            lf = -_softplus(-(gt_ref[0, 6 * nh:8 * nh, sl] + gb[2 * nh:4 * nh]))
            tiles.append(dict(sl=sl, stacked=jnp.concatenate([g, lf], axis=0),
                              beta=_sigmoid(gt_ref[0, 2 * nh:4 * nh, sl]),
                              ib=gt_ref[0, 4 * nh:6 * nh, sl] + gb[0:2 * nh]))
        for t in tiles:
            t["cum"] = (_dot_exact(t["stacked"], upper), _dot_exact(t["stacked"], lower))
        for t in tiles:
            t["f_local"] = [t["cum"][d][2 * nh + nh * d:3 * nh + nh * d] for d in (0, 1)]
            t["a_local"] = [t["ib"][nh * d:nh * d + nh] - t["f_local"][d] for d in (0, 1)]
            t["a_max"] = list(t["a_local"])
        for sh in (1, 2, 4, 8, 16, 32, 64):
            for t in tiles:
                fwd, bwd = t["a_max"]
                t["a_max"] = [
                    jnp.maximum(fwd, jnp.where(lane >= sh, pltpu.roll(fwd, sh, 1), NEG_BIG)),
                    jnp.maximum(bwd, jnp.where(lane < ch - sh, pltpu.roll(bwd, ch - sh, 1), NEG_BIG)),
                ]
        for t in tiles:
            sl = t["sl"]
            for d in (0, 1):
                rows = slice(nh * d, nh * d + nh)
                gcum = t["cum"][d][rows]
                gend = bcast(t["cum"][0][rows, ch - 1:ch])
                gout_ref[0, 3 * d + 0, :, sl] = gcum
                gout_ref[0, 3 * d + 1, :, sl] = jnp.exp(gend - gcum)
                gout_ref[0, 3 * d + 2, :, sl] = jnp.exp(gend)
                gout_ref[0, 6 + 2 * d, :, sl] = t["beta"][rows]
                gout_ref[0, 7 + 2 * d, :, sl] = jnp.exp(gcum)
                end = ch - 1 if d == 0 else 0
                mout_ref[0, 3 * d + 0, :, sl] = t["a_local"][d]
                mout_ref[0, 6 + 3 * d, :, sl] = t["a_max"][d]
                mout_ref[0, 8 + 3 * d, :, sl] = t["f_local"][d]
                mout_ref[0, 12 + 2 * d, :, sl] = bcast(t["a_max"][d][:, end:end + 1])
                mout_ref[0, 13 + 2 * d, :, sl] = bcast(t["f_local"][d][:, end:end + 1])
            for q in range(10, GATE_ROWS):
                gout_ref[0, q, :, sl] = zeros8
        return carry

    def carry_part(c, carry, d):
        c_f, c_r = carry
        sl = pl.ds(pl.multiple_of(c * ch, ch), ch)
        a = mout_ref[0, 3 * d + 0, :, sl] - c_f
        r = jnp.maximum(mout_ref[0, 6 + 3 * d, :, sl] - c_f, c_r)
        fcum = mout_ref[0, 8 + 3 * d, :, sl] + c_f
        r_end = jnp.maximum(mout_ref[0, 12 + 2 * d, :, sl] - c_f, c_r)
        f_end = mout_ref[0, 13 + 2 * d, :, sl] + c_f
        mout_ref[0, 3 * d + 0, :, sl] = a
        mout_ref[0, 3 * d + 1, :, sl] = jnp.exp(a - r_end)
        mout_ref[0, 3 * d + 2, :, sl] = jnp.exp(c_r - r_end)
        mout_ref[0, 6 + 3 * d, :, sl] = r
        mout_ref[0, 7 + 3 * d, :, sl] = jnp.exp(c_r - r)
        mout_ref[0, 8 + 3 * d, :, sl] = jnp.exp(-(fcum + r))
        return f_end, r_end

    def carry_body(i, carry):
        return carry_part(i, carry[0], 0), carry_part(nc - 1 - i, carry[1], 1)

    lax.fori_loop(0, nc // GATES_UNROLL, local_body, 0)
    lax.fori_loop(0, nc, carry_body, ((zeros8, zeros8), (zeros8, zeros8)))


def _gates(gt, alog, dtb, gb):
    b, n_rows, s = gt.shape
    assert (s // SCAN_CHUNK) % GATES_UNROLL == 0
    out_block = pl.BlockSpec((1, GATE_ROWS, N_HEADS, s), lambda bi: (bi, 0, 0, 0))
    out_shape = jax.ShapeDtypeStruct((b, GATE_ROWS, N_HEADS, s), F32)
    return pl.pallas_call(
        functools.partial(_gates_kernel, seq=s),
        grid=(b,),
        in_specs=[
            pl.BlockSpec((1, n_rows, s), lambda bi: (bi, 0, 0)),
            pl.BlockSpec(alog.shape, lambda bi: (0, 0)),
            pl.BlockSpec(dtb.shape, lambda bi: (0, 0)),
            pl.BlockSpec(gb.shape, lambda bi: (0, 0)),
        ],
        out_specs=[out_block, out_block],
        out_shape=[out_shape, out_shape],
        compiler_params=pltpu.CompilerParams(
            dimension_semantics=("arbitrary",), vmem_limit_bytes=VMEM_LIMIT_BYTES),
        name="gates",
    )(gt, alog, dtb, gb)


def _fill_padded(pad_ref, ref, seq):
    ch = SCAN_CHUNK
    zeros = jnp.zeros((CONV_HALO, pad_ref.shape[-1]), F32)
    pad_ref[0:CONV_HALO, :] = zeros
    pad_ref[seq + CONV_HALO:seq + 2 * CONV_HALO, :] = zeros

    def body(c, carry):
        t0 = pl.multiple_of(c * ch, ch)
        pad_ref[pl.ds(t0 + CONV_HALO, ch), :] = ref[0, 0, pl.ds(t0, ch), :].astype(F32)
        return carry

    lax.fori_loop(0, seq // ch // 4, _unrolled(body, 4), 0)


def _conv_silu(pad_ref, c, w):
    ch = SCAN_CHUNK
    base = c * ch + (CONV_HALO - CONV_K // 2)
    acc = pad_ref[pl.ds(base, ch), :] * w[0:1]
    for j in range(1, CONV_K):
        acc = acc + pad_ref[pl.ds(base + j, ch), :] * w[j:j + 1]
    return _silu(acc)


def _gate_cols(rows):
    ch = SCAN_CHUNK
    return jnp.concatenate([rows, jnp.zeros((ch - GATE_ROWS, ch), F32)], axis=0).T


def _gate_rows_spec(seq):
    return pl.BlockSpec((1, GATE_ROWS, 1, 1, seq), lambda bi, hi: (bi, 0, hi, 0, 0))


def _unrolled(body, factor):
    def wrapped(i, carry):
        for u in range(factor):
            carry = body(i * factor + u, carry)
        return carry
    return wrapped


def _level_masks():
    ch = SCAN_CHUNK
    i = np.arange(ch)[:, None]
    j = np.arange(ch)[None, :]
    n_levels = ch.bit_length() - 1
    out = np.zeros((2, n_levels + 2, ch, ch), np.float32)
    for p in range(n_levels):
        pair = ((i ^ j) >> p) == 1
        out[0, p] = pair & (((i >> p) & 1) == 1)
        out[1, p] = pair & (((j >> p) & 1) == 1)
    out[0, n_levels] = i >= j
    out[1, n_levels] = i <= j
    out[0, n_levels + 1] = out[0, 0] + out[1, 0]
    return jnp.asarray(out, dtype=BF16)


def _gdn_kernel(q_ref, k_ref, v_ref, z_ref, row_ref, cwq_ref, cwk_ref, cwv_ref, nw_ref, lm_ref, out_ref,
                mq_scr, n_scr, o_scr, pad_scr, *, seq):
    ch = SCAN_CHUNK
    nc = seq // ch
    ii = lax.broadcasted_iota(jnp.int32, (ch, ch), 0)
    jj = lax.broadcasted_iota(jnp.int32, (ch, ch), 1)
    eye = (ii == jj).astype(F32)
    incl = (ii >= jj, ii <= jj)
    n_levels = ch.bit_length() - 1

    def prep_group(i, carry):
        chunks = []
        for u in range(GDN_PREP_UNROLL):
            c = i * GDN_PREP_UNROLL + u
            q0 = _conv_silu(pad_scr.at[0], c, cwq_ref[...])
            k0 = _conv_silu(pad_scr.at[1], c, cwk_ref[...])
            v = _conv_silu(pad_scr.at[2], c, cwv_ref[...])
            q = q0 * lax.rsqrt(jnp.sum(q0 * q0, axis=-1, keepdims=True) + L2_EPS) * (HEAD_DIM ** -0.5)
            k = k0 * lax.rsqrt(jnp.sum(k0 * k0, axis=-1, keepdims=True) + L2_EPS)
            kt = k.T
            chunks.append(dict(c=c, sl=pl.ds(pl.multiple_of(c * ch, ch), ch), q=q, k=k, v=v, kt=kt))
        for cd in chunks:
            ktb = cd["kt"].astype(BF16)
            cd["kk"] = _dot(cd["k"].astype(BF16), ktb)
            cd["qk"] = _dot(cd["q"].astype(BF16), ktb)
        chains = []
        for cd in chunks:
            rows = row_ref[0, :, 0, 0, cd["sl"]]
            cols = _gate_cols(rows)
            weight = None
            for d in (0, 1):
                g_col = cols[:, 3 * d:3 * d + 1]
                beta = cols[:, 6 + 2 * d:7 + 2 * d]
                decay = jnp.exp(jnp.where(incl[d], g_col - rows[3 * d:3 * d + 1], NEG_BIG))
                weight = decay * beta if weight is None else weight + decay * beta
                chains.append(dict(cd=cd, d=d, decay=decay, beta=beta,
                                   eg=cols[:, 7 + 2 * d:8 + 2 * d], et_row=rows[3 * d + 1:3 * d + 2]))
            cd["ab"] = (cd["kk"] * weight).astype(BF16)
            cd["t"] = eye - (cd["ab"] * lm_ref[0, n_levels + 1]).astype(F32)
        for p in range(1, n_levels):
            for cd in chunks:
                tb = cd["t"].astype(BF16)
                t_lo, t_up = tb * lm_ref[0, n_levels], tb * lm_ref[1, n_levels]
                l_both = jnp.concatenate([cd["ab"] * lm_ref[0, p], cd["ab"] * lm_ref[1, p]], axis=1)
                x = _dot(l_both, jnp.concatenate([t_lo, t_up], axis=0)).astype(BF16)
                cd["t_both"] = jnp.concatenate([t_lo, t_up], axis=1)
                cd["x_both"] = jnp.concatenate([x * lm_ref[0, p], x * lm_ref[1, p]], axis=0)
            for cd in chunks:
                cd["t"] = cd["t"] - _dot(cd["t_both"], cd["x_both"])
        for cn in chains:
            cd, beta = cn["cd"], cn["beta"]
            rhs = jnp.concatenate([cd["v"] * beta, cd["k"] * (beta * cn["eg"])], axis=1).astype(BF16)
            t_dir = cd["t"].astype(BF16) * lm_ref[cn["d"], n_levels]
            cn["sol"] = _dot(t_dir, rhs).astype(BF16)
        for cn in chains:
            cd = cn["cd"]
            cn["ks"] = _dot((cd["kt"] * cn["et_row"]).astype(BF16), cn["sol"])
            cn["aw"] = _dot((cd["qk"] * cn["decay"]).astype(BF16), cn["sol"])
        for cn in chains:
            cd, d = cn["cd"], cn["d"]
            n_scr[d, cd["sl"], :] = cn["ks"][:, :HEAD_DIM]
            o_scr[d, cd["sl"], :] = cn["aw"][:, :HEAD_DIM]
            mq_scr[d, cd["c"], 0:ch, :] = (-cn["ks"][:, HEAD_DIM:]).astype(BF16)
            mq_scr[d, cd["c"], ch:2 * ch, :] = (cd["q"] * cn["eg"] - cn["aw"][:, HEAD_DIM:]).astype(BF16)
        return carry

    def finish(sl, o):
        y = o * lax.rsqrt(jnp.mean(o * o, axis=-1, keepdims=True) + RMS_EPS) * nw_ref[...]
        out_ref[0, sl, :] = (y * _silu(z_ref[0, 0, sl, :].astype(F32))).astype(out_ref.dtype)

    def scan_body(i, states, *, finalize):
        steps = []
        for d, c in ((0, i), (1, nc - 1 - i)):
            steps.append((d, c, pl.ds(pl.multiple_of(c * ch, ch), ch), states[d], states[d].astype(BF16)))
        new_states = []
        for d, c, sl, s, sb in steps:
            chunk_decay = row_ref[0, 3 * d + 2:3 * d + 3, 0, 0, sl]
            new_states.append(s * chunk_decay + _dot(mq_scr[d, c, 0:ch, :], sb) + n_scr[d, sl, :])
        for d, c, sl, s, sb in steps:
            o = o_scr[d, sl, :] + _dot(mq_scr[d, c, ch:2 * ch, :], sb)
            if finalize:
                finish(sl, o + o_scr[1 - d, sl, :])
            else:
                o_scr[d, sl, :] = o
        return tuple(new_states)

    for t, ref in enumerate((q_ref, k_ref, v_ref)):
        _fill_padded(pad_scr.at[t], ref, seq)
    lax.fori_loop(0, nc // GDN_PREP_UNROLL, prep_group, 0)
    zero_state = jnp.zeros((HEAD_DIM, HEAD_DIM), F32)
    states = lax.fori_loop(0, nc // 4, _unrolled(functools.partial(scan_body, finalize=False), 2),
                           (zero_state, zero_state))
    lax.fori_loop(nc // 4, nc // 2, _unrolled(functools.partial(scan_body, finalize=True), 2), states)


def _gdn(p, rows, conv_w, norm_w, *, col0):
    b, _, s, _ = p.shape
    h = N_HEADS
    nc = s // SCAN_CHUNK
    assert nc % GDN_PREP_UNROLL == 0 and nc % 4 == 0
    level_masks = _level_masks()

    def pblock(off):
        return pl.BlockSpec((1, 1, s, HEAD_DIM), lambda bi, hi: (bi, col0 + off + hi, 0, 0))

    def cblock(off):
        return pl.BlockSpec((CONV_K, HEAD_DIM), lambda bi, hi: (0, off + hi))

    return pl.pallas_call(
        functools.partial(_gdn_kernel, seq=s),
        grid=(b, h),
        in_specs=[
            pblock(0), pblock(h), pblock(2 * h), pblock(3 * h),
            _gate_rows_spec(s),
            cblock(0), cblock(h), cblock(2 * h),
            pl.BlockSpec((1, HEAD_DIM), lambda bi, hi: (0, 0)),
            pl.BlockSpec(level_masks.shape, lambda bi, hi: (0, 0, 0, 0)),
        ],
        out_specs=pl.BlockSpec((1, s, HEAD_DIM), lambda bi, hi: (bi, 0, hi)),
        out_shape=jax.ShapeDtypeStruct((b, s, h * HEAD_DIM), BF16),
        scratch_shapes=[
            pltpu.VMEM((2, nc, 2 * SCAN_CHUNK, HEAD_DIM), BF16),
            pltpu.VMEM((2, s, HEAD_DIM), F32),
            pltpu.VMEM((2, s, HEAD_DIM), F32),
            pltpu.VMEM((3, s + 2 * CONV_HALO, HEAD_DIM), F32),
        ],
        compiler_params=pltpu.CompilerParams(
            dimension_semantics=("arbitrary", "arbitrary"), vmem_limit_bytes=VMEM_LIMIT_BYTES),
        name="gdn",
    )(p, p, p, p, rows, conv_w, conv_w, conv_w, norm_w, level_masks)


def _mlstm_kernel(qk_ref, v_ref, o_ref, z_ref, row_ref, cw_ref, nw_ref, out_ref,
                  qm_scr, kt_scr, qkm_scr, cols_scr, h_scr, pad_scr, *, seq):
    ch = SCAN_CHUNK
    nc = seq // ch
    ii = lax.broadcasted_iota(jnp.int32, (ch, ch), 0)
    jj = lax.broadcasted_iota(jnp.int32, (ch, ch), 1)
    incl = (ii >= jj, ii <= jj)
    ones = jnp.ones((ch, HEAD_DIM), BF16)

    def prep_body(c, carry):
        sl = pl.ds(pl.multiple_of(c * ch, ch), ch)
        t = _conv_silu(pad_scr, c, cw_ref[...])
        qm = jnp.where(jj < MLSTM_DQK, t, 0.0).astype(BF16)
        tt = t.T
        kt = jnp.concatenate([tt[MLSTM_DQK:], jnp.zeros((ch - MLSTM_DQK, ch), F32)], axis=0)
        ktb = (kt * (MLSTM_DQK ** -0.5)).astype(BF16)
        qm_scr[sl, :] = qm
        kt_scr[:, sl] = ktb
        qkm_scr[sl, :] = _dot(qm, ktb)
        cols_scr[sl, :] = _gate_cols(row_ref[0, :, 0, 0, sl])
        return carry

    def scan_group(i, states):
        items = []
        for u in range(MLSTM_UNROLL):
            step = i * MLSTM_UNROLL + u
            for d, c in ((0, step), (1, nc - 1 - step)):
                sl = pl.ds(pl.multiple_of(c * ch, ch), ch)
                items.append(dict(d=d, sl=sl, rows=row_ref[0, :, 0, 0, sl]))
        for it in items:
            d, sl, rows, cols = it["d"], it["sl"], it["rows"], cols_scr[it["sl"], :]
            r_col = cols[:, 6 + 3 * d:7 + 3 * d]
            it["inter"] = cols[:, 7 + 3 * d:8 + 3 * d]
            it["enm"] = cols[:, 8 + 3 * d:9 + 3 * d]
            dec_row = rows[3 * d + 2:3 * d + 3]
            it["dec"] = jnp.concatenate([dec_row, dec_row], axis=1)
            it["vaug"] = jnp.concatenate([v_ref[0, 0, sl, :], ones], axis=1)
            wt = jnp.exp(jnp.where(incl[d], rows[3 * d:3 * d + 1] - r_col, NEG_BIG)) * qkm_scr[sl, :]
            it["wt"] = wt.astype(BF16)
            awk = jnp.broadcast_to(rows[3 * d + 1:3 * d + 2], (ch, ch)).astype(BF16)
            it["wk"] = kt_scr[:, sl] * awk
        for it in items:
            it["intra"] = _dot(it["wt"], it["vaug"])
            it["update"] = _dot(it["wk"], it["vaug"])
        states = list(states)
        for it in items:
            d = it["d"]
            it["qs"] = _dot(qm_scr[it["sl"], :], states[d].astype(BF16))
            states[d] = states[d] * it["dec"] + it["update"]
        for it in items:
            nd = it["inter"] * it["qs"] + it["intra"]
            num = nd[:, :HEAD_DIM]
            den = nd[:, HEAD_DIM:]
            h_scr[it["d"], it["sl"], :] = num / jnp.maximum(jnp.abs(den), it["enm"])
        return tuple(states)

    def final_body(c, carry):
        sl = pl.ds(pl.multiple_of(c * ch, ch), ch)
        hh = h_scr[0, sl, :] + h_scr[1, sl, :]
        y = hh * lax.rsqrt(jnp.mean(hh * hh, axis=-1, keepdims=True) + RMS_EPS) * nw_ref[...]
        y = y * _sigmoid(o_ref[0, 0, sl, :].astype(F32)) * _silu(z_ref[0, 0, sl, :].astype(F32))
        out_ref[0, sl, :] = y.astype(out_ref.dtype)
        return carry

    n_iter = nc // MLSTM_UNROLL
    _fill_padded(pad_scr, qk_ref, seq)
    lax.fori_loop(0, n_iter, _unrolled(prep_body, MLSTM_UNROLL), 0)
    zero_state = jnp.zeros((ch, 2 * HEAD_DIM), F32)
    lax.fori_loop(0, n_iter, scan_group, (zero_state, zero_state))
    lax.fori_loop(0, n_iter, _unrolled(final_body, MLSTM_UNROLL), 0)


def _mlstm(p, rows, conv_w, norm_w, *, col0):
    b, _, s, _ = p.shape
    h = N_HEADS
    assert (s // SCAN_CHUNK) % MLSTM_UNROLL == 0

    def pblock(off):
        return pl.BlockSpec((1, 1, s, HEAD_DIM), lambda bi, hi: (bi, col0 + off + hi, 0, 0))

    return pl.pallas_call(
        functools.partial(_mlstm_kernel, seq=s),
        grid=(b, h),
        in_specs=[
            pblock(0), pblock(h), pblock(2 * h), pblock(3 * h),
            _gate_rows_spec(s),
            pl.BlockSpec((CONV_K, HEAD_DIM), lambda bi, hi: (0, hi)),
            pl.BlockSpec((1, HEAD_DIM), lambda bi, hi: (0, hi)),
        ],
        out_specs=pl.BlockSpec((1, s, HEAD_DIM), lambda bi, hi: (bi, 0, hi)),
        out_shape=jax.ShapeDtypeStruct((b, s, h * HEAD_DIM), BF16),
        scratch_shapes=[
            pltpu.VMEM((s, HEAD_DIM), BF16),
            pltpu.VMEM((HEAD_DIM, s), BF16),
            pltpu.VMEM((s, SCAN_CHUNK), F32),
            pltpu.VMEM((s, SCAN_CHUNK), F32),
            pltpu.VMEM((2, s, HEAD_DIM), F32),
            pltpu.VMEM((s + 2 * CONV_HALO, HEAD_DIM), F32),
        ],
        compiler_params=pltpu.CompilerParams(
            dimension_semantics=("arbitrary", "arbitrary"), vmem_limit_bytes=VMEM_LIMIT_BYTES),
        name="mlstm",
    )(p, p, p, p, rows, conv_w, norm_w)


def _outproj_kernel(g_ref, m_ref, wo_ref, x_ref, npw_ref, y_ref, *, width):
    mixed = _dot(g_ref[0], wo_ref[:width]) + _dot(m_ref[0], wo_ref[width:])
    ms = jnp.mean(mixed * mixed, axis=-1, keepdims=True)
    y_ref[0] = x_ref[0] + mixed * lax.rsqrt(ms + RMS_EPS) * npw_ref[...]


def _out_proj(g, m, w_out, x, npw, *, tm=512):
    b, s, d = x.shape
    width = g.shape[-1]
    return pl.pallas_call(
        functools.partial(_outproj_kernel, width=width),
        grid=(b, s // tm),
        in_specs=[
            pl.BlockSpec((1, tm, width), lambda bi, si: (bi, si, 0)),
            pl.BlockSpec((1, tm, width), lambda bi, si: (bi, si, 0)),
            pl.BlockSpec(w_out.shape, lambda bi, si: (0, 0)),
            pl.BlockSpec((1, tm, d), lambda bi, si: (bi, si, 0)),
            pl.BlockSpec((1, d), lambda bi, si: (0, 0)),
        ],
        out_specs=pl.BlockSpec((1, tm, d), lambda bi, si: (bi, si, 0)),
        out_shape=jax.ShapeDtypeStruct((b, s, d), F32),
        compiler_params=pltpu.CompilerParams(
            dimension_semantics=("arbitrary", "arbitrary"), vmem_limit_bytes=VMEM_LIMIT_BYTES),
        name="out_proj",
    )(g, m, w_out, x, npw)


def _layer(x, norm_pre_w, w_in, gdn_conv_w, gdn_a_log, gdn_dt_bias, gdn_norm_w,
           mlstm_conv_w, mlstm_gate_bias, mlstm_norm_w, w_out, norm_post_w):
    d = x.shape[-1]
    h = N_HEADS
    gw = h * HEAD_DIM
    mqk = h * MLSTM_DQK
    w_main_t, w_gate_t = _weight_layout(jnp.swapaxes(w_in, 1, 2))

    p, gt = _in_proj(x, norm_pre_w.reshape(1, d), w_main_t, w_gate_t)
    g_rows, m_rows = _gates(gt, gdn_a_log.reshape(2 * h, 1), gdn_dt_bias.reshape(2 * h, 1),
                            mlstm_gate_bias.reshape(4 * h, 1))
    g_rows = g_rows[:, :, :, None, :]
    m_rows = m_rows[:, :, :, None, :]

    cq = mlstm_conv_w[:, :mqk].reshape(CONV_K, h, MLSTM_DQK)
    ck = mlstm_conv_w[:, mqk:].reshape(CONV_K, h, MLSTM_DQK)
    m_conv = jnp.concatenate([cq, ck], axis=2).reshape(CONV_K, gw)

    g_out = _gdn(p, g_rows, gdn_conv_w, gdn_norm_w.reshape(1, HEAD_DIM), col0=0)
    m_out = _mlstm(p, m_rows, m_conv, mlstm_norm_w.reshape(1, gw), col0=4 * h)
    return _out_proj(g_out, m_out, w_out.astype(BF16), x, norm_post_w.reshape(1, d))


def kernel(x, norm_pre_w, w_in, gdn_conv_w, gdn_a_log, gdn_dt_bias, gdn_norm_w,
           mlstm_conv_w, mlstm_gate_bias, mlstm_norm_w, w_out, norm_post_w):
    for layer in range(norm_pre_w.shape[0]):
        x = _layer(x, norm_pre_w[layer], w_in[layer:layer + 1], gdn_conv_w[layer], gdn_a_log[layer],
                   gdn_dt_bias[layer], gdn_norm_w[layer], mlstm_conv_w[layer],
                   mlstm_gate_bias[layer], mlstm_norm_w[layer], w_out[layer], norm_post_w[layer])
    return x
```

```python
import functools

import jax
import jax.numpy as jnp
import numpy as np
from jax import lax
from jax.experimental import pallas as pl
from jax.experimental.pallas import tpu as pltpu

F32 = jnp.float32
BF16 = jnp.bfloat16

N_HEADS = 8
HEAD_DIM = 128
MLSTM_DQK = 64
CONV_K = 5
CONV_HALO = 8
SCAN_CHUNK = 128
GATE_ROWS = 16
WT_ROWS = 32
WT_PIECES = 16
GDN_PREP_UNROLL = 8
MLSTM_UNROLL = 8
GATES_UNROLL = 8
RMS_EPS = 1e-6
L2_EPS = 1e-6
NEG_BIG = -1e30
VMEM_LIMIT_BYTES = 56 * 1024 * 1024


def _dot(a, b):
    return jnp.dot(a, b, preferred_element_type=F32)


def _dot_exact(a, b):
    return jnp.dot(a, b, preferred_element_type=F32, precision=lax.Precision.HIGHEST)


def _softplus(x):
    return jnp.maximum(x, 0.0) + jnp.log1p(jnp.exp(-jnp.abs(x)))


def _sigmoid(x):
    return 0.5 * jnp.tanh(0.5 * x) + 0.5


def _silu(x):
    h = 0.5 * x
    return h * jnp.tanh(h) + h


def _weight_layout_kernel(*refs):
    parts, gate_parts, (main_ref, gate_ref) = refs[:WT_PIECES], refs[WT_PIECES:-2], refs[-2:]
    for i, part in enumerate(parts):
        main_ref[WT_ROWS * i:WT_ROWS * (i + 1), :] = part[0].astype(BF16)
    for i, part in enumerate(gate_parts):
        gate_ref[WT_ROWS * i:WT_ROWS * (i + 1), :] = part[0].astype(BF16)


def _weight_layout(w_t):
    _, width, d = w_t.shape
    gw = N_HEADS * HEAD_DIM
    mqk = N_HEADS * MLSTM_DQK
    n_gate = width - 8 * gw
    m0 = 4 * gw + n_gate // 2
    assert n_gate == 2 * WT_ROWS and m0 % WT_ROWS == 0 and MLSTM_DQK == 2 * WT_ROWS
    per_head = HEAD_DIM // WT_ROWS
    assert WT_PIECES % per_head == 0 and (8 * gw) % (WT_PIECES * WT_ROWS) == 0
    q0, k0, rest0 = m0 // WT_ROWS, (m0 + mqk) // WT_ROWS, (m0 + 2 * mqk) // WT_ROWS
    qk_begin, qk_end = 4 * gw // WT_ROWS, 5 * gw // WT_ROWS

    def piece(part):
        sub = part % per_head
        def index(r):
            g = r * WT_PIECES + part
            hd = (g - qk_begin) // per_head
            qk = (q0 if sub < per_head // 2 else k0 - per_head // 2) + (per_head // 2) * hd + sub
            return 0, jnp.where(g < qk_begin, g, jnp.where(g < qk_end, qk, rest0 + g - qk_end)), 0
        return pl.BlockSpec((1, WT_ROWS, d), index)

    def gate_piece(row0):
        return pl.BlockSpec((1, WT_ROWS, d), lambda r: (0, row0 // WT_ROWS, 0))

    return pl.pallas_call(
        _weight_layout_kernel,
        grid=(8 * gw // (WT_PIECES * WT_ROWS),),
        in_specs=[piece(p) for p in range(WT_PIECES)] + [gate_piece(4 * gw), gate_piece(width - WT_ROWS)],
        out_specs=[pl.BlockSpec((WT_PIECES * WT_ROWS, d), lambda r: (r, 0)),
                   pl.BlockSpec((n_gate, d), lambda r: (0, 0))],
        out_shape=[jax.ShapeDtypeStruct((8 * gw, d), BF16),
                   jax.ShapeDtypeStruct((n_gate, d), BF16)],
        compiler_params=pltpu.CompilerParams(
            dimension_semantics=("arbitrary",), vmem_limit_bytes=VMEM_LIMIT_BYTES),
        name="weight_layout",
    )(*([w_t] * (WT_PIECES + 2)))


_CONTRACT_MINOR = (((1,), (1,)), ((), ()))


def _inproj_kernel(x_ref, npw_ref, w_ref, wg_ref, p_ref, gt_ref, h_scr, *, n_sub):
    j = pl.program_id(2)

    @pl.when(j == 0)
    def _():
        x = x_ref[0]
        ms = jnp.mean(x * x, axis=-1, keepdims=True)
        h = (x * lax.rsqrt(ms + RMS_EPS) * npw_ref[...]).astype(BF16)
        h_scr[...] = h
        gt_ref[0] = lax.dot_general(wg_ref[...], h, _CONTRACT_MINOR, preferred_element_type=F32)

    h = h_scr[...]
    for c in range(n_sub // 2):
        acc = lax.dot_general(h, w_ref[c * 256:(c + 1) * 256, :], _CONTRACT_MINOR,
                              preferred_element_type=F32)
        p_ref[0, 2 * c] = acc[:, :128].astype(p_ref.dtype)
        p_ref[0, 2 * c + 1] = acc[:, 128:].astype(p_ref.dtype)


def _in_proj(x, npw, w_main_t, w_gate_t, *, tm=2048, tn=2048):
    b, s, d = x.shape
    n_blocks = w_main_t.shape[0] // 128
    n_gate = w_gate_t.shape[0]
    n_sub = tn // 128
    return pl.pallas_call(
        functools.partial(_inproj_kernel, n_sub=n_sub),
        grid=(b, s // tm, w_main_t.shape[0] // tn),
        in_specs=[
            pl.BlockSpec((1, tm, d), lambda bi, si, j: (bi, si, 0)),
            pl.BlockSpec((1, d), lambda bi, si, j: (0, 0)),
            pl.BlockSpec((tn, d), lambda bi, si, j: (j, 0)),
            pl.BlockSpec(w_gate_t.shape, lambda bi, si, j: (0, 0)),
        ],
        out_specs=[
            pl.BlockSpec((1, n_sub, tm, 128), lambda bi, si, j: (bi, j, si, 0)),
            pl.BlockSpec((1, n_gate, tm), lambda bi, si, j: (bi, 0, si)),
        ],
        out_shape=[
            jax.ShapeDtypeStruct((b, n_blocks, s, 128), BF16),
            jax.ShapeDtypeStruct((b, n_gate, s), F32),
        ],
        scratch_shapes=[pltpu.VMEM((tm, d), BF16)],
        compiler_params=pltpu.CompilerParams(
            dimension_semantics=("arbitrary", "arbitrary", "arbitrary"),
            vmem_limit_bytes=VMEM_LIMIT_BYTES),
        name="in_proj",
    )(x, npw, w_main_t, w_gate_t)


def _gates_kernel(gt_ref, alog_ref, dtb_ref, gb_ref, gout_ref, mout_ref, *, seq):
    nc = seq // SCAN_CHUNK
    ch = SCAN_CHUNK
    nh = N_HEADS
    ii = lax.broadcasted_iota(jnp.int32, (ch, ch), 0)
    jj = lax.broadcasted_iota(jnp.int32, (ch, ch), 1)
    upper = (ii <= jj).astype(F32)
    lower = (ii >= jj).astype(F32)
    lane = lax.broadcasted_iota(jnp.int32, (nh, ch), 1)
    neg_a = -jnp.exp(alog_ref[...])
    dtb = dtb_ref[...]
    gb = gb_ref[...]
    zeros8 = jnp.zeros((nh, ch), F32)

    def bcast(col):
        return jnp.broadcast_to(col, (nh, ch))

    def local_body(i, carry):
        tiles = []
        for u in range(GATES_UNROLL):
            sl = pl.ds(pl.multiple_of((i * GATES_UNROLL + u) * ch, ch), ch)
            g = neg_a * _softplus(gt_ref[0, 0:2 * nh, sl] + dtb)
            lf = -_softplus(-(gt_ref[0, 6 * nh:8 * nh, sl] + gb[2 * nh:4 * nh]))
            tiles.append(dict(sl=sl, stacked=jnp.concatenate([g, lf], axis=0),
                              beta=_sigmoid(gt_ref[0, 2 * nh:4 * nh, sl]),
                              ib=gt_ref[0, 4 * nh:6 * nh, sl] + gb[0:2 * nh]))
        for t in tiles:
            t["cum"] = (_dot_exact(t["stacked"], upper), _dot_exact(t["stacked"], lower))
        for t in tiles:
            t["f_local"] = [t["cum"][d][2 * nh + nh * d:3 * nh + nh * d] for d in (0, 1)]
            t["a_local"] = [t["ib"][nh * d:nh * d + nh] - t["f_local"][d] for d in (0, 1)]
            t["a_max"] = list(t["a_local"])
        for sh in (1, 2, 4, 8, 16, 32, 64):
            for t in tiles:
                fwd, bwd = t["a_max"]
                t["a_max"] = [
                    jnp.maximum(fwd, jnp.where(lane >= sh, pltpu.roll(fwd, sh, 1), NEG_BIG)),
                    jnp.maximum(bwd, jnp.where(lane < ch - sh, pltpu.roll(bwd, ch - sh, 1), NEG_BIG)),
                ]
        for t in tiles:
            sl = t["sl"]
            for d in (0, 1):
                rows = slice(nh * d, nh * d + nh)
                gcum = t["cum"][d][rows]
                gend = bcast(t["cum"][0][rows, ch - 1:ch])
                gout_ref[0, 3 * d + 0, :, sl] = gcum
                gout_ref[0, 3 * d + 1, :, sl] = jnp.exp(gend - gcum)
                gout_ref[0, 3 * d + 2, :, sl] = jnp.exp(gend)
                gout_ref[0, 6 + 2 * d, :, sl] = t["beta"][rows]
                gout_ref[0, 7 + 2 * d, :, sl] = jnp.exp(gcum)
                end = ch - 1 if d == 0 else 0
                mout_ref[0, 3 * d + 0, :, sl] = t["a_local"][d]
                mout_ref[0, 6 + 3 * d, :, sl] = t["a_max"][d]
                mout_ref[0, 8 + 3 * d, :, sl] = t["f_local"][d]
                mout_ref[0, 12 + 2 * d, :, sl] = bcast(t["a_max"][d][:, end:end + 1])
                mout_ref[0, 13 + 2 * d, :, sl] = bcast(t["f_local"][d][:, end:end + 1])
            for q in range(10, GATE_ROWS):
                gout_ref[0, q, :, sl] = zeros8
        return carry

    def carry_part(c, carry, d):
        c_f, c_r = carry
        sl = pl.ds(pl.multiple_of(c * ch, ch), ch)
        a = mout_ref[0, 3 * d + 0, :, sl] - c_f
        r = jnp.maximum(mout_ref[0, 6 + 3 * d, :, sl] - c_f, c_r)
        fcum = mout_ref[0, 8 + 3 * d, :, sl] + c_f
        r_end = jnp.maximum(mout_ref[0, 12 + 2 * d, :, sl] - c_f, c_r)
        f_end = mout_ref[0, 13 + 2 * d, :, sl] + c_f
        mout_ref[0, 3 * d + 0, :, sl] = a
        mout_ref[0, 3 * d + 1, :, sl] = jnp.exp(a - r_end)
        mout_ref[0, 3 * d + 2, :, sl] = jnp.exp(c_r - r_end)
        mout_ref[0, 6 + 3 * d, :, sl] = r
        mout_ref[0, 7 + 3 * d, :, sl] = jnp.exp(c_r - r)
        mout_ref[0, 8 + 3 * d, :, sl] = jnp.exp(-(fcum + r))
        return f_end, r_end

    def carry_body(i, carry):
        return carry_part(i, carry[0], 0), carry_part(nc - 1 - i, carry[1], 1)

    lax.fori_loop(0, nc // GATES_UNROLL, local_body, 0)
    lax.fori_loop(0, nc, carry_body, ((zeros8, zeros8), (zeros8, zeros8)))


def _gates(gt, alog, dtb, gb):
    b, n_rows, s = gt.shape
    assert (s // SCAN_CHUNK) % GATES_UNROLL == 0
    out_block = pl.BlockSpec((1, GATE_ROWS, N_HEADS, s), lambda bi: (bi, 0, 0, 0))
    out_shape = jax.ShapeDtypeStruct((b, GATE_ROWS, N_HEADS, s), F32)
    return pl.pallas_call(
        functools.partial(_gates_kernel, seq=s),
        grid=(b,),
        in_specs=[
            pl.BlockSpec((1, n_rows, s), lambda bi: (bi, 0, 0)),
            pl.BlockSpec(alog.shape, lambda bi: (0, 0)),
            pl.BlockSpec(dtb.shape, lambda bi: (0, 0)),
            pl.BlockSpec(gb.shape, lambda bi: (0, 0)),
        ],
        out_specs=[out_block, out_block],
        out_shape=[out_shape, out_shape],
        compiler_params=pltpu.CompilerParams(
            dimension_semantics=("arbitrary",), vmem_limit_bytes=VMEM_LIMIT_BYTES),
        name="gates",
    )(gt, alog, dtb, gb)


def _fill_padded(pad_ref, ref, seq):
    ch = SCAN_CHUNK
    zeros = jnp.zeros((CONV_HALO, pad_ref.shape[-1]), F32)
    pad_ref[0:CONV_HALO, :] = zeros
    pad_ref[seq + CONV_HALO:seq + 2 * CONV_HALO, :] = zeros

    def body(c, carry):
        t0 = pl.multiple_of(c * ch, ch)
        pad_ref[pl.ds(t0 + CONV_HALO, ch), :] = ref[0, 0, pl.ds(t0, ch), :].astype(F32)
        return carry

    lax.fori_loop(0, seq // ch // 4, _unrolled(body, 4), 0)


def _conv_silu(pad_ref, c, w):
    ch = SCAN_CHUNK
    base = c * ch + (CONV_HALO - CONV_K // 2)
    acc = pad_ref[pl.ds(base, ch), :] * w[0:1]
    for j in range(1, CONV_K):
        acc = acc + pad_ref[pl.ds(base + j, ch), :] * w[j:j + 1]
    return _silu(acc)


def _gate_cols(rows):
    ch = SCAN_CHUNK
    return jnp.concatenate([rows, jnp.zeros((ch - GATE_ROWS, ch), F32)], axis=0).T


def _gate_rows_spec(seq):
    return pl.BlockSpec((1, GATE_ROWS, 1, 1, seq), lambda bi, hi: (bi, 0, hi, 0, 0))


def _unrolled(body, factor):
    def wrapped(i, carry):
        for u in range(factor):
            carry = body(i * factor + u, carry)
        return carry
    return wrapped


def _level_masks():
    ch = SCAN_CHUNK
    i = np.arange(ch)[:, None]
    j = np.arange(ch)[None, :]
    n_levels = ch.bit_length() - 1
    out = np.zeros((2, n_levels + 2, ch, ch), np.float32)
    for p in range(n_levels):
        pair = ((i ^ j) >> p) == 1
        out[0, p] = pair & (((i >> p) & 1) == 1)
        out[1, p] = pair & (((j >> p) & 1) == 1)
    out[0, n_levels] = i >= j
    out[1, n_levels] = i <= j
    out[0, n_levels + 1] = out[0, 0] + out[1, 0]
    return jnp.asarray(out, dtype=BF16)


def _gdn_kernel(q_ref, k_ref, v_ref, z_ref, row_ref, cwq_ref, cwk_ref, cwv_ref, nw_ref, lm_ref, out_ref,
                mq_scr, n_scr, o_scr, pad_scr, *, seq):
    ch = SCAN_CHUNK
    nc = seq // ch
    ii = lax.broadcasted_iota(jnp.int32, (ch, ch), 0)
    jj = lax.broadcasted_iota(jnp.int32, (ch, ch), 1)
    eye = (ii == jj).astype(F32)
    incl = (ii >= jj, ii <= jj)
    n_levels = ch.bit_length() - 1

    def prep_group(i, carry):
        chunks = []
        for u in range(GDN_PREP_UNROLL):
            c = i * GDN_PREP_UNROLL + u
            q0 = _conv_silu(pad_scr.at[0], c, cwq_ref[...])
            k0 = _conv_silu(pad_scr.at[1], c, cwk_ref[...])
            v = _conv_silu(pad_scr.at[2], c, cwv_ref[...])
            q = q0 * lax.rsqrt(jnp.sum(q0 * q0, axis=-1, keepdims=True) + L2_EPS) * (HEAD_DIM ** -0.5)
            k = k0 * lax.rsqrt(jnp.sum(k0 * k0, axis=-1, keepdims=True) + L2_EPS)
            kt = k.T
            chunks.append(dict(c=c, sl=pl.ds(pl.multiple_of(c * ch, ch), ch), q=q, k=k, v=v, kt=kt))
        for cd in chunks:
            ktb = cd["kt"].astype(BF16)
            cd["kk"] = _dot(cd["k"].astype(BF16), ktb)
            cd["qk"] = _dot(cd["q"].astype(BF16), ktb)
        chains = []
        for cd in chunks:
            rows = row_ref[0, :, 0, 0, cd["sl"]]
            cols = _gate_cols(rows)
            weight = None
            for d in (0, 1):
                g_col = cols[:, 3 * d:3 * d + 1]
                beta = cols[:, 6 + 2 * d:7 + 2 * d]
                decay = jnp.exp(jnp.where(incl[d], g_col - rows[3 * d:3 * d + 1], NEG_BIG))
                weight = decay * beta if weight is None else weight + decay * beta
                chains.append(dict(cd=cd, d=d, decay=decay, beta=beta,
                                   eg=cols[:, 7 + 2 * d:8 + 2 * d], et_row=rows[3 * d + 1:3 * d + 2]))
            cd["ab"] = (cd["kk"] * weight).astype(BF16)
            cd["t"] = eye - (cd["ab"] * lm_ref[0, n_levels + 1]).astype(F32)
        for p in range(1, n_levels):
            for cd in chunks:
                tb = cd["t"].astype(BF16)
                t_lo, t_up = tb * lm_ref[0, n_levels], tb * lm_ref[1, n_levels]
                l_both = jnp.concatenate([cd["ab"] * lm_ref[0, p], cd["ab"] * lm_ref[1, p]], axis=1)
                x = _dot(l_both, jnp.concatenate([t_lo, t_up], axis=0)).astype(BF16)
                cd["t_both"] = jnp.concatenate([t_lo, t_up], axis=1)
                cd["x_both"] = jnp.concatenate([x * lm_ref[0, p], x * lm_ref[1, p]], axis=0)
            for cd in chunks:
                cd["t"] = cd["t"] - _dot(cd["t_both"], cd["x_both"])
        for cn in chains:
            cd, beta = cn["cd"], cn["beta"]
            rhs = jnp.concatenate([cd["v"] * beta, cd["k"] * (beta * cn["eg"])], axis=1).astype(BF16)
            t_dir = cd["t"].astype(BF16) * lm_ref[cn["d"], n_levels]
            cn["sol"] = _dot(t_dir, rhs).astype(BF16)
        for cn in chains:
            cd = cn["cd"]
            cn["ks"] = _dot((cd["kt"] * cn["et_row"]).astype(BF16), cn["sol"])
            cn["aw"] = _dot((cd["qk"] * cn["decay"]).astype(BF16), cn["sol"])
        for cn in chains:
            cd, d = cn["cd"], cn["d"]
            n_scr[d, cd["sl"], :] = cn["ks"][:, :HEAD_DIM]
            o_scr[d, cd["sl"], :] = cn["aw"][:, :HEAD_DIM]
            mq_scr[d, cd["c"], 0:ch, :] = (-cn["ks"][:, HEAD_DIM:]).astype(BF16)
            mq_scr[d, cd["c"], ch:2 * ch, :] = (cd["q"] * cn["eg"] - cn["aw"][:, HEAD_DIM:]).astype(BF16)
        return carry

    def finish(sl, o):
        y = o * lax.rsqrt(jnp.mean(o * o, axis=-1, keepdims=True) + RMS_EPS) * nw_ref[...]
        out_ref[0, sl, :] = (y * _silu(z_ref[0, 0, sl, :].astype(F32))).astype(out_ref.dtype)

    def scan_body(i, states, *, finalize):
        steps = []
        for d, c in ((0, i), (1, nc - 1 - i)):
            steps.append((d, c, pl.ds(pl.multiple_of(c * ch, ch), ch), states[d], states[d].astype(BF16)))
        new_states = []
        for d, c, sl, s, sb in steps:
            chunk_decay = row_ref[0, 3 * d + 2:3 * d + 3, 0, 0, sl]
            new_states.append(s * chunk_decay + _dot(mq_scr[d, c, 0:ch, :], sb) + n_scr[d, sl, :])
        for d, c, sl, s, sb in steps:
            o = o_scr[d, sl, :] + _dot(mq_scr[d, c, ch:2 * ch, :], sb)
            if finalize:
                finish(sl, o + o_scr[1 - d, sl, :])
            else:
                o_scr[d, sl, :] = o
        return tuple(new_states)

    for t, ref in enumerate((q_ref, k_ref, v_ref)):
        _fill_padded(pad_scr.at[t], ref, seq)
    lax.fori_loop(0, nc // GDN_PREP_UNROLL, prep_group, 0)
    zero_state = jnp.zeros((HEAD_DIM, HEAD_DIM), F32)
    states = lax.fori_loop(0, nc // 4, _unrolled(functools.partial(scan_body, finalize=False), 2),
                           (zero_state, zero_state))
    lax.fori_loop(nc // 4, nc // 2, _unrolled(functools.partial(scan_body, finalize=True), 2), states)


def _gdn(p, rows, conv_w, norm_w, *, col0):
    b, _, s, _ = p.shape
    h = N_HEADS
    nc = s // SCAN_CHUNK
    assert nc % GDN_PREP_UNROLL == 0 and nc % 4 == 0
    level_masks = _level_masks()

    def pblock(off):
        return pl.BlockSpec((1, 1, s, HEAD_DIM), lambda bi, hi: (bi, col0 + off + hi, 0, 0))

    def cblock(off):
        return pl.BlockSpec((CONV_K, HEAD_DIM), lambda bi, hi: (0, off + hi))

    return pl.pallas_call(
        functools.partial(_gdn_kernel, seq=s),
        grid=(b, h),
        in_specs=[
            pblock(0), pblock(h), pblock(2 * h), pblock(3 * h),
            _gate_rows_spec(s),
            cblock(0), cblock(h), cblock(2 * h),
            pl.BlockSpec((1, HEAD_DIM), lambda bi, hi: (0, 0)),
            pl.BlockSpec(level_masks.shape, lambda bi, hi: (0, 0, 0, 0)),
        ],
        out_specs=pl.BlockSpec((1, s, HEAD_DIM), lambda bi, hi: (bi, 0, hi)),
        out_shape=jax.ShapeDtypeStruct((b, s, h * HEAD_DIM), BF16),
        scratch_shapes=[
            pltpu.VMEM((2, nc, 2 * SCAN_CHUNK, HEAD_DIM), BF16),
            pltpu.VMEM((2, s, HEAD_DIM), F32),
            pltpu.VMEM((2, s, HEAD_DIM), F32),
            pltpu.VMEM((3, s + 2 * CONV_HALO, HEAD_DIM), F32),
        ],
        compiler_params=pltpu.CompilerParams(
            dimension_semantics=("arbitrary", "arbitrary"), vmem_limit_bytes=VMEM_LIMIT_BYTES),
        name="gdn",
    )(p, p, p, p, rows, conv_w, conv_w, conv_w, norm_w, level_masks)


def _mlstm_kernel(qk_ref, v_ref, o_ref, z_ref, row_ref, cw_ref, nw_ref, out_ref,
                  qm_scr, kt_scr, qkm_scr, cols_scr, h_scr, pad_scr, *, seq):
    ch = SCAN_CHUNK
    nc = seq // ch
    ii = lax.broadcasted_iota(jnp.int32, (ch, ch), 0)
    jj = lax.broadcasted_iota(jnp.int32, (ch, ch), 1)
    incl = (ii >= jj, ii <= jj)
    ones = jnp.ones((ch, HEAD_DIM), BF16)

    def prep_body(c, carry):
        sl = pl.ds(pl.multiple_of(c * ch, ch), ch)
        t = _conv_silu(pad_scr, c, cw_ref[...])
        qm = jnp.where(jj < MLSTM_DQK, t, 0.0).astype(BF16)
        tt = t.T
        kt = jnp.concatenate([tt[MLSTM_DQK:], jnp.zeros((ch - MLSTM_DQK, ch), F32)], axis=0)
        ktb = (kt * (MLSTM_DQK ** -0.5)).astype(BF16)
        qm_scr[sl, :] = qm
        kt_scr[:, sl] = ktb
        qkm_scr[sl, :] = _dot(qm, ktb)
        cols_scr[sl, :] = _gate_cols(row_ref[0, :, 0, 0, sl])
        return carry

    def scan_group(i, states):
        items = []
        for u in range(MLSTM_UNROLL):
            step = i * MLSTM_UNROLL + u
            for d, c in ((0, step), (1, nc - 1 - step)):
                sl = pl.ds(pl.multiple_of(c * ch, ch), ch)
                items.append(dict(d=d, sl=sl, rows=row_ref[0, :, 0, 0, sl]))
        for it in items:
            d, sl, rows, cols = it["d"], it["sl"], it["rows"], cols_scr[it["sl"], :]
            r_col = cols[:, 6 + 3 * d:7 + 3 * d]
            it["inter"] = cols[:, 7 + 3 * d:8 + 3 * d]
            it["enm"] = cols[:, 8 + 3 * d:9 + 3 * d]
            dec_row = rows[3 * d + 2:3 * d + 3]
            it["dec"] = jnp.concatenate([dec_row, dec_row], axis=1)
            it["vaug"] = jnp.concatenate([v_ref[0, 0, sl, :], ones], axis=1)
            wt = jnp.exp(jnp.where(incl[d], rows[3 * d:3 * d + 1] - r_col, NEG_BIG)) * qkm_scr[sl, :]
            it["wt"] = wt.astype(BF16)
            awk = jnp.broadcast_to(rows[3 * d + 1:3 * d + 2], (ch, ch)).astype(BF16)
            it["wk"] = kt_scr[:, sl] * awk
        for it in items:
            it["intra"] = _dot(it["wt"], it["vaug"])
            it["update"] = _dot(it["wk"], it["vaug"])
        states = list(states)
        for it in items:
            d = it["d"]
            it["qs"] = _dot(qm_scr[it["sl"], :], states[d].astype(BF16))
            states[d] = states[d] * it["dec"] + it["update"]
        for it in items:
            nd = it["inter"] * it["qs"] + it["intra"]
            num = nd[:, :HEAD_DIM]
            den = nd[:, HEAD_DIM:]
            h_scr[it["d"], it["sl"], :] = num / jnp.maximum(jnp.abs(den), it["enm"])
        return tuple(states)

    def final_body(c, carry):
        sl = pl.ds(pl.multiple_of(c * ch, ch), ch)
        hh = h_scr[0, sl, :] + h_scr[1, sl, :]
        y = hh * lax.rsqrt(jnp.mean(hh * hh, axis=-1, keepdims=True) + RMS_EPS) * nw_ref[...]
        y = y * _sigmoid(o_ref[0, 0, sl, :].astype(F32)) * _silu(z_ref[0, 0, sl, :].astype(F32))
        out_ref[0, sl, :] = y.astype(out_ref.dtype)
        return carry

    n_iter = nc // MLSTM_UNROLL
    _fill_padded(pad_scr, qk_ref, seq)
    lax.fori_loop(0, n_iter, _unrolled(prep_body, MLSTM_UNROLL), 0)
    zero_state = jnp.zeros((ch, 2 * HEAD_DIM), F32)
    lax.fori_loop(0, n_iter, scan_group, (zero_state, zero_state))
    lax.fori_loop(0, n_iter, _unrolled(final_body, MLSTM_UNROLL), 0)


def _mlstm(p, rows, conv_w, norm_w, *, col0):
    b, _, s, _ = p.shape
    h = N_HEADS
    assert (s // SCAN_CHUNK) % MLSTM_UNROLL == 0

    def pblock(off):
        return pl.BlockSpec((1, 1, s, HEAD_DIM), lambda bi, hi: (bi, col0 + off + hi, 0, 0))

    return pl.pallas_call(
        functools.partial(_mlstm_kernel, seq=s),
        grid=(b, h),
        in_specs=[
            pblock(0), pblock(h), pblock(2 * h), pblock(3 * h),
            _gate_rows_spec(s),
            pl.BlockSpec((CONV_K, HEAD_DIM), lambda bi, hi: (0, hi)),
            pl.BlockSpec((1, HEAD_DIM), lambda bi, hi: (0, hi)),
        ],
        out_specs=pl.BlockSpec((1, s, HEAD_DIM), lambda bi, hi: (bi, 0, hi)),
        out_shape=jax.ShapeDtypeStruct((b, s, h * HEAD_DIM), BF16),
        scratch_shapes=[
            pltpu.VMEM((s, HEAD_DIM), BF16),
            pltpu.VMEM((HEAD_DIM, s), BF16),
            pltpu.VMEM((s, SCAN_CHUNK), F32),
            pltpu.VMEM((s, SCAN_CHUNK), F32),
            pltpu.VMEM((2, s, HEAD_DIM), F32),
            pltpu.VMEM((s + 2 * CONV_HALO, HEAD_DIM), F32),
        ],
        compiler_params=pltpu.CompilerParams(
            dimension_semantics=("arbitrary", "arbitrary"), vmem_limit_bytes=VMEM_LIMIT_BYTES),
        name="mlstm",
    )(p, p, p, p, rows, conv_w, norm_w)


def _outproj_kernel(g_ref, m_ref, wo_ref, x_ref, npw_ref, y_ref, *, width):
    mixed = _dot(g_ref[0], wo_ref[:width]) + _dot(m_ref[0], wo_ref[width:])
    ms = jnp.mean(mixed * mixed, axis=-1, keepdims=True)
    y_ref[0] = x_ref[0] + mixed * lax.rsqrt(ms + RMS_EPS) * npw_ref[...]


def _out_proj(g, m, w_out, x, npw, *, tm=512):
    b, s, d = x.shape
    width = g.shape[-1]
    return pl.pallas_call(
        functools.partial(_outproj_kernel, width=width),
        grid=(b, s // tm),
        in_specs=[
            pl.BlockSpec((1, tm, width), lambda bi, si: (bi, si, 0)),
            pl.BlockSpec((1, tm, width), lambda bi, si: (bi, si, 0)),
            pl.BlockSpec(w_out.shape, lambda bi, si: (0, 0)),
            pl.BlockSpec((1, tm, d), lambda bi, si: (bi, si, 0)),
            pl.BlockSpec((1, d), lambda bi, si: (0, 0)),
        ],
        out_specs=pl.BlockSpec((1, tm, d), lambda bi, si: (bi, si, 0)),
        out_shape=jax.ShapeDtypeStruct((b, s, d), F32),
        compiler_params=pltpu.CompilerParams(
            dimension_semantics=("arbitrary", "arbitrary"), vmem_limit_bytes=VMEM_LIMIT_BYTES),
        name="out_proj",
    )(g, m, w_out, x, npw)


def _layer(x, norm_pre_w, w_in, gdn_conv_w, gdn_a_log, gdn_dt_bias, gdn_norm_w,
           mlstm_conv_w, mlstm_gate_bias, mlstm_norm_w, w_out, norm_post_w):
    d = x.shape[-1]
    h = N_HEADS
    gw = h * HEAD_DIM
    mqk = h * MLSTM_DQK
    w_main_t, w_gate_t = _weight_layout(jnp.swapaxes(w_in, 1, 2))

    p, gt = _in_proj(x, norm_pre_w.reshape(1, d), w_main_t, w_gate_t)
    g_rows, m_rows = _gates(gt, gdn_a_log.reshape(2 * h, 1), gdn_dt_bias.reshape(2 * h, 1),
                            mlstm_gate_bias.reshape(4 * h, 1))
    g_rows = g_rows[:, :, :, None, :]
    m_rows = m_rows[:, :, :, None, :]

    cq = mlstm_conv_w[:, :mqk].reshape(CONV_K, h, MLSTM_DQK)
    ck = mlstm_conv_w[:, mqk:].reshape(CONV_K, h, MLSTM_DQK)
    m_conv = jnp.concatenate([cq, ck], axis=2).reshape(CONV_K, gw)

    g_out = _gdn(p, g_rows, gdn_conv_w, gdn_norm_w.reshape(1, HEAD_DIM), col0=0)
    m_out = _mlstm(p, m_rows, m_conv, mlstm_norm_w.reshape(1, gw), col0=4 * h)
    return _out_proj(g_out, m_out, w_out.astype(BF16), x, norm_post_w.reshape(1, d))


def kernel(x, norm_pre_w, w_in, gdn_conv_w, gdn_a_log, gdn_dt_bias, gdn_norm_w,
           mlstm_conv_w, mlstm_gate_bias, mlstm_norm_w, w_out, norm_post_w):
    for layer in range(norm_pre_w.shape[0]):
        x = _layer(x, norm_pre_w[layer], w_in[layer:layer + 1], gdn_conv_w[layer], gdn_a_log[layer],
                   gdn_dt_bias[layer], gdn_norm_w[layer], mlstm_conv_w[layer],
                   mlstm_gate_bias[layer], mlstm_norm_w[layer], w_out[layer], norm_post_w[layer])
    return x
```

```python
import functools

import jax
import jax.numpy as jnp
import numpy as np
from jax import lax
from jax.experimental import pallas as pl
from jax.experimental.pallas import tpu as pltpu

F32 = jnp.float32
BF16 = jnp.bfloat16

N_HEADS = 8
HEAD_DIM = 128
MLSTM_DQK = 64
CONV_K = 5
CONV_HALO = 8
SCAN_CHUNK = 128
GATE_ROWS = 16
WT_ROWS = 32
WT_PIECES = 16
GDN_PREP_UNROLL = 16
GDN_SCAN_UNROLL = 4
MLSTM_UNROLL = 16
GATES_UNROLL = 8
RMS_EPS = 1e-6
L2_EPS = 1e-6
NEG_BIG = -1e30
VMEM_LIMIT_BYTES = 56 * 1024 * 1024


def _dot(a, b):
    return jnp.dot(a, b, preferred_element_type=F32)


def _dot_exact(a, b):
    return jnp.dot(a, b, preferred_element_type=F32, precision=lax.Precision.HIGHEST)


def _softplus(x):
    return jnp.maximum(x, 0.0) + jnp.log1p(jnp.exp(-jnp.abs(x)))


def _sigmoid(x):
    return 0.5 * jnp.tanh(0.5 * x) + 0.5


def _silu(x):
    h = 0.5 * x
    return h * jnp.tanh(h) + h


def _weight_layout_kernel(*refs):
    parts, gate_parts, (main_ref, gate_ref) = refs[:WT_PIECES], refs[WT_PIECES:-2], refs[-2:]
    for i, part in enumerate(parts):
        main_ref[WT_ROWS * i:WT_ROWS * (i + 1), :] = part[0].astype(BF16)
    for i, part in enumerate(gate_parts):
        gate_ref[WT_ROWS * i:WT_ROWS * (i + 1), :] = part[0].astype(BF16)


def _weight_layout(w_t):
    _, width, d = w_t.shape
    gw = N_HEADS * HEAD_DIM
    mqk = N_HEADS * MLSTM_DQK
    n_gate = width - 8 * gw
    m0 = 4 * gw + n_gate // 2
    assert n_gate == 2 * WT_ROWS and m0 % WT_ROWS == 0 and MLSTM_DQK == 2 * WT_ROWS
    per_head = HEAD_DIM // WT_ROWS
    assert WT_PIECES % per_head == 0 and (8 * gw) % (WT_PIECES * WT_ROWS) == 0
    q0, k0, rest0 = m0 // WT_ROWS, (m0 + mqk) // WT_ROWS, (m0 + 2 * mqk) // WT_ROWS
    qk_begin, qk_end = 4 * gw // WT_ROWS, 5 * gw // WT_ROWS

    def piece(part):
        sub = part % per_head
        def index(r):
            g = r * WT_PIECES + part
            hd = (g - qk_begin) // per_head
            qk = (q0 if sub < per_head // 2 else k0 - per_head // 2) + (per_head // 2) * hd + sub
            return 0, jnp.where(g < qk_begin, g, jnp.where(g < qk_end, qk, rest0 + g - qk_end)), 0
        return pl.BlockSpec((1, WT_ROWS, d), index)

    def gate_piece(row0):
        return pl.BlockSpec((1, WT_ROWS, d), lambda r: (0, row0 // WT_ROWS, 0))

    return pl.pallas_call(
        _weight_layout_kernel,
        grid=(8 * gw // (WT_PIECES * WT_ROWS),),
        in_specs=[piece(p) for p in range(WT_PIECES)] + [gate_piece(4 * gw), gate_piece(width - WT_ROWS)],
        out_specs=[pl.BlockSpec((WT_PIECES * WT_ROWS, d), lambda r: (r, 0)),
                   pl.BlockSpec((n_gate, d), lambda r: (0, 0))],
        out_shape=[jax.ShapeDtypeStruct((8 * gw, d), BF16),
                   jax.ShapeDtypeStruct((n_gate, d), BF16)],
        compiler_params=pltpu.CompilerParams(
            dimension_semantics=("arbitrary",), vmem_limit_bytes=VMEM_LIMIT_BYTES),
        name="weight_layout",
    )(*([w_t] * (WT_PIECES + 2)))


_CONTRACT_MINOR = (((1,), (1,)), ((), ()))


def _inproj_kernel(x_ref, npw_ref, w_ref, wg_ref, p_ref, gt_ref, h_scr, *, n_sub):
    j = pl.program_id(2)

    @pl.when(j == 0)
    def _():
        x = x_ref[0]
        ms = jnp.mean(x * x, axis=-1, keepdims=True)
        h = (x * lax.rsqrt(ms + RMS_EPS) * npw_ref[...]).astype(BF16)
        h_scr[...] = h
        gt_ref[0] = lax.dot_general(wg_ref[...], h, _CONTRACT_MINOR, preferred_element_type=F32)

    h = h_scr[...]
    for c in range(n_sub // 2):
        acc = lax.dot_general(h, w_ref[c * 256:(c + 1) * 256, :], _CONTRACT_MINOR,
                              preferred_element_type=F32)
        p_ref[0, 2 * c] = acc[:, :128].astype(p_ref.dtype)
        p_ref[0, 2 * c + 1] = acc[:, 128:].astype(p_ref.dtype)


def _in_proj(x, npw, w_main_t, w_gate_t, *, tm=2048, tn=2048):
    b, s, d = x.shape
    n_blocks = w_main_t.shape[0] // 128
    n_gate = w_gate_t.shape[0]
    n_sub = tn // 128
    return pl.pallas_call(
        functools.partial(_inproj_kernel, n_sub=n_sub),
        grid=(b, s // tm, w_main_t.shape[0] // tn),
        in_specs=[
            pl.BlockSpec((1, tm, d), lambda bi, si, j: (bi, si, 0)),
            pl.BlockSpec((1, d), lambda bi, si, j: (0, 0)),
            pl.BlockSpec((tn, d), lambda bi, si, j: (j, 0)),
            pl.BlockSpec(w_gate_t.shape, lambda bi, si, j: (0, 0)),
        ],
        out_specs=[
            pl.BlockSpec((1, n_sub, tm, 128), lambda bi, si, j: (bi, j, si, 0)),
            pl.BlockSpec((1, n_gate, tm), lambda bi, si, j: (bi, 0, si)),
        ],
        out_shape=[
            jax.ShapeDtypeStruct((b, n_blocks, s, 128), BF16),
            jax.ShapeDtypeStruct((b, n_gate, s), F32),
        ],
        scratch_shapes=[pltpu.VMEM((tm, d), BF16)],
        compiler_params=pltpu.CompilerParams(
            dimension_semantics=("arbitrary", "arbitrary", "arbitrary"),
            vmem_limit_bytes=VMEM_LIMIT_BYTES),
        name="in_proj",
    )(x, npw, w_main_t, w_gate_t)


def _gates_kernel(gt_ref, alog_ref, dtb_ref, gb_ref, gout_ref, mout_ref, *, seq):
    nc = seq // SCAN_CHUNK
    ch = SCAN_CHUNK
    nh = N_HEADS
    ii = lax.broadcasted_iota(jnp.int32, (ch, ch), 0)
    jj = lax.broadcasted_iota(jnp.int32, (ch, ch), 1)
    upper = (ii <= jj).astype(F32)
    lower = (ii >= jj).astype(F32)
    lane = lax.broadcasted_iota(jnp.int32, (nh, ch), 1)
    neg_a = -jnp.exp(alog_ref[...])
    dtb = dtb_ref[...]
    gb = gb_ref[...]
    zeros8 = jnp.zeros((nh, ch), F32)

    def bcast(col):
        return jnp.broadcast_to(col, (nh, ch))

    def local_body(i, carry):
        tiles = []
        for u in range(GATES_UNROLL):
            sl = pl.ds(pl.multiple_of((i * GATES_UNROLL + u) * ch, ch), ch)
            g = neg_a * _softplus(gt_ref[0, 0:2 * nh, sl] + dtb)
            lf = -_softplus(-(gt_ref[0, 6 * nh:8 * nh, sl] + gb[2 * nh:4 * nh]))
            tiles.append(dict(sl=sl, stacked=jnp.concatenate([g, lf], axis=0),
                              beta=_sigmoid(gt_ref[0, 2 * nh:4 * nh, sl]),
                              ib=gt_ref[0, 4 * nh:6 * nh, sl] + gb[0:2 * nh]))
        for t in tiles:
            t["cum"] = (_dot_exact(t["stacked"], upper), _dot_exact(t["stacked"], lower))
        for t in tiles:
            t["f_local"] = [t["cum"][d][2 * nh + nh * d:3 * nh + nh * d] for d in (0, 1)]
            t["a_local"] = [t["ib"][nh * d:nh * d + nh] - t["f_local"][d] for d in (0, 1)]
            t["a_max"] = list(t["a_local"])
        for sh in (1, 2, 4, 8, 16, 32, 64):
            for t in tiles:
                fwd, bwd = t["a_max"]
                t["a_max"] = [
                    jnp.maximum(fwd, jnp.where(lane >= sh, pltpu.roll(fwd, sh, 1), NEG_BIG)),
                    jnp.maximum(bwd, jnp.where(lane < ch - sh, pltpu.roll(bwd, ch - sh, 1), NEG_BIG)),
                ]
        for t in tiles:
            sl = t["sl"]
            for d in (0, 1):
                rows = slice(nh * d, nh * d + nh)
                gcum = t["cum"][d][rows]
                gend = bcast(t["cum"][0][rows, ch - 1:ch])
                gout_ref[0, 3 * d + 0, :, sl] = gcum
                gout_ref[0, 3 * d + 1, :, sl] = jnp.exp(gend - gcum)
                gout_ref[0, 3 * d + 2, :, sl] = jnp.exp(gend)
                gout_ref[0, 6 + 2 * d, :, sl] = t["beta"][rows]
                gout_ref[0, 7 + 2 * d, :, sl] = jnp.exp(gcum)
                end = ch - 1 if d == 0 else 0
                mout_ref[0, 3 * d + 0, :, sl] = t["a_local"][d]
                mout_ref[0, 6 + 3 * d, :, sl] = t["a_max"][d]
                mout_ref[0, 8 + 3 * d, :, sl] = t["f_local"][d]
                mout_ref[0, 12 + 2 * d, :, sl] = bcast(t["a_max"][d][:, end:end + 1])
                mout_ref[0, 13 + 2 * d, :, sl] = bcast(t["f_local"][d][:, end:end + 1])
            for q in range(10, GATE_ROWS):
                gout_ref[0, q, :, sl] = zeros8
        return carry

    def carry_part(c, carry, d):
        c_f, c_r = carry
        sl = pl.ds(pl.multiple_of(c * ch, ch), ch)
        a = mout_ref[0, 3 * d + 0, :, sl] - c_f
        r = jnp.maximum(mout_ref[0, 6 + 3 * d, :, sl] - c_f, c_r)
        fcum = mout_ref[0, 8 + 3 * d, :, sl] + c_f
        r_end = jnp.maximum(mout_ref[0, 12 + 2 * d, :, sl] - c_f, c_r)
        f_end = mout_ref[0, 13 + 2 * d, :, sl] + c_f
        mout_ref[0, 3 * d + 0, :, sl] = a
        mout_ref[0, 3 * d + 1, :, sl] = jnp.exp(a - r_end)
        mout_ref[0, 3 * d + 2, :, sl] = jnp.exp(c_r - r_end)
        mout_ref[0, 6 + 3 * d, :, sl] = r
        mout_ref[0, 7 + 3 * d, :, sl] = jnp.exp(c_r - r)
        mout_ref[0, 8 + 3 * d, :, sl] = jnp.exp(-(fcum + r))
        return f_end, r_end

    def carry_body(i, carry):
        return carry_part(i, carry[0], 0), carry_part(nc - 1 - i, carry[1], 1)

    lax.fori_loop(0, nc // GATES_UNROLL, local_body, 0)
    lax.fori_loop(0, nc, carry_body, ((zeros8, zeros8), (zeros8, zeros8)))


def _gates(gt, alog, dtb, gb):
    b, n_rows, s = gt.shape
    assert (s // SCAN_CHUNK) % GATES_UNROLL == 0
    out_block = pl.BlockSpec((1, GATE_ROWS, N_HEADS, s), lambda bi: (bi, 0, 0, 0))
    out_shape = jax.ShapeDtypeStruct((b, GATE_ROWS, N_HEADS, s), F32)
    return pl.pallas_call(
        functools.partial(_gates_kernel, seq=s),
        grid=(b,),
        in_specs=[
            pl.BlockSpec((1, n_rows, s), lambda bi: (bi, 0, 0)),
            pl.BlockSpec(alog.shape, lambda bi: (0, 0)),
            pl.BlockSpec(dtb.shape, lambda bi: (0, 0)),
            pl.BlockSpec(gb.shape, lambda bi: (0, 0)),
        ],
        out_specs=[out_block, out_block],
        out_shape=[out_shape, out_shape],
        compiler_params=pltpu.CompilerParams(
            dimension_semantics=("arbitrary",), vmem_limit_bytes=VMEM_LIMIT_BYTES),
        name="gates",
    )(gt, alog, dtb, gb)


def _fill_padded(pad_ref, ref, seq):
    ch = SCAN_CHUNK
    zeros = jnp.zeros((CONV_HALO, pad_ref.shape[-1]), F32)
    pad_ref[0:CONV_HALO, :] = zeros
    pad_ref[seq + CONV_HALO:seq + 2 * CONV_HALO, :] = zeros

    def body(c, carry):
        t0 = pl.multiple_of(c * ch, ch)
        pad_ref[pl.ds(t0 + CONV_HALO, ch), :] = ref[0, 0, pl.ds(t0, ch), :].astype(F32)
        return carry

    lax.fori_loop(0, seq // ch // 4, _unrolled(body, 4), 0)


def _conv_silu(pad_ref, c, w):
    ch = SCAN_CHUNK
    base = c * ch + (CONV_HALO - CONV_K // 2)
    acc = pad_ref[pl.ds(base, ch), :] * w[0:1]
    for j in range(1, CONV_K):
        acc = acc + pad_ref[pl.ds(base + j, ch), :] * w[j:j + 1]
    return _silu(acc)


def _gate_cols(rows):
    ch = SCAN_CHUNK
    return jnp.concatenate([rows, jnp.zeros((ch - GATE_ROWS, ch), F32)], axis=0).T


def _gate_rows_spec(seq):
    return pl.BlockSpec((1, GATE_ROWS, 1, 1, seq), lambda bi, hi: (bi, 0, hi, 0, 0))


def _unrolled(body, factor):
    def wrapped(i, carry):
        for u in range(factor):
            carry = body(i * factor + u, carry)
        return carry
    return wrapped


def _level_masks():
    ch = SCAN_CHUNK
    i = np.arange(ch)[:, None]
    j = np.arange(ch)[None, :]
    n_levels = ch.bit_length() - 1
    out = np.zeros((2, n_levels + 2, ch, ch), np.float32)
    for p in range(n_levels):
        pair = ((i ^ j) >> p) == 1
        out[0, p] = pair & (((i >> p) & 1) == 1)
        out[1, p] = pair & (((j >> p) & 1) == 1)
    out[0, n_levels] = i >= j
    out[1, n_levels] = i <= j
    out[0, n_levels + 1] = out[0, 0] + out[1, 0]
    return jnp.asarray(out, dtype=BF16)


def _gdn_kernel(q_ref, k_ref, v_ref, z_ref, row_ref, cwq_ref, cwk_ref, cwv_ref, nw_ref, lm_ref, out_ref,
                mq_scr, n_scr, o_scr, pad_scr, *, seq):
    ch = SCAN_CHUNK
    nc = seq // ch
    ii = lax.broadcasted_iota(jnp.int32, (ch, ch), 0)
    jj = lax.broadcasted_iota(jnp.int32, (ch, ch), 1)
    eye = (ii == jj).astype(F32)
    incl = (ii >= jj, ii <= jj)
    n_levels = ch.bit_length() - 1

    def prep_group(i, carry):
        chunks = []
        for u in range(GDN_PREP_UNROLL):
            c = i * GDN_PREP_UNROLL + u
            q0 = _conv_silu(pad_scr.at[0], c, cwq_ref[...])
            k0 = _conv_silu(pad_scr.at[1], c, cwk_ref[...])
            v = _conv_silu(pad_scr.at[2], c, cwv_ref[...])
            q = q0 * lax.rsqrt(jnp.sum(q0 * q0, axis=-1, keepdims=True) + L2_EPS) * (HEAD_DIM ** -0.5)
            k = k0 * lax.rsqrt(jnp.sum(k0 * k0, axis=-1, keepdims=True) + L2_EPS)
            kt = k.T
            chunks.append(dict(c=c, sl=pl.ds(pl.multiple_of(c * ch, ch), ch), q=q, k=k, v=v, kt=kt))
        for cd in chunks:
            ktb = cd["kt"].astype(BF16)
            cd["kk"] = _dot(cd["k"].astype(BF16), ktb)
            cd["qk"] = _dot(cd["q"].astype(BF16), ktb)
        chains = []
        for cd in chunks:
            rows = row_ref[0, :, 0, 0, cd["sl"]]
            cols = _gate_cols(rows)
            weight = None
            for d in (0, 1):
                g_col = cols[:, 3 * d:3 * d + 1]
                beta = cols[:, 6 + 2 * d:7 + 2 * d]
                decay = jnp.exp(jnp.where(incl[d], g_col - rows[3 * d:3 * d + 1], NEG_BIG))
                weight = decay * beta if weight is None else weight + decay * beta
                chains.append(dict(cd=cd, d=d, decay=decay, beta=beta,
                                   eg=cols[:, 7 + 2 * d:8 + 2 * d], et_row=rows[3 * d + 1:3 * d + 2]))
            cd["ab"] = (cd["kk"] * weight).astype(BF16)
            cd["t"] = eye - (cd["ab"] * lm_ref[0, n_levels + 1]).astype(F32)
        for p in range(1, n_levels):
            for cd in chunks:
                tb = cd["t"].astype(BF16)
                t_lo, t_up = tb * lm_ref[0, n_levels], tb * lm_ref[1, n_levels]
                l_both = jnp.concatenate([cd["ab"] * lm_ref[0, p], cd["ab"] * lm_ref[1, p]], axis=1)
                x = _dot(l_both, jnp.concatenate([t_lo, t_up], axis=0)).astype(BF16)
                cd["t_both"] = jnp.concatenate([t_lo, t_up], axis=1)
                cd["x_both"] = jnp.concatenate([x * lm_ref[0, p], x * lm_ref[1, p]], axis=0)
            for cd in chunks:
                cd["t"] = cd["t"] - _dot(cd["t_both"], cd["x_both"])
        for cn in chains:
            cd, beta = cn["cd"], cn["beta"]
            rhs = jnp.concatenate([cd["v"] * beta, cd["k"] * (beta * cn["eg"])], axis=1).astype(BF16)
            t_dir = cd["t"].astype(BF16) * lm_ref[cn["d"], n_levels]
            cn["sol"] = _dot(t_dir, rhs).astype(BF16)
        for cn in chains:
            cd = cn["cd"]
            cn["ks"] = _dot((cd["kt"] * cn["et_row"]).astype(BF16), cn["sol"])
            cn["aw"] = _dot((cd["qk"] * cn["decay"]).astype(BF16), cn["sol"])
        for cn in chains:
            cd, d = cn["cd"], cn["d"]
            n_scr[d, cd["sl"], :] = cn["ks"][:, :HEAD_DIM]
            o_scr[d, cd["sl"], :] = cn["aw"][:, :HEAD_DIM]
            mq_scr[d, cd["c"], 0:ch, :] = (-cn["ks"][:, HEAD_DIM:]).astype(BF16)
            mq_scr[d, cd["c"], ch:2 * ch, :] = (cd["q"] * cn["eg"] - cn["aw"][:, HEAD_DIM:]).astype(BF16)
        return carry

    def finish(sl, o):
        y = o * lax.rsqrt(jnp.mean(o * o, axis=-1, keepdims=True) + RMS_EPS) * nw_ref[...]
        out_ref[0, sl, :] = (y * _silu(z_ref[0, 0, sl, :].astype(F32))).astype(out_ref.dtype)

    def scan_body(i, states, *, finalize):
        steps = []
        for d, c in ((0, i), (1, nc - 1 - i)):
            steps.append((d, c, pl.ds(pl.multiple_of(c * ch, ch), ch), states[d], states[d].astype(BF16)))
        new_states = []
        for d, c, sl, s, sb in steps:
            chunk_decay = row_ref[0, 3 * d + 2:3 * d + 3, 0, 0, sl]
            new_states.append(s * chunk_decay + _dot(mq_scr[d, c, 0:ch, :], sb) + n_scr[d, sl, :])
        for d, c, sl, s, sb in steps:
            o = o_scr[d, sl, :] + _dot(mq_scr[d, c, ch:2 * ch, :], sb)
            if finalize:
                finish(sl, o + o_scr[1 - d, sl, :])
            else:
                o_scr[d, sl, :] = o
        return tuple(new_states)

    for t, ref in enumerate((q_ref, k_ref, v_ref)):
        _fill_padded(pad_scr.at[t], ref, seq)
    lax.fori_loop(0, nc // GDN_PREP_UNROLL, prep_group, 0)
    zero_state = jnp.zeros((HEAD_DIM, HEAD_DIM), F32)
    half = nc // (2 * GDN_SCAN_UNROLL)
    states = lax.fori_loop(0, half, _unrolled(functools.partial(scan_body, finalize=False), GDN_SCAN_UNROLL),
                           (zero_state, zero_state))
    lax.fori_loop(half, 2 * half, _unrolled(functools.partial(scan_body, finalize=True), GDN_SCAN_UNROLL),
                  states)


def _gdn(p, rows, conv_w, norm_w, *, col0):
    b, _, s, _ = p.shape
    h = N_HEADS
    nc = s // SCAN_CHUNK
    assert nc % GDN_PREP_UNROLL == 0 and nc % (2 * GDN_SCAN_UNROLL) == 0
    level_masks = _level_masks()

    def pblock(off):
        return pl.BlockSpec((1, 1, s, HEAD_DIM), lambda bi, hi: (bi, col0 + off + hi, 0, 0))

    def cblock(off):
        return pl.BlockSpec((CONV_K, HEAD_DIM), lambda bi, hi: (0, off + hi))

    return pl.pallas_call(
        functools.partial(_gdn_kernel, seq=s),
        grid=(b, h),
        in_specs=[
            pblock(0), pblock(h), pblock(2 * h), pblock(3 * h),
            _gate_rows_spec(s),
            cblock(0), cblock(h), cblock(2 * h),
            pl.BlockSpec((1, HEAD_DIM), lambda bi, hi: (0, 0)),
            pl.BlockSpec(level_masks.shape, lambda bi, hi: (0, 0, 0, 0)),
        ],
        out_specs=pl.BlockSpec((1, s, HEAD_DIM), lambda bi, hi: (bi, 0, hi)),
        out_shape=jax.ShapeDtypeStruct((b, s, h * HEAD_DIM), BF16),
        scratch_shapes=[
            pltpu.VMEM((2, nc, 2 * SCAN_CHUNK, HEAD_DIM), BF16),
            pltpu.VMEM((2, s, HEAD_DIM), F32),
            pltpu.VMEM((2, s, HEAD_DIM), F32),
            pltpu.VMEM((3, s + 2 * CONV_HALO, HEAD_DIM), F32),
        ],
        compiler_params=pltpu.CompilerParams(
            dimension_semantics=("arbitrary", "arbitrary"), vmem_limit_bytes=VMEM_LIMIT_BYTES),
        name="gdn",
    )(p, p, p, p, rows, conv_w, conv_w, conv_w, norm_w, level_masks)


def _mlstm_kernel(qk_ref, v_ref, o_ref, z_ref, row_ref, cw_ref, nw_ref, out_ref,
                  qm_scr, kt_scr, qkm_scr, cols_scr, h_scr, pad_scr, *, seq):
    ch = SCAN_CHUNK
    nc = seq // ch
    ii = lax.broadcasted_iota(jnp.int32, (ch, ch), 0)
    jj = lax.broadcasted_iota(jnp.int32, (ch, ch), 1)
    incl = (ii >= jj, ii <= jj)
    ones = jnp.ones((ch, HEAD_DIM), BF16)

    def prep_body(c, carry):
        sl = pl.ds(pl.multiple_of(c * ch, ch), ch)
        t = _conv_silu(pad_scr, c, cw_ref[...])
        qm = jnp.where(jj < MLSTM_DQK, t, 0.0).astype(BF16)
        tt = t.T
        kt = jnp.concatenate([tt[MLSTM_DQK:], jnp.zeros((ch - MLSTM_DQK, ch), F32)], axis=0)
        ktb = (kt * (MLSTM_DQK ** -0.5)).astype(BF16)
        qm_scr[sl, :] = qm
        kt_scr[:, sl] = ktb
        qkm_scr[sl, :] = _dot(qm, ktb)
        cols_scr[sl, :] = _gate_cols(row_ref[0, :, 0, 0, sl])
        return carry

    def scan_group(i, states):
        items = []
        for u in range(MLSTM_UNROLL):
            step = i * MLSTM_UNROLL + u
            for d, c in ((0, step), (1, nc - 1 - step)):
                sl = pl.ds(pl.multiple_of(c * ch, ch), ch)
                items.append(dict(d=d, sl=sl, rows=row_ref[0, :, 0, 0, sl]))
        for it in items:
            d, sl, rows, cols = it["d"], it["sl"], it["rows"], cols_scr[it["sl"], :]
            r_col = cols[:, 6 + 3 * d:7 + 3 * d]
            it["inter"] = cols[:, 7 + 3 * d:8 + 3 * d]
            it["enm"] = cols[:, 8 + 3 * d:9 + 3 * d]
            dec_row = rows[3 * d + 2:3 * d + 3]
            it["dec"] = jnp.concatenate([dec_row, dec_row], axis=1)
            it["vaug"] = jnp.concatenate([v_ref[0, 0, sl, :], ones], axis=1)
            wt = jnp.exp(jnp.where(incl[d], rows[3 * d:3 * d + 1] - r_col, NEG_BIG)) * qkm_scr[sl, :]
            it["wt"] = wt.astype(BF16)
            awk = jnp.broadcast_to(rows[3 * d + 1:3 * d + 2], (ch, ch)).astype(BF16)
            it["wk"] = kt_scr[:, sl] * awk
        for it in items:
            it["intra"] = _dot(it["wt"], it["vaug"])
            it["update"] = _dot(it["wk"], it["vaug"])
        states = list(states)
        for it in items:
            d = it["d"]
            it["qs"] = _dot(qm_scr[it["sl"], :], states[d].astype(BF16))
            states[d] = states[d] * it["dec"] + it["update"]
        for it in items:
            nd = it["inter"] * it["qs"] + it["intra"]
            num = nd[:, :HEAD_DIM]
            den = nd[:, HEAD_DIM:]
            h_scr[it["d"], it["sl"], :] = num / jnp.maximum(jnp.abs(den), it["enm"])
        return tuple(states)

    def final_body(c, carry):
        sl = pl.ds(pl.multiple_of(c * ch, ch), ch)
        hh = h_scr[0, sl, :] + h_scr[1, sl, :]
        y = hh * lax.rsqrt(jnp.mean(hh * hh, axis=-1, keepdims=True) + RMS_EPS) * nw_ref[...]
        y = y * _sigmoid(o_ref[0, 0, sl, :].astype(F32)) * _silu(z_ref[0, 0, sl, :].astype(F32))
        out_ref[0, sl, :] = y.astype(out_ref.dtype)
        return carry

    n_iter = nc // MLSTM_UNROLL
    _fill_padded(pad_scr, qk_ref, seq)
    lax.fori_loop(0, n_iter, _unrolled(prep_body, MLSTM_UNROLL), 0)
    zero_state = jnp.zeros((ch, 2 * HEAD_DIM), F32)
    lax.fori_loop(0, n_iter, scan_group, (zero_state, zero_state))
    lax.fori_loop(0, n_iter, _unrolled(final_body, MLSTM_UNROLL), 0)


def _mlstm(p, rows, conv_w, norm_w, *, col0):
    b, _, s, _ = p.shape
    h = N_HEADS
    assert (s // SCAN_CHUNK) % MLSTM_UNROLL == 0

    def pblock(off):
        return pl.BlockSpec((1, 1, s, HEAD_DIM), lambda bi, hi: (bi, col0 + off + hi, 0, 0))

    return pl.pallas_call(
        functools.partial(_mlstm_kernel, seq=s),
        grid=(b, h),
        in_specs=[
            pblock(0), pblock(h), pblock(2 * h), pblock(3 * h),
            _gate_rows_spec(s),
            pl.BlockSpec((CONV_K, HEAD_DIM), lambda bi, hi: (0, hi)),
            pl.BlockSpec((1, HEAD_DIM), lambda bi, hi: (0, hi)),
        ],
        out_specs=pl.BlockSpec((1, s, HEAD_DIM), lambda bi, hi: (bi, 0, hi)),
        out_shape=jax.ShapeDtypeStruct((b, s, h * HEAD_DIM), BF16),
        scratch_shapes=[
            pltpu.VMEM((s, HEAD_DIM), BF16),
            pltpu.VMEM((HEAD_DIM, s), BF16),
            pltpu.VMEM((s, SCAN_CHUNK), F32),
            pltpu.VMEM((s, SCAN_CHUNK), F32),
            pltpu.VMEM((2, s, HEAD_DIM), F32),
            pltpu.VMEM((s + 2 * CONV_HALO, HEAD_DIM), F32),
        ],
        compiler_params=pltpu.CompilerParams(
            dimension_semantics=("arbitrary", "arbitrary"), vmem_limit_bytes=VMEM_LIMIT_BYTES),
        name="mlstm",
    )(p, p, p, p, rows, conv_w, norm_w)


def _outproj_kernel(g_ref, m_ref, wo_ref, x_ref, npw_ref, y_ref, *, width):
    mixed = _dot(g_ref[0], wo_ref[:width]) + _dot(m_ref[0], wo_ref[width:])
    ms = jnp.mean(mixed * mixed, axis=-1, keepdims=True)
    y_ref[0] = x_ref[0] + mixed * lax.rsqrt(ms + RMS_EPS) * npw_ref[...]


def _out_proj(g, m, w_out, x, npw, *, tm=512):
    b, s, d = x.shape
    width = g.shape[-1]
    return pl.pallas_call(
        functools.partial(_outproj_kernel, width=width),
        grid=(b, s // tm),
        in_specs=[
            pl.BlockSpec((1, tm, width), lambda bi, si: (bi, si, 0)),
            pl.BlockSpec((1, tm, width), lambda bi, si: (bi, si, 0)),
            pl.BlockSpec(w_out.shape, lambda bi, si: (0, 0)),
            pl.BlockSpec((1, tm, d), lambda bi, si: (bi, si, 0)),
            pl.BlockSpec((1, d), lambda bi, si: (0, 0)),
        ],
        out_specs=pl.BlockSpec((1, tm, d), lambda bi, si: (bi, si, 0)),
        out_shape=jax.ShapeDtypeStruct((b, s, d), F32),
        compiler_params=pltpu.CompilerParams(
            dimension_semantics=("arbitrary", "arbitrary"), vmem_limit_bytes=VMEM_LIMIT_BYTES),
        name="out_proj",
    )(g, m, w_out, x, npw)


def _layer(x, norm_pre_w, w_in, gdn_conv_w, gdn_a_log, gdn_dt_bias, gdn_norm_w,
           mlstm_conv_w, mlstm_gate_bias, mlstm_norm_w, w_out, norm_post_w):
    d = x.shape[-1]
    h = N_HEADS
    gw = h * HEAD_DIM
    mqk = h * MLSTM_DQK
    w_main_t, w_gate_t = _weight_layout(jnp.swapaxes(w_in, 1, 2))

    p, gt = _in_proj(x, norm_pre_w.reshape(1, d), w_main_t, w_gate_t)
    g_rows, m_rows = _gates(gt, gdn_a_log.reshape(2 * h, 1), gdn_dt_bias.reshape(2 * h, 1),
                            mlstm_gate_bias.reshape(4 * h, 1))
    g_rows = g_rows[:, :, :, None, :]
    m_rows = m_rows[:, :, :, None, :]

    cq = mlstm_conv_w[:, :mqk].reshape(CONV_K, h, MLSTM_DQK)
    ck = mlstm_conv_w[:, mqk:].reshape(CONV_K, h, MLSTM_DQK)
    m_conv = jnp.concatenate([cq, ck], axis=2).reshape(CONV_K, gw)

    g_out = _gdn(p, g_rows, gdn_conv_w, gdn_norm_w.reshape(1, HEAD_DIM), col0=0)
    m_out = _mlstm(p, m_rows, m_conv, mlstm_norm_w.reshape(1, gw), col0=4 * h)
    return _out_proj(g_out, m_out, w_out.astype(BF16), x, norm_post_w.reshape(1, d))


def kernel(x, norm_pre_w, w_in, gdn_conv_w, gdn_a_log, gdn_dt_bias, gdn_norm_w,
           mlstm_conv_w, mlstm_gate_bias, mlstm_norm_w, w_out, norm_post_w):
    for layer in range(norm_pre_w.shape[0]):
        x = _layer(x, norm_pre_w[layer], w_in[layer:layer + 1], gdn_conv_w[layer], gdn_a_log[layer],
                   gdn_dt_bias[layer], gdn_norm_w[layer], mlstm_conv_w[layer],
                   mlstm_gate_bias[layer], mlstm_norm_w[layer], w_out[layer], norm_post_w[layer])
    return x
```

```python
import functools

import jax
import jax.numpy as jnp
import numpy as np
from jax import lax
from jax.experimental import pallas as pl
from jax.experimental.pallas import tpu as pltpu

F32 = jnp.float32
BF16 = jnp.bfloat16

N_HEADS = 8
HEAD_DIM = 128
MLSTM_DQK = 64
CONV_K = 5
CONV_HALO = 8
SCAN_CHUNK = 128
GATE_ROWS = 16
WT_ROWS = 32
WT_PIECES = 16
GDN_PREP_UNROLL = 16
GDN_SCAN_UNROLL = 8
MLSTM_UNROLL = 16
GATES_UNROLL = 8
RMS_EPS = 1e-6
L2_EPS = 1e-6
NEG_BIG = -1e30
VMEM_LIMIT_BYTES = 56 * 1024 * 1024


def _dot(a, b):
    return jnp.dot(a, b, preferred_element_type=F32)


def _dot_exact(a, b):
    return jnp.dot(a, b, preferred_element_type=F32, precision=lax.Precision.HIGHEST)


def _softplus(x):
    return jnp.maximum(x, 0.0) + jnp.log1p(jnp.exp(-jnp.abs(x)))


def _sigmoid(x):
    return 0.5 * jnp.tanh(0.5 * x) + 0.5


def _silu(x):
    h = 0.5 * x
    return h * jnp.tanh(h) + h


def _weight_layout_kernel(*refs):
    parts, gate_parts, (main_ref, gate_ref) = refs[:WT_PIECES], refs[WT_PIECES:-2], refs[-2:]
    for i, part in enumerate(parts):
        main_ref[WT_ROWS * i:WT_ROWS * (i + 1), :] = part[0].astype(BF16)
    for i, part in enumerate(gate_parts):
        gate_ref[WT_ROWS * i:WT_ROWS * (i + 1), :] = part[0].astype(BF16)


def _weight_layout(w_t):
    _, width, d = w_t.shape
    gw = N_HEADS * HEAD_DIM
    mqk = N_HEADS * MLSTM_DQK
    n_gate = width - 8 * gw
    m0 = 4 * gw + n_gate // 2
    assert n_gate == 2 * WT_ROWS and m0 % WT_ROWS == 0 and MLSTM_DQK == 2 * WT_ROWS
    per_head = HEAD_DIM // WT_ROWS
    assert WT_PIECES % per_head == 0 and (8 * gw) % (WT_PIECES * WT_ROWS) == 0
    q0, k0, rest0 = m0 // WT_ROWS, (m0 + mqk) // WT_ROWS, (m0 + 2 * mqk) // WT_ROWS
    qk_begin, qk_end = 4 * gw // WT_ROWS, 5 * gw // WT_ROWS

    def piece(part):
        sub = part % per_head
        def index(r):
            g = r * WT_PIECES + part
            hd = (g - qk_begin) // per_head
            qk = (q0 if sub < per_head // 2 else k0 - per_head // 2) + (per_head // 2) * hd + sub
            return 0, jnp.where(g < qk_begin, g, jnp.where(g < qk_end, qk, rest0 + g - qk_end)), 0
        return pl.BlockSpec((1, WT_ROWS, d), index)

    def gate_piece(row0):
        return pl.BlockSpec((1, WT_ROWS, d), lambda r: (0, row0 // WT_ROWS, 0))

    return pl.pallas_call(
        _weight_layout_kernel,
        grid=(8 * gw // (WT_PIECES * WT_ROWS),),
        in_specs=[piece(p) for p in range(WT_PIECES)] + [gate_piece(4 * gw), gate_piece(width - WT_ROWS)],
        out_specs=[pl.BlockSpec((WT_PIECES * WT_ROWS, d), lambda r: (r, 0)),
                   pl.BlockSpec((n_gate, d), lambda r: (0, 0))],
        out_shape=[jax.ShapeDtypeStruct((8 * gw, d), BF16),
                   jax.ShapeDtypeStruct((n_gate, d), BF16)],
        compiler_params=pltpu.CompilerParams(
            dimension_semantics=("arbitrary",), vmem_limit_bytes=VMEM_LIMIT_BYTES),
        name="weight_layout",
    )(*([w_t] * (WT_PIECES + 2)))


_CONTRACT_MINOR = (((1,), (1,)), ((), ()))


def _inproj_kernel(x_ref, npw_ref, w_ref, wg_ref, p_ref, gt_ref, h_scr, *, n_sub):
    j = pl.program_id(2)

    @pl.when(j == 0)
    def _():
        x = x_ref[0]
        ms = jnp.mean(x * x, axis=-1, keepdims=True)
        h = (x * lax.rsqrt(ms + RMS_EPS) * npw_ref[...]).astype(BF16)
        h_scr[...] = h
        gt_ref[0] = lax.dot_general(wg_ref[...], h, _CONTRACT_MINOR, preferred_element_type=F32)

    h = h_scr[...]
    for c in range(n_sub // 2):
        acc = lax.dot_general(h, w_ref[c * 256:(c + 1) * 256, :], _CONTRACT_MINOR,
                              preferred_element_type=F32)
        p_ref[0, 2 * c] = acc[:, :128].astype(p_ref.dtype)
        p_ref[0, 2 * c + 1] = acc[:, 128:].astype(p_ref.dtype)


def _in_proj(x, npw, w_main_t, w_gate_t, *, tm=2048, tn=2048):
    b, s, d = x.shape
    n_blocks = w_main_t.shape[0] // 128
    n_gate = w_gate_t.shape[0]
    n_sub = tn // 128
    return pl.pallas_call(
        functools.partial(_inproj_kernel, n_sub=n_sub),
        grid=(b, s // tm, w_main_t.shape[0] // tn),
        in_specs=[
            pl.BlockSpec((1, tm, d), lambda bi, si, j: (bi, si, 0)),
            pl.BlockSpec((1, d), lambda bi, si, j: (0, 0)),
            pl.BlockSpec((tn, d), lambda bi, si, j: (j, 0)),
            pl.BlockSpec(w_gate_t.shape, lambda bi, si, j: (0, 0)),
        ],
        out_specs=[
            pl.BlockSpec((1, n_sub, tm, 128), lambda bi, si, j: (bi, j, si, 0)),
            pl.BlockSpec((1, n_gate, tm), lambda bi, si, j: (bi, 0, si)),
        ],
        out_shape=[
            jax.ShapeDtypeStruct((b, n_blocks, s, 128), BF16),
            jax.ShapeDtypeStruct((b, n_gate, s), F32),
        ],
        scratch_shapes=[pltpu.VMEM((tm, d), BF16)],
        compiler_params=pltpu.CompilerParams(
            dimension_semantics=("arbitrary", "arbitrary", "arbitrary"),
            vmem_limit_bytes=VMEM_LIMIT_BYTES),
        name="in_proj",
    )(x, npw, w_main_t, w_gate_t)


def _gates_kernel(gt_ref, alog_ref, dtb_ref, gb_ref, gout_ref, mout_ref, *, seq):
    nc = seq // SCAN_CHUNK
    ch = SCAN_CHUNK
    nh = N_HEADS
    ii = lax.broadcasted_iota(jnp.int32, (ch, ch), 0)
    jj = lax.broadcasted_iota(jnp.int32, (ch, ch), 1)
    upper = (ii <= jj).astype(F32)
    lower = (ii >= jj).astype(F32)
    lane = lax.broadcasted_iota(jnp.int32, (nh, ch), 1)
    neg_a = -jnp.exp(alog_ref[...])
    dtb = dtb_ref[...]
    gb = gb_ref[...]
    zeros8 = jnp.zeros((nh, ch), F32)

    def bcast(col):
        return jnp.broadcast_to(col, (nh, ch))

    def local_body(i, carry):
        tiles = []
        for u in range(GATES_UNROLL):
            sl = pl.ds(pl.multiple_of((i * GATES_UNROLL + u) * ch, ch), ch)
            g = neg_a * _softplus(gt_ref[0, 0:2 * nh, sl] + dtb)
            lf = -_softplus(-(gt_ref[0, 6 * nh:8 * nh, sl] + gb[2 * nh:4 * nh]))
            tiles.append(dict(sl=sl, stacked=jnp.concatenate([g, lf], axis=0),
                              beta=_sigmoid(gt_ref[0, 2 * nh:4 * nh, sl]),
                              ib=gt_ref[0, 4 * nh:6 * nh, sl] + gb[0:2 * nh]))
        for t in tiles:
            t["cum"] = (_dot_exact(t["stacked"], upper), _dot_exact(t["stacked"], lower))
        for t in tiles:
            t["f_local"] = [t["cum"][d][2 * nh + nh * d:3 * nh + nh * d] for d in (0, 1)]
            t["a_local"] = [t["ib"][nh * d:nh * d + nh] - t["f_local"][d] for d in (0, 1)]
            t["a_max"] = list(t["a_local"])
        for sh in (1, 2, 4, 8, 16, 32, 64):
            for t in tiles:
                fwd, bwd = t["a_max"]
                t["a_max"] = [
                    jnp.maximum(fwd, jnp.where(lane >= sh, pltpu.roll(fwd, sh, 1), NEG_BIG)),
                    jnp.maximum(bwd, jnp.where(lane < ch - sh, pltpu.roll(bwd, ch - sh, 1), NEG_BIG)),
                ]
        for t in tiles:
            sl = t["sl"]
            for d in (0, 1):
                rows = slice(nh * d, nh * d + nh)
                gcum = t["cum"][d][rows]
                gend = bcast(t["cum"][0][rows, ch - 1:ch])
                gout_ref[0, 3 * d + 0, :, sl] = gcum
                gout_ref[0, 3 * d + 1, :, sl] = jnp.exp(gend - gcum)
                gout_ref[0, 3 * d + 2, :, sl] = jnp.exp(gend)
                gout_ref[0, 6 + 2 * d, :, sl] = t["beta"][rows]
                gout_ref[0, 7 + 2 * d, :, sl] = jnp.exp(gcum)
                end = ch - 1 if d == 0 else 0
                mout_ref[0, 3 * d + 0, :, sl] = t["a_local"][d]
                mout_ref[0, 6 + 3 * d, :, sl] = t["a_max"][d]
                mout_ref[0, 8 + 3 * d, :, sl] = t["f_local"][d]
                mout_ref[0, 12 + 2 * d, :, sl] = bcast(t["a_max"][d][:, end:end + 1])
                mout_ref[0, 13 + 2 * d, :, sl] = bcast(t["f_local"][d][:, end:end + 1])
            for q in range(10, GATE_ROWS):
                gout_ref[0, q, :, sl] = zeros8
        return carry

    def carry_part(c, carry, d):
        c_f, c_r = carry
        sl = pl.ds(pl.multiple_of(c * ch, ch), ch)
        a = mout_ref[0, 3 * d + 0, :, sl] - c_f
        r = jnp.maximum(mout_ref[0, 6 + 3 * d, :, sl] - c_f, c_r)
        fcum = mout_ref[0, 8 + 3 * d, :, sl] + c_f
        r_end = jnp.maximum(mout_ref[0, 12 + 2 * d, :, sl] - c_f, c_r)
        f_end = mout_ref[0, 13 + 2 * d, :, sl] + c_f
        mout_ref[0, 3 * d + 0, :, sl] = a
        mout_ref[0, 3 * d + 1, :, sl] = jnp.exp(a - r_end)
        mout_ref[0, 3 * d + 2, :, sl] = jnp.exp(c_r - r_end)
        mout_ref[0, 6 + 3 * d, :, sl] = r
        mout_ref[0, 7 + 3 * d, :, sl] = jnp.exp(c_r - r)
        mout_ref[0, 8 + 3 * d, :, sl] = jnp.exp(-(fcum + r))
        return f_end, r_end

    def carry_body(i, carry):
        return carry_part(i, carry[0], 0), carry_part(nc - 1 - i, carry[1], 1)

    lax.fori_loop(0, nc // GATES_UNROLL, local_body, 0)
    lax.fori_loop(0, nc, carry_body, ((zeros8, zeros8), (zeros8, zeros8)))


def _gates(gt, alog, dtb, gb):
    b, n_rows, s = gt.shape
    assert (s // SCAN_CHUNK) % GATES_UNROLL == 0
    out_block = pl.BlockSpec((1, GATE_ROWS, N_HEADS, s), lambda bi: (bi, 0, 0, 0))
    out_shape = jax.ShapeDtypeStruct((b, GATE_ROWS, N_HEADS, s), F32)
    return pl.pallas_call(
        functools.partial(_gates_kernel, seq=s),
        grid=(b,),
        in_specs=[
            pl.BlockSpec((1, n_rows, s), lambda bi: (bi, 0, 0)),
            pl.BlockSpec(alog.shape, lambda bi: (0, 0)),
            pl.BlockSpec(dtb.shape, lambda bi: (0, 0)),
            pl.BlockSpec(gb.shape, lambda bi: (0, 0)),
        ],
        out_specs=[out_block, out_block],
        out_shape=[out_shape, out_shape],
        compiler_params=pltpu.CompilerParams(
            dimension_semantics=("arbitrary",), vmem_limit_bytes=VMEM_LIMIT_BYTES),
        name="gates",
    )(gt, alog, dtb, gb)


def _fill_padded(pad_ref, ref, seq):
    ch = SCAN_CHUNK
    zeros = jnp.zeros((CONV_HALO, pad_ref.shape[-1]), F32)
    pad_ref[0:CONV_HALO, :] = zeros
    pad_ref[seq + CONV_HALO:seq + 2 * CONV_HALO, :] = zeros

    def body(c, carry):
        t0 = pl.multiple_of(c * ch, ch)
        pad_ref[pl.ds(t0 + CONV_HALO, ch), :] = ref[0, 0, pl.ds(t0, ch), :].astype(F32)
        return carry

    lax.fori_loop(0, seq // ch // 4, _unrolled(body, 4), 0)


def _conv_silu(pad_ref, c, w):
    ch = SCAN_CHUNK
    base = c * ch + (CONV_HALO - CONV_K // 2)
    acc = pad_ref[pl.ds(base, ch), :] * w[0:1]
    for j in range(1, CONV_K):
        acc = acc + pad_ref[pl.ds(base + j, ch), :] * w[j:j + 1]
    return _silu(acc)


def _gate_cols(rows):
    ch = SCAN_CHUNK
    return jnp.concatenate([rows, jnp.zeros((ch - GATE_ROWS, ch), F32)], axis=0).T


def _gate_rows_spec(seq):
    return pl.BlockSpec((1, GATE_ROWS, 1, 1, seq), lambda bi, hi: (bi, 0, hi, 0, 0))


def _unrolled(body, factor):
    def wrapped(i, carry):
        for u in range(factor):
            carry = body(i * factor + u, carry)
        return carry
    return wrapped


def _level_masks():
    ch = SCAN_CHUNK
    i = np.arange(ch)[:, None]
    j = np.arange(ch)[None, :]
    n_levels = ch.bit_length() - 1
    out = np.zeros((2, n_levels + 2, ch, ch), np.float32)
    for p in range(n_levels):
        pair = ((i ^ j) >> p) == 1
        out[0, p] = pair & (((i >> p) & 1) == 1)
        out[1, p] = pair & (((j >> p) & 1) == 1)
    out[0, n_levels] = i >= j
    out[1, n_levels] = i <= j
    out[0, n_levels + 1] = out[0, 0] + out[1, 0]
    return jnp.asarray(out, dtype=BF16)


def _gdn_kernel(q_ref, k_ref, v_ref, z_ref, row_ref, cwq_ref, cwk_ref, cwv_ref, nw_ref, lm_ref, out_ref,
                mq_scr, n_scr, o_scr, pad_scr, *, seq):
    ch = SCAN_CHUNK
    nc = seq // ch
    ii = lax.broadcasted_iota(jnp.int32, (ch, ch), 0)
    jj = lax.broadcasted_iota(jnp.int32, (ch, ch), 1)
    eye = (ii == jj).astype(F32)
    incl = (ii >= jj, ii <= jj)
    n_levels = ch.bit_length() - 1

    def prep_group(i, carry):
        chunks = []
        for u in range(GDN_PREP_UNROLL):
            c = i * GDN_PREP_UNROLL + u
            q0 = _conv_silu(pad_scr.at[0], c, cwq_ref[...])
            k0 = _conv_silu(pad_scr.at[1], c, cwk_ref[...])
            v = _conv_silu(pad_scr.at[2], c, cwv_ref[...])
            q = q0 * lax.rsqrt(jnp.sum(q0 * q0, axis=-1, keepdims=True) + L2_EPS) * (HEAD_DIM ** -0.5)
            k = k0 * lax.rsqrt(jnp.sum(k0 * k0, axis=-1, keepdims=True) + L2_EPS)
            kt = k.T
            chunks.append(dict(c=c, sl=pl.ds(pl.multiple_of(c * ch, ch), ch), q=q, k=k, v=v, kt=kt))
        for cd in chunks:
            ktb = cd["kt"].astype(BF16)
            cd["kk"] = _dot(cd["k"].astype(BF16), ktb)
            cd["qk"] = _dot(cd["q"].astype(BF16), ktb)
        chains = []
        for cd in chunks:
            rows = row_ref[0, :, 0, 0, cd["sl"]]
            cols = _gate_cols(rows)
            weight = None
            for d in (0, 1):
                g_col = cols[:, 3 * d:3 * d + 1]
                beta = cols[:, 6 + 2 * d:7 + 2 * d]
                decay = jnp.exp(jnp.where(incl[d], g_col - rows[3 * d:3 * d + 1], NEG_BIG))
                weight = decay * beta if weight is None else weight + decay * beta
                chains.append(dict(cd=cd, d=d, decay=decay, beta=beta,
                                   eg=cols[:, 7 + 2 * d:8 + 2 * d], et_row=rows[3 * d + 1:3 * d + 2]))
            cd["ab"] = (cd["kk"] * weight).astype(BF16)
            cd["t"] = eye - (cd["ab"] * lm_ref[0, n_levels + 1]).astype(F32)
        for p in range(1, n_levels):
            for cd in chunks:
                tb = cd["t"].astype(BF16)
                t_lo, t_up = tb * lm_ref[0, n_levels], tb * lm_ref[1, n_levels]
                l_both = jnp.concatenate([cd["ab"] * lm_ref[0, p], cd["ab"] * lm_ref[1, p]], axis=1)
                x = _dot(l_both, jnp.concatenate([t_lo, t_up], axis=0)).astype(BF16)
                cd["t_both"] = jnp.concatenate([t_lo, t_up], axis=1)
                cd["x_both"] = jnp.concatenate([x * lm_ref[0, p], x * lm_ref[1, p]], axis=0)
            for cd in chunks:
                cd["t"] = cd["t"] - _dot(cd["t_both"], cd["x_both"])
        for cn in chains:
            cd, beta = cn["cd"], cn["beta"]
            rhs = jnp.concatenate([cd["v"] * beta, cd["k"] * (beta * cn["eg"])], axis=1).astype(BF16)
            t_dir = cd["t"].astype(BF16) * lm_ref[cn["d"], n_levels]
            cn["sol"] = _dot(t_dir, rhs).astype(BF16)
        for cn in chains:
            cd = cn["cd"]
            cn["ks"] = _dot((cd["kt"] * cn["et_row"]).astype(BF16), cn["sol"])
            cn["aw"] = _dot((cd["qk"] * cn["decay"]).astype(BF16), cn["sol"])
        for cn in chains:
            cd, d = cn["cd"], cn["d"]
            n_scr[d, cd["sl"], :] = cn["ks"][:, :HEAD_DIM]
            o_scr[d, cd["sl"], :] = cn["aw"][:, :HEAD_DIM]
            mq_scr[d, cd["c"], 0:ch, :] = (-cn["ks"][:, HEAD_DIM:]).astype(BF16)
            mq_scr[d, cd["c"], ch:2 * ch, :] = (cd["q"] * cn["eg"] - cn["aw"][:, HEAD_DIM:]).astype(BF16)
        return carry

    def finish(sl, o):
        y = o * lax.rsqrt(jnp.mean(o * o, axis=-1, keepdims=True) + RMS_EPS) * nw_ref[...]
        out_ref[0, sl, :] = (y * _silu(z_ref[0, 0, sl, :].astype(F32))).astype(out_ref.dtype)

    def scan_body(i, states, *, finalize):
        steps = []
        for d, c in ((0, i), (1, nc - 1 - i)):
            steps.append((d, c, pl.ds(pl.multiple_of(c * ch, ch), ch), states[d], states[d].astype(BF16)))
        new_states = []
        for d, c, sl, s, sb in steps:
            chunk_decay = row_ref[0, 3 * d + 2:3 * d + 3, 0, 0, sl]
            new_states.append(s * chunk_decay + _dot(mq_scr[d, c, 0:ch, :], sb) + n_scr[d, sl, :])
        for d, c, sl, s, sb in steps:
            o = o_scr[d, sl, :] + _dot(mq_scr[d, c, ch:2 * ch, :], sb)
            if finalize:
                finish(sl, o + o_scr[1 - d, sl, :])
            else:
                o_scr[d, sl, :] = o
        return tuple(new_states)

    for t, ref in enumerate((q_ref, k_ref, v_ref)):
        _fill_padded(pad_scr.at[t], ref, seq)
    lax.fori_loop(0, nc // GDN_PREP_UNROLL, prep_group, 0)
    zero_state = jnp.zeros((HEAD_DIM, HEAD_DIM), F32)
    half = nc // (2 * GDN_SCAN_UNROLL)
    states = lax.fori_loop(0, half, _unrolled(functools.partial(scan_body, finalize=False), GDN_SCAN_UNROLL),
                           (zero_state, zero_state))
    lax.fori_loop(half, 2 * half, _unrolled(functools.partial(scan_body, finalize=True), GDN_SCAN_UNROLL),
                  states)


def _gdn(p, rows, conv_w, norm_w, *, col0):
    b, _, s, _ = p.shape
    h = N_HEADS
    nc = s // SCAN_CHUNK
    assert nc % GDN_PREP_UNROLL == 0 and nc % (2 * GDN_SCAN_UNROLL) == 0
    level_masks = _level_masks()

    def pblock(off):
        return pl.BlockSpec((1, 1, s, HEAD_DIM), lambda bi, hi: (bi, col0 + off + hi, 0, 0))

    def cblock(off):
        return pl.BlockSpec((CONV_K, HEAD_DIM), lambda bi, hi: (0, off + hi))

    return pl.pallas_call(
        functools.partial(_gdn_kernel, seq=s),
        grid=(b, h),
        in_specs=[
            pblock(0), pblock(h), pblock(2 * h), pblock(3 * h),
            _gate_rows_spec(s),
            cblock(0), cblock(h), cblock(2 * h),
            pl.BlockSpec((1, HEAD_DIM), lambda bi, hi: (0, 0)),
            pl.BlockSpec(level_masks.shape, lambda bi, hi: (0, 0, 0, 0)),
        ],
        out_specs=pl.BlockSpec((1, s, HEAD_DIM), lambda bi, hi: (bi, 0, hi)),
        out_shape=jax.ShapeDtypeStruct((b, s, h * HEAD_DIM), BF16),
        scratch_shapes=[
            pltpu.VMEM((2, nc, 2 * SCAN_CHUNK, HEAD_DIM), BF16),
            pltpu.VMEM((2, s, HEAD_DIM), F32),
            pltpu.VMEM((2, s, HEAD_DIM), F32),
            pltpu.VMEM((3, s + 2 * CONV_HALO, HEAD_DIM), F32),
        ],
        compiler_params=pltpu.CompilerParams(
            dimension_semantics=("arbitrary", "arbitrary"), vmem_limit_bytes=VMEM_LIMIT_BYTES),
        name="gdn",
    )(p, p, p, p, rows, conv_w, conv_w, conv_w, norm_w, level_masks)


def _mlstm_kernel(qk_ref, v_ref, o_ref, z_ref, row_ref, cw_ref, nw_ref, out_ref,
                  qm_scr, kt_scr, qkm_scr, cols_scr, h_scr, pad_scr, *, seq):
    ch = SCAN_CHUNK
    nc = seq // ch
    ii = lax.broadcasted_iota(jnp.int32, (ch, ch), 0)
    jj = lax.broadcasted_iota(jnp.int32, (ch, ch), 1)
    incl = (ii >= jj, ii <= jj)
    ones = jnp.ones((ch, HEAD_DIM), BF16)

    def prep_body(c, carry):
        sl = pl.ds(pl.multiple_of(c * ch, ch), ch)
        t = _conv_silu(pad_scr, c, cw_ref[...])
        qm = jnp.where(jj < MLSTM_DQK, t, 0.0).astype(BF16)
        tt = t.T
        kt = jnp.concatenate([tt[MLSTM_DQK:], jnp.zeros((ch - MLSTM_DQK, ch), F32)], axis=0)
        ktb = (kt * (MLSTM_DQK ** -0.5)).astype(BF16)
        qm_scr[sl, :] = qm
        kt_scr[:, sl] = ktb
        qkm_scr[sl, :] = _dot(qm, ktb)
        cols_scr[sl, :] = _gate_cols(row_ref[0, :, 0, 0, sl])
        return carry

    def scan_group(i, states):
        items = []
        for u in range(MLSTM_UNROLL):
            step = i * MLSTM_UNROLL + u
            for d, c in ((0, step), (1, nc - 1 - step)):
                sl = pl.ds(pl.multiple_of(c * ch, ch), ch)
                items.append(dict(d=d, sl=sl, rows=row_ref[0, :, 0, 0, sl]))
        for it in items:
            d, sl, rows, cols = it["d"], it["sl"], it["rows"], cols_scr[it["sl"], :]
            r_col = cols[:, 6 + 3 * d:7 + 3 * d]
            it["inter"] = cols[:, 7 + 3 * d:8 + 3 * d]
            it["enm"] = cols[:, 8 + 3 * d:9 + 3 * d]
            dec_row = rows[3 * d + 2:3 * d + 3]
            it["dec"] = jnp.concatenate([dec_row, dec_row], axis=1)
            it["vaug"] = jnp.concatenate([v_ref[0, 0, sl, :], ones], axis=1)
            wt = jnp.exp(jnp.where(incl[d], rows[3 * d:3 * d + 1] - r_col, NEG_BIG)) * qkm_scr[sl, :]
            it["wt"] = wt.astype(BF16)
            awk = jnp.broadcast_to(rows[3 * d + 1:3 * d + 2], (ch, ch)).astype(BF16)
            it["wk"] = kt_scr[:, sl] * awk
        for it in items:
            it["intra"] = _dot(it["wt"], it["vaug"])
            it["update"] = _dot(it["wk"], it["vaug"])
        states = list(states)
        for it in items:
            d = it["d"]
            it["qs"] = _dot(qm_scr[it["sl"], :], states[d].astype(BF16))
            states[d] = states[d] * it["dec"] + it["update"]
        for it in items:
            nd = it["inter"] * it["qs"] + it["intra"]
            num = nd[:, :HEAD_DIM]
            den = nd[:, HEAD_DIM:]
            h_scr[it["d"], it["sl"], :] = num / jnp.maximum(jnp.abs(den), it["enm"])
        return tuple(states)

    def final_body(c, carry):
        sl = pl.ds(pl.multiple_of(c * ch, ch), ch)
        hh = h_scr[0, sl, :] + h_scr[1, sl, :]
        y = hh * lax.rsqrt(jnp.mean(hh * hh, axis=-1, keepdims=True) + RMS_EPS) * nw_ref[...]
        y = y * _sigmoid(o_ref[0, 0, sl, :].astype(F32)) * _silu(z_ref[0, 0, sl, :].astype(F32))
        out_ref[0, sl, :] = y.astype(out_ref.dtype)
        return carry

    n_iter = nc // MLSTM_UNROLL
    _fill_padded(pad_scr, qk_ref, seq)
    lax.fori_loop(0, n_iter, _unrolled(prep_body, MLSTM_UNROLL), 0)
    zero_state = jnp.zeros((ch, 2 * HEAD_DIM), F32)
    lax.fori_loop(0, n_iter, scan_group, (zero_state, zero_state))
    lax.fori_loop(0, n_iter, _unrolled(final_body, MLSTM_UNROLL), 0)


def _mlstm(p, rows, conv_w, norm_w, *, col0):
    b, _, s, _ = p.shape
    h = N_HEADS
    assert (s // SCAN_CHUNK) % MLSTM_UNROLL == 0

    def pblock(off):
        return pl.BlockSpec((1, 1, s, HEAD_DIM), lambda bi, hi: (bi, col0 + off + hi, 0, 0))

    return pl.pallas_call(
        functools.partial(_mlstm_kernel, seq=s),
        grid=(b, h),
        in_specs=[
            pblock(0), pblock(h), pblock(2 * h), pblock(3 * h),
            _gate_rows_spec(s),
            pl.BlockSpec((CONV_K, HEAD_DIM), lambda bi, hi: (0, hi)),
            pl.BlockSpec((1, HEAD_DIM), lambda bi, hi: (0, hi)),
        ],
        out_specs=pl.BlockSpec((1, s, HEAD_DIM), lambda bi, hi: (bi, 0, hi)),
        out_shape=jax.ShapeDtypeStruct((b, s, h * HEAD_DIM), BF16),
        scratch_shapes=[
            pltpu.VMEM((s, HEAD_DIM), BF16),
            pltpu.VMEM((HEAD_DIM, s), BF16),
            pltpu.VMEM((s, SCAN_CHUNK), F32),
            pltpu.VMEM((s, SCAN_CHUNK), F32),
            pltpu.VMEM((2, s, HEAD_DIM), F32),
            pltpu.VMEM((s + 2 * CONV_HALO, HEAD_DIM), F32),
        ],
        compiler_params=pltpu.CompilerParams(
            dimension_semantics=("arbitrary", "arbitrary"), vmem_limit_bytes=VMEM_LIMIT_BYTES),
        name="mlstm",
    )(p, p, p, p, rows, conv_w, norm_w)


def _outproj_kernel(g_ref, m_ref, wo_ref, x_ref, npw_ref, y_ref, *, width):
    mixed = _dot(g_ref[0], wo_ref[:width]) + _dot(m_ref[0], wo_ref[width:])
    ms = jnp.mean(mixed * mixed, axis=-1, keepdims=True)
    y_ref[0] = x_ref[0] + mixed * lax.rsqrt(ms + RMS_EPS) * npw_ref[...]


def _out_proj(g, m, w_out, x, npw, *, tm=1024):
    b, s, d = x.shape
    width = g.shape[-1]
    return pl.pallas_call(
        functools.partial(_outproj_kernel, width=width),
        grid=(b, s // tm),
        in_specs=[
            pl.BlockSpec((1, tm, width), lambda bi, si: (bi, si, 0)),
            pl.BlockSpec((1, tm, width), lambda bi, si: (bi, si, 0)),
            pl.BlockSpec(w_out.shape, lambda bi, si: (0, 0)),
            pl.BlockSpec((1, tm, d), lambda bi, si: (bi, si, 0)),
            pl.BlockSpec((1, d), lambda bi, si: (0, 0)),
        ],
        out_specs=pl.BlockSpec((1, tm, d), lambda bi, si: (bi, si, 0)),
        out_shape=jax.ShapeDtypeStruct((b, s, d), F32),
        compiler_params=pltpu.CompilerParams(
            dimension_semantics=("arbitrary", "arbitrary"), vmem_limit_bytes=VMEM_LIMIT_BYTES),
        name="out_proj",
    )(g, m, w_out, x, npw)


def _layer(x, norm_pre_w, w_in, gdn_conv_w, gdn_a_log, gdn_dt_bias, gdn_norm_w,
           mlstm_conv_w, mlstm_gate_bias, mlstm_norm_w, w_out, norm_post_w):
    d = x.shape[-1]
    h = N_HEADS
    gw = h * HEAD_DIM
    mqk = h * MLSTM_DQK
    w_main_t, w_gate_t = _weight_layout(jnp.swapaxes(w_in, 1, 2))

    p, gt = _in_proj(x, norm_pre_w.reshape(1, d), w_main_t, w_gate_t)
    g_rows, m_rows = _gates(gt, gdn_a_log.reshape(2 * h, 1), gdn_dt_bias.reshape(2 * h, 1),
                            mlstm_gate_bias.reshape(4 * h, 1))
    g_rows = g_rows[:, :, :, None, :]
    m_rows = m_rows[:, :, :, None, :]

    cq = mlstm_conv_w[:, :mqk].reshape(CONV_K, h, MLSTM_DQK)
    ck = mlstm_conv_w[:, mqk:].reshape(CONV_K, h, MLSTM_DQK)
    m_conv = jnp.concatenate([cq, ck], axis=2).reshape(CONV_K, gw)

    g_out = _gdn(p, g_rows, gdn_conv_w, gdn_norm_w.reshape(1, HEAD_DIM), col0=0)
    m_out = _mlstm(p, m_rows, m_conv, mlstm_norm_w.reshape(1, gw), col0=4 * h)
    return _out_proj(g_out, m_out, w_out.astype(BF16), x, norm_post_w.reshape(1, d))


def kernel(x, norm_pre_w, w_in, gdn_conv_w, gdn_a_log, gdn_dt_bias, gdn_norm_w,
           mlstm_conv_w, mlstm_gate_bias, mlstm_norm_w, w_out, norm_post_w):
    for layer in range(norm_pre_w.shape[0]):
        x = _layer(x, norm_pre_w[layer], w_in[layer:layer + 1], gdn_conv_w[layer], gdn_a_log[layer],
                   gdn_dt_bias[layer], gdn_norm_w[layer], mlstm_conv_w[layer],
                   mlstm_gate_bias[layer], mlstm_norm_w[layer], w_out[layer], norm_post_w[layer])
    return x
```

```python
import functools

import jax
import jax.numpy as jnp
import numpy as np
from jax import lax
from jax.experimental import pallas as pl
from jax.experimental.pallas import tpu as pltpu

F32 = jnp.float32
BF16 = jnp.bfloat16

N_HEADS = 8
HEAD_DIM = 128
MLSTM_DQK = 64
CONV_K = 5
CONV_HALO = 8
SCAN_CHUNK = 128
GATE_ROWS = 16
WT_ROWS = 32
WT_PIECES = 16
GDN_PREP_UNROLL = 16
GDN_SCAN_UNROLL = 16
MLSTM_UNROLL = 16
GATES_UNROLL = 16
RMS_EPS = 1e-6
L2_EPS = 1e-6
NEG_BIG = -1e30
VMEM_LIMIT_BYTES = 56 * 1024 * 1024


def _dot(a, b):
    return jnp.dot(a, b, preferred_element_type=F32)


def _dot_exact(a, b):
    return jnp.dot(a, b, preferred_element_type=F32, precision=lax.Precision.HIGHEST)


def _softplus(x):
    return jnp.maximum(x, 0.0) + jnp.log1p(jnp.exp(-jnp.abs(x)))


def _sigmoid(x):
    return 0.5 * jnp.tanh(0.5 * x) + 0.5


def _silu(x):
    h = 0.5 * x
    return h * jnp.tanh(h) + h


def _weight_layout_kernel(*refs):
    parts, gate_parts, (main_ref, gate_ref) = refs[:WT_PIECES], refs[WT_PIECES:-2], refs[-2:]
    for i, part in enumerate(parts):
        main_ref[WT_ROWS * i:WT_ROWS * (i + 1), :] = part[0].astype(BF16)
    for i, part in enumerate(gate_parts):
        gate_ref[WT_ROWS * i:WT_ROWS * (i + 1), :] = part[0].astype(BF16)


def _weight_layout(w_t):
    _, width, d = w_t.shape
    gw = N_HEADS * HEAD_DIM
    mqk = N_HEADS * MLSTM_DQK
    n_gate = width - 8 * gw
    m0 = 4 * gw + n_gate // 2
    assert n_gate == 2 * WT_ROWS and m0 % WT_ROWS == 0 and MLSTM_DQK == 2 * WT_ROWS
    per_head = HEAD_DIM // WT_ROWS
    assert WT_PIECES % per_head == 0 and (8 * gw) % (WT_PIECES * WT_ROWS) == 0
    q0, k0, rest0 = m0 // WT_ROWS, (m0 + mqk) // WT_ROWS, (m0 + 2 * mqk) // WT_ROWS
    qk_begin, qk_end = 4 * gw // WT_ROWS, 5 * gw // WT_ROWS

    def piece(part):
        sub = part % per_head
        def index(r):
            g = r * WT_PIECES + part
            hd = (g - qk_begin) // per_head
            qk = (q0 if sub < per_head // 2 else k0 - per_head // 2) + (per_head // 2) * hd + sub
            return 0, jnp.where(g < qk_begin, g, jnp.where(g < qk_end, qk, rest0 + g - qk_end)), 0
        return pl.BlockSpec((1, WT_ROWS, d), index)

    def gate_piece(row0):
        return pl.BlockSpec((1, WT_ROWS, d), lambda r: (0, row0 // WT_ROWS, 0))

    return pl.pallas_call(
        _weight_layout_kernel,
        grid=(8 * gw // (WT_PIECES * WT_ROWS),),
        in_specs=[piece(p) for p in range(WT_PIECES)] + [gate_piece(4 * gw), gate_piece(width - WT_ROWS)],
        out_specs=[pl.BlockSpec((WT_PIECES * WT_ROWS, d), lambda r: (r, 0)),
                   pl.BlockSpec((n_gate, d), lambda r: (0, 0))],
        out_shape=[jax.ShapeDtypeStruct((8 * gw, d), BF16),
                   jax.ShapeDtypeStruct((n_gate, d), BF16)],
        compiler_params=pltpu.CompilerParams(
            dimension_semantics=("arbitrary",), vmem_limit_bytes=VMEM_LIMIT_BYTES),
        name="weight_layout",
    )(*([w_t] * (WT_PIECES + 2)))


_CONTRACT_MINOR = (((1,), (1,)), ((), ()))


def _inproj_kernel(x_ref, npw_ref, w_ref, wg_ref, p_ref, gt_ref, h_scr, *, n_sub):
    j = pl.program_id(2)

    @pl.when(j == 0)
    def _():
        x = x_ref[0]
        ms = jnp.mean(x * x, axis=-1, keepdims=True)
        h = (x * lax.rsqrt(ms + RMS_EPS) * npw_ref[...]).astype(BF16)
        h_scr[...] = h
        gt_ref[0] = lax.dot_general(wg_ref[...], h, _CONTRACT_MINOR, preferred_element_type=F32)

    h = h_scr[...]
    for c in range(n_sub // 2):
        acc = lax.dot_general(h, w_ref[c * 256:(c + 1) * 256, :], _CONTRACT_MINOR,
                              preferred_element_type=F32)
        p_ref[0, 2 * c] = acc[:, :128].astype(p_ref.dtype)
        p_ref[0, 2 * c + 1] = acc[:, 128:].astype(p_ref.dtype)


def _in_proj(x, npw, w_main_t, w_gate_t, *, tm=2048, tn=2048):
    b, s, d = x.shape
    n_blocks = w_main_t.shape[0] // 128
    n_gate = w_gate_t.shape[0]
    n_sub = tn // 128
    return pl.pallas_call(
        functools.partial(_inproj_kernel, n_sub=n_sub),
        grid=(b, s // tm, w_main_t.shape[0] // tn),
        in_specs=[
            pl.BlockSpec((1, tm, d), lambda bi, si, j: (bi, si, 0)),
            pl.BlockSpec((1, d), lambda bi, si, j: (0, 0)),
            pl.BlockSpec((tn, d), lambda bi, si, j: (j, 0)),
            pl.BlockSpec(w_gate_t.shape, lambda bi, si, j: (0, 0)),
        ],
        out_specs=[
            pl.BlockSpec((1, n_sub, tm, 128), lambda bi, si, j: (bi, j, si, 0)),
            pl.BlockSpec((1, n_gate, tm), lambda bi, si, j: (bi, 0, si)),
        ],
        out_shape=[
            jax.ShapeDtypeStruct((b, n_blocks, s, 128), BF16),
            jax.ShapeDtypeStruct((b, n_gate, s), F32),
        ],
        scratch_shapes=[pltpu.VMEM((tm, d), BF16)],
        compiler_params=pltpu.CompilerParams(
            dimension_semantics=("arbitrary", "arbitrary", "arbitrary"),
            vmem_limit_bytes=VMEM_LIMIT_BYTES),
        name="in_proj",
    )(x, npw, w_main_t, w_gate_t)


def _gates_kernel(gt_ref, alog_ref, dtb_ref, gb_ref, gout_ref, mout_ref, *, seq):
    nc = seq // SCAN_CHUNK
    ch = SCAN_CHUNK
    nh = N_HEADS
    ii = lax.broadcasted_iota(jnp.int32, (ch, ch), 0)
    jj = lax.broadcasted_iota(jnp.int32, (ch, ch), 1)
    upper = (ii <= jj).astype(F32)
    lower = (ii >= jj).astype(F32)
    lane = lax.broadcasted_iota(jnp.int32, (nh, ch), 1)
    neg_a = -jnp.exp(alog_ref[...])
    dtb = dtb_ref[...]
    gb = gb_ref[...]
    zeros8 = jnp.zeros((nh, ch), F32)

    def bcast(col):
        return jnp.broadcast_to(col, (nh, ch))

    def local_body(i, carry):
        tiles = []
        for u in range(GATES_UNROLL):
            sl = pl.ds(pl.multiple_of((i * GATES_UNROLL + u) * ch, ch), ch)
            g = neg_a * _softplus(gt_ref[0, 0:2 * nh, sl] + dtb)
            lf = -_softplus(-(gt_ref[0, 6 * nh:8 * nh, sl] + gb[2 * nh:4 * nh]))
            tiles.append(dict(sl=sl, stacked=jnp.concatenate([g, lf], axis=0),
                              beta=_sigmoid(gt_ref[0, 2 * nh:4 * nh, sl]),
                              ib=gt_ref[0, 4 * nh:6 * nh, sl] + gb[0:2 * nh]))
        for t in tiles:
            t["cum"] = (_dot_exact(t["stacked"], upper), _dot_exact(t["stacked"], lower))
        for t in tiles:
            t["f_local"] = [t["cum"][d][2 * nh + nh * d:3 * nh + nh * d] for d in (0, 1)]
            t["a_local"] = [t["ib"][nh * d:nh * d + nh] - t["f_local"][d] for d in (0, 1)]
            t["a_max"] = list(t["a_local"])
        for sh in (1, 2, 4, 8, 16, 32, 64):
            for t in tiles:
                fwd, bwd = t["a_max"]
                t["a_max"] = [
                    jnp.maximum(fwd, jnp.where(lane >= sh, pltpu.roll(fwd, sh, 1), NEG_BIG)),
                    jnp.maximum(bwd, jnp.where(lane < ch - sh, pltpu.roll(bwd, ch - sh, 1), NEG_BIG)),
                ]
        for t in tiles:
            sl = t["sl"]
            for d in (0, 1):
                rows = slice(nh * d, nh * d + nh)
                gcum = t["cum"][d][rows]
                gend = bcast(t["cum"][0][rows, ch - 1:ch])
                gout_ref[0, 3 * d + 0, :, sl] = gcum
                gout_ref[0, 3 * d + 1, :, sl] = jnp.exp(gend - gcum)
                gout_ref[0, 3 * d + 2, :, sl] = jnp.exp(gend)
                gout_ref[0, 6 + 2 * d, :, sl] = t["beta"][rows]
                gout_ref[0, 7 + 2 * d, :, sl] = jnp.exp(gcum)
                end = ch - 1 if d == 0 else 0
                mout_ref[0, 3 * d + 0, :, sl] = t["a_local"][d]
                mout_ref[0, 6 + 3 * d, :, sl] = t["a_max"][d]
                mout_ref[0, 8 + 3 * d, :, sl] = t["f_local"][d]
                mout_ref[0, 12 + 2 * d, :, sl] = bcast(t["a_max"][d][:, end:end + 1])
                mout_ref[0, 13 + 2 * d, :, sl] = bcast(t["f_local"][d][:, end:end + 1])
            for q in range(10, GATE_ROWS):
                gout_ref[0, q, :, sl] = zeros8
        return carry

    def carry_part(c, carry, d):
        c_f, c_r = carry
        sl = pl.ds(pl.multiple_of(c * ch, ch), ch)
        a = mout_ref[0, 3 * d + 0, :, sl] - c_f
        r = jnp.maximum(mout_ref[0, 6 + 3 * d, :, sl] - c_f, c_r)
        fcum = mout_ref[0, 8 + 3 * d, :, sl] + c_f
        r_end = jnp.maximum(mout_ref[0, 12 + 2 * d, :, sl] - c_f, c_r)
        f_end = mout_ref[0, 13 + 2 * d, :, sl] + c_f
        mout_ref[0, 3 * d + 0, :, sl] = a
        mout_ref[0, 3 * d + 1, :, sl] = jnp.exp(a - r_end)
        mout_ref[0, 3 * d + 2, :, sl] = jnp.exp(c_r - r_end)
        mout_ref[0, 6 + 3 * d, :, sl] = r
        mout_ref[0, 7 + 3 * d, :, sl] = jnp.exp(c_r - r)
        mout_ref[0, 8 + 3 * d, :, sl] = jnp.exp(-(fcum + r))
        return f_end, r_end

    def carry_body(i, carry):
        return carry_part(i, carry[0], 0), carry_part(nc - 1 - i, carry[1], 1)

    lax.fori_loop(0, nc // GATES_UNROLL, local_body, 0)
    lax.fori_loop(0, nc, carry_body, ((zeros8, zeros8), (zeros8, zeros8)))


def _gates(gt, alog, dtb, gb):
    b, n_rows, s = gt.shape
    assert (s // SCAN_CHUNK) % GATES_UNROLL == 0
    out_block = pl.BlockSpec((1, GATE_ROWS, N_HEADS, s), lambda bi: (bi, 0, 0, 0))
    out_shape = jax.ShapeDtypeStruct((b, GATE_ROWS, N_HEADS, s), F32)
    return pl.pallas_call(
        functools.partial(_gates_kernel, seq=s),
        grid=(b,),
        in_specs=[
            pl.BlockSpec((1, n_rows, s), lambda bi: (bi, 0, 0)),
            pl.BlockSpec(alog.shape, lambda bi: (0, 0)),
            pl.BlockSpec(dtb.shape, lambda bi: (0, 0)),
            pl.BlockSpec(gb.shape, lambda bi: (0, 0)),
        ],
        out_specs=[out_block, out_block],
        out_shape=[out_shape, out_shape],
        compiler_params=pltpu.CompilerParams(
            dimension_semantics=("arbitrary",), vmem_limit_bytes=VMEM_LIMIT_BYTES),
        name="gates",
    )(gt, alog, dtb, gb)


def _fill_padded(pad_ref, ref, seq):
    ch = SCAN_CHUNK
    zeros = jnp.zeros((CONV_HALO, pad_ref.shape[-1]), F32)
    pad_ref[0:CONV_HALO, :] = zeros
    pad_ref[seq + CONV_HALO:seq + 2 * CONV_HALO, :] = zeros

    def body(c, carry):
        t0 = pl.multiple_of(c * ch, ch)
        pad_ref[pl.ds(t0 + CONV_HALO, ch), :] = ref[0, 0, pl.ds(t0, ch), :].astype(F32)
        return carry

    lax.fori_loop(0, seq // ch // 4, _unrolled(body, 4), 0)


def _conv_silu(pad_ref, c, w):
    ch = SCAN_CHUNK
    base = c * ch + (CONV_HALO - CONV_K // 2)
    acc = pad_ref[pl.ds(base, ch), :] * w[0:1]
    for j in range(1, CONV_K):
        acc = acc + pad_ref[pl.ds(base + j, ch), :] * w[j:j + 1]
    return _silu(acc)


def _gate_cols(rows):
    ch = SCAN_CHUNK
    return jnp.concatenate([rows, jnp.zeros((ch - GATE_ROWS, ch), F32)], axis=0).T


def _gate_rows_spec(seq):
    return pl.BlockSpec((1, GATE_ROWS, 1, 1, seq), lambda bi, hi: (bi, 0, hi, 0, 0))


def _unrolled(body, factor):
    def wrapped(i, carry):
        for u in range(factor):
            carry = body(i * factor + u, carry)
        return carry
    return wrapped


def _level_masks():
    ch = SCAN_CHUNK
    i = np.arange(ch)[:, None]
    j = np.arange(ch)[None, :]
    n_levels = ch.bit_length() - 1
    out = np.zeros((2, n_levels + 2, ch, ch), np.float32)
    for p in range(n_levels):
        pair = ((i ^ j) >> p) == 1
        out[0, p] = pair & (((i >> p) & 1) == 1)
        out[1, p] = pair & (((j >> p) & 1) == 1)
    out[0, n_levels] = i >= j
    out[1, n_levels] = i <= j
    out[0, n_levels + 1] = out[0, 0] + out[1, 0]
    return jnp.asarray(out, dtype=BF16)


def _gdn_kernel(q_ref, k_ref, v_ref, z_ref, row_ref, cwq_ref, cwk_ref, cwv_ref, nw_ref, lm_ref, out_ref,
                mq_scr, n_scr, o_scr, pad_scr, *, seq):
    ch = SCAN_CHUNK
    nc = seq // ch
    ii = lax.broadcasted_iota(jnp.int32, (ch, ch), 0)
    jj = lax.broadcasted_iota(jnp.int32, (ch, ch), 1)
    eye = (ii == jj).astype(F32)
    incl = (ii >= jj, ii <= jj)
    n_levels = ch.bit_length() - 1

    def prep_group(i, carry):
        chunks = []
        for u in range(GDN_PREP_UNROLL):
            c = i * GDN_PREP_UNROLL + u
            q0 = _conv_silu(pad_scr.at[0], c, cwq_ref[...])
            k0 = _conv_silu(pad_scr.at[1], c, cwk_ref[...])
            v = _conv_silu(pad_scr.at[2], c, cwv_ref[...])
            q = q0 * lax.rsqrt(jnp.sum(q0 * q0, axis=-1, keepdims=True) + L2_EPS) * (HEAD_DIM ** -0.5)
            k = k0 * lax.rsqrt(jnp.sum(k0 * k0, axis=-1, keepdims=True) + L2_EPS)
            kt = k.T
            chunks.append(dict(c=c, sl=pl.ds(pl.multiple_of(c * ch, ch), ch), q=q, k=k, v=v, kt=kt))
        for cd in chunks:
            ktb = cd["kt"].astype(BF16)
            cd["kk"] = _dot(cd["k"].astype(BF16), ktb)
            cd["qk"] = _dot(cd["q"].astype(BF16), ktb)
        chains = []
        for cd in chunks:
            rows = row_ref[0, :, 0, 0, cd["sl"]]
            cols = _gate_cols(rows)
            weight = None
            for d in (0, 1):
                g_col = cols[:, 3 * d:3 * d + 1]
                beta = cols[:, 6 + 2 * d:7 + 2 * d]
                decay = jnp.exp(jnp.where(incl[d], g_col - rows[3 * d:3 * d + 1], NEG_BIG))
                weight = decay * beta if weight is None else weight + decay * beta
                chains.append(dict(cd=cd, d=d, decay=decay, beta=beta,
                                   eg=cols[:, 7 + 2 * d:8 + 2 * d], et_row=rows[3 * d + 1:3 * d + 2]))
            cd["ab"] = (cd["kk"] * weight).astype(BF16)
            cd["t"] = eye - (cd["ab"] * lm_ref[0, n_levels + 1]).astype(F32)
        for p in range(1, n_levels):
            for cd in chunks:
                tb = cd["t"].astype(BF16)
                t_lo, t_up = tb * lm_ref[0, n_levels], tb * lm_ref[1, n_levels]
                l_both = jnp.concatenate([cd["ab"] * lm_ref[0, p], cd["ab"] * lm_ref[1, p]], axis=1)
                x = _dot(l_both, jnp.concatenate([t_lo, t_up], axis=0)).astype(BF16)
                cd["t_both"] = jnp.concatenate([t_lo, t_up], axis=1)
                cd["x_both"] = jnp.concatenate([x * lm_ref[0, p], x * lm_ref[1, p]], axis=0)
            for cd in chunks:
                cd["t"] = cd["t"] - _dot(cd["t_both"], cd["x_both"])
        for cn in chains:
            cd, beta = cn["cd"], cn["beta"]
            rhs = jnp.concatenate([cd["v"] * beta, cd["k"] * (beta * cn["eg"])], axis=1).astype(BF16)
            t_dir = cd["t"].astype(BF16) * lm_ref[cn["d"], n_levels]
            cn["sol"] = _dot(t_dir, rhs).astype(BF16)
        for cn in chains:
            cd = cn["cd"]
            cn["ks"] = _dot((cd["kt"] * cn["et_row"]).astype(BF16), cn["sol"])
            cn["aw"] = _dot((cd["qk"] * cn["decay"]).astype(BF16), cn["sol"])
        for cn in chains:
            cd, d = cn["cd"], cn["d"]
            n_scr[d, cd["sl"], :] = cn["ks"][:, :HEAD_DIM]
            o_scr[d, cd["sl"], :] = cn["aw"][:, :HEAD_DIM]
            mq_scr[d, cd["c"], 0:ch, :] = (-cn["ks"][:, HEAD_DIM:]).astype(BF16)
            mq_scr[d, cd["c"], ch:2 * ch, :] = (cd["q"] * cn["eg"] - cn["aw"][:, HEAD_DIM:]).astype(BF16)
        return carry

    def finish(sl, o):
        y = o * lax.rsqrt(jnp.mean(o * o, axis=-1, keepdims=True) + RMS_EPS) * nw_ref[...]
        out_ref[0, sl, :] = (y * _silu(z_ref[0, 0, sl, :].astype(F32))).astype(out_ref.dtype)

    def scan_body(i, states, *, finalize):
        steps = []
        for d, c in ((0, i), (1, nc - 1 - i)):
            steps.append((d, c, pl.ds(pl.multiple_of(c * ch, ch), ch), states[d], states[d].astype(BF16)))
        new_states = []
        for d, c, sl, s, sb in steps:
            chunk_decay = row_ref[0, 3 * d + 2:3 * d + 3, 0, 0, sl]
            new_states.append(s * chunk_decay + _dot(mq_scr[d, c, 0:ch, :], sb) + n_scr[d, sl, :])
        for d, c, sl, s, sb in steps:
            o = o_scr[d, sl, :] + _dot(mq_scr[d, c, ch:2 * ch, :], sb)
            if finalize:
                finish(sl, o + o_scr[1 - d, sl, :])
            else:
                o_scr[d, sl, :] = o
        return tuple(new_states)

    for t, ref in enumerate((q_ref, k_ref, v_ref)):
        _fill_padded(pad_scr.at[t], ref, seq)
    lax.fori_loop(0, nc // GDN_PREP_UNROLL, prep_group, 0)
    zero_state = jnp.zeros((HEAD_DIM, HEAD_DIM), F32)
    half = nc // (2 * GDN_SCAN_UNROLL)
    states = lax.fori_loop(0, half, _unrolled(functools.partial(scan_body, finalize=False), GDN_SCAN_UNROLL),
                           (zero_state, zero_state))
    lax.fori_loop(half, 2 * half, _unrolled(functools.partial(scan_body, finalize=True), GDN_SCAN_UNROLL),
                  states)


def _gdn(p, rows, conv_w, norm_w, *, col0):
    b, _, s, _ = p.shape
    h = N_HEADS
    nc = s // SCAN_CHUNK
    assert nc % GDN_PREP_UNROLL == 0 and nc % (2 * GDN_SCAN_UNROLL) == 0
    level_masks = _level_masks()

    def pblock(off):
        return pl.BlockSpec((1, 1, s, HEAD_DIM), lambda bi, hi: (bi, col0 + off + hi, 0, 0))

    def cblock(off):
        return pl.BlockSpec((CONV_K, HEAD_DIM), lambda bi, hi: (0, off + hi))

    return pl.pallas_call(
        functools.partial(_gdn_kernel, seq=s),
        grid=(b, h),
        in_specs=[
            pblock(0), pblock(h), pblock(2 * h), pblock(3 * h),
            _gate_rows_spec(s),
            cblock(0), cblock(h), cblock(2 * h),
            pl.BlockSpec((1, HEAD_DIM), lambda bi, hi: (0, 0)),
            pl.BlockSpec(level_masks.shape, lambda bi, hi: (0, 0, 0, 0)),
        ],
        out_specs=pl.BlockSpec((1, s, HEAD_DIM), lambda bi, hi: (bi, 0, hi)),
        out_shape=jax.ShapeDtypeStruct((b, s, h * HEAD_DIM), BF16),
        scratch_shapes=[
            pltpu.VMEM((2, nc, 2 * SCAN_CHUNK, HEAD_DIM), BF16),
            pltpu.VMEM((2, s, HEAD_DIM), F32),
            pltpu.VMEM((2, s, HEAD_DIM), F32),
            pltpu.VMEM((3, s + 2 * CONV_HALO, HEAD_DIM), F32),
        ],
        compiler_params=pltpu.CompilerParams(
            dimension_semantics=("arbitrary", "arbitrary"), vmem_limit_bytes=VMEM_LIMIT_BYTES),
        name="gdn",
    )(p, p, p, p, rows, conv_w, conv_w, conv_w, norm_w, level_masks)


def _mlstm_kernel(qk_ref, v_ref, o_ref, z_ref, row_ref, cw_ref, nw_ref, out_ref,
                  qm_scr, kt_scr, qkm_scr, cols_scr, h_scr, pad_scr, *, seq):
    ch = SCAN_CHUNK
    nc = seq // ch
    ii = lax.broadcasted_iota(jnp.int32, (ch, ch), 0)
    jj = lax.broadcasted_iota(jnp.int32, (ch, ch), 1)
    incl = (ii >= jj, ii <= jj)
    ones = jnp.ones((ch, HEAD_DIM), BF16)

    def prep_body(c, carry):
        sl = pl.ds(pl.multiple_of(c * ch, ch), ch)
        t = _conv_silu(pad_scr, c, cw_ref[...])
        qm = jnp.where(jj < MLSTM_DQK, t, 0.0).astype(BF16)
        tt = t.T
        kt = jnp.concatenate([tt[MLSTM_DQK:], jnp.zeros((ch - MLSTM_DQK, ch), F32)], axis=0)
        ktb = (kt * (MLSTM_DQK ** -0.5)).astype(BF16)
        qm_scr[sl, :] = qm
        kt_scr[:, sl] = ktb
        qkm_scr[sl, :] = _dot(qm, ktb)
        cols_scr[sl, :] = _gate_cols(row_ref[0, :, 0, 0, sl])
        return carry

    def scan_group(i, states):
        items = []
        for u in range(MLSTM_UNROLL):
            step = i * MLSTM_UNROLL + u
            for d, c in ((0, step), (1, nc - 1 - step)):
                sl = pl.ds(pl.multiple_of(c * ch, ch), ch)
                items.append(dict(d=d, sl=sl, rows=row_ref[0, :, 0, 0, sl]))
        for it in items:
            d, sl, rows, cols = it["d"], it["sl"], it["rows"], cols_scr[it["sl"], :]
            r_col = cols[:, 6 + 3 * d:7 + 3 * d]
            it["inter"] = cols[:, 7 + 3 * d:8 + 3 * d]
            it["enm"] = cols[:, 8 + 3 * d:9 + 3 * d]
            dec_row = rows[3 * d + 2:3 * d + 3]
            it["dec"] = jnp.concatenate([dec_row, dec_row], axis=1)
            it["vaug"] = jnp.concatenate([v_ref[0, 0, sl, :], ones], axis=1)
            wt = jnp.exp(jnp.where(incl[d], rows[3 * d:3 * d + 1] - r_col, NEG_BIG)) * qkm_scr[sl, :]
            it["wt"] = wt.astype(BF16)
            awk = jnp.broadcast_to(rows[3 * d + 1:3 * d + 2], (ch, ch)).astype(BF16)
            it["wk"] = kt_scr[:, sl] * awk
        for it in items:
            it["intra"] = _dot(it["wt"], it["vaug"])
            it["update"] = _dot(it["wk"], it["vaug"])
        states = list(states)
        for it in items:
            d = it["d"]
            it["qs"] = _dot(qm_scr[it["sl"], :], states[d].astype(BF16))
            states[d] = states[d] * it["dec"] + it["update"]
        for it in items:
            nd = it["inter"] * it["qs"] + it["intra"]
            num = nd[:, :HEAD_DIM]
            den = nd[:, HEAD_DIM:]
            h_scr[it["d"], it["sl"], :] = num / jnp.maximum(jnp.abs(den), it["enm"])
        return tuple(states)

    def final_body(c, carry):
        sl = pl.ds(pl.multiple_of(c * ch, ch), ch)
        hh = h_scr[0, sl, :] + h_scr[1, sl, :]
        y = hh * lax.rsqrt(jnp.mean(hh * hh, axis=-1, keepdims=True) + RMS_EPS) * nw_ref[...]
        y = y * _sigmoid(o_ref[0, 0, sl, :].astype(F32)) * _silu(z_ref[0, 0, sl, :].astype(F32))
        out_ref[0, sl, :] = y.astype(out_ref.dtype)
        return carry

    n_iter = nc // MLSTM_UNROLL
    _fill_padded(pad_scr, qk_ref, seq)
    lax.fori_loop(0, n_iter, _unrolled(prep_body, MLSTM_UNROLL), 0)
    zero_state = jnp.zeros((ch, 2 * HEAD_DIM), F32)
    lax.fori_loop(0, n_iter, scan_group, (zero_state, zero_state))
    lax.fori_loop(0, n_iter, _unrolled(final_body, MLSTM_UNROLL), 0)


def _mlstm(p, rows, conv_w, norm_w, *, col0):
    b, _, s, _ = p.shape
    h = N_HEADS
    assert (s // SCAN_CHUNK) % MLSTM_UNROLL == 0

    def pblock(off):
        return pl.BlockSpec((1, 1, s, HEAD_DIM), lambda bi, hi: (bi, col0 + off + hi, 0, 0))

    return pl.pallas_call(
        functools.partial(_mlstm_kernel, seq=s),
        grid=(b, h),
        in_specs=[
            pblock(0), pblock(h), pblock(2 * h), pblock(3 * h),
            _gate_rows_spec(s),
            pl.BlockSpec((CONV_K, HEAD_DIM), lambda bi, hi: (0, hi)),
            pl.BlockSpec((1, HEAD_DIM), lambda bi, hi: (0, hi)),
        ],
        out_specs=pl.BlockSpec((1, s, HEAD_DIM), lambda bi, hi: (bi, 0, hi)),
        out_shape=jax.ShapeDtypeStruct((b, s, h * HEAD_DIM), BF16),
        scratch_shapes=[
            pltpu.VMEM((s, HEAD_DIM), BF16),
            pltpu.VMEM((HEAD_DIM, s), BF16),
            pltpu.VMEM((s, SCAN_CHUNK), F32),
            pltpu.VMEM((s, SCAN_CHUNK), F32),
            pltpu.VMEM((2, s, HEAD_DIM), F32),
            pltpu.VMEM((s + 2 * CONV_HALO, HEAD_DIM), F32),
        ],
        compiler_params=pltpu.CompilerParams(
            dimension_semantics=("arbitrary", "arbitrary"), vmem_limit_bytes=VMEM_LIMIT_BYTES),
        name="mlstm",
    )(p, p, p, p, rows, conv_w, norm_w)


def _outproj_kernel(g_ref, m_ref, wo_ref, x_ref, npw_ref, y_ref, *, width):
    mixed = _dot(g_ref[0], wo_ref[:width]) + _dot(m_ref[0], wo_ref[width:])
    ms = jnp.mean(mixed * mixed, axis=-1, keepdims=True)
    y_ref[0] = x_ref[0] + mixed * lax.rsqrt(ms + RMS_EPS) * npw_ref[...]


def _out_proj(g, m, w_out, x, npw, *, tm=1024):
    b, s, d = x.shape
    width = g.shape[-1]
    return pl.pallas_call(
        functools.partial(_outproj_kernel, width=width),
        grid=(b, s // tm),
        in_specs=[
            pl.BlockSpec((1, tm, width), lambda bi, si: (bi, si, 0)),
            pl.BlockSpec((1, tm, width), lambda bi, si: (bi, si, 0)),
            pl.BlockSpec(w_out.shape, lambda bi, si: (0, 0)),
            pl.BlockSpec((1, tm, d), lambda bi, si: (bi, si, 0)),
            pl.BlockSpec((1, d), lambda bi, si: (0, 0)),
        ],
        out_specs=pl.BlockSpec((1, tm, d), lambda bi, si: (bi, si, 0)),
        out_shape=jax.ShapeDtypeStruct((b, s, d), F32),
        compiler_params=pltpu.CompilerParams(
            dimension_semantics=("arbitrary", "arbitrary"), vmem_limit_bytes=VMEM_LIMIT_BYTES),
        name="out_proj",
    )(g, m, w_out, x, npw)


def _layer(x, norm_pre_w, w_in, gdn_conv_w, gdn_a_log, gdn_dt_bias, gdn_norm_w,
           mlstm_conv_w, mlstm_gate_bias, mlstm_norm_w, w_out, norm_post_w):
    d = x.shape[-1]
    h = N_HEADS
    gw = h * HEAD_DIM
    mqk = h * MLSTM_DQK
    w_main_t, w_gate_t = _weight_layout(jnp.swapaxes(w_in, 1, 2))

    p, gt = _in_proj(x, norm_pre_w.reshape(1, d), w_main_t, w_gate_t)
    g_rows, m_rows = _gates(gt, gdn_a_log.reshape(2 * h, 1), gdn_dt_bias.reshape(2 * h, 1),
                            mlstm_gate_bias.reshape(4 * h, 1))
    g_rows = g_rows[:, :, :, None, :]
    m_rows = m_rows[:, :, :, None, :]

    cq = mlstm_conv_w[:, :mqk].reshape(CONV_K, h, MLSTM_DQK)
    ck = mlstm_conv_w[:, mqk:].reshape(CONV_K, h, MLSTM_DQK)
    m_conv = jnp.concatenate([cq, ck], axis=2).reshape(CONV_K, gw)

    g_out = _gdn(p, g_rows, gdn_conv_w, gdn_norm_w.reshape(1, HEAD_DIM), col0=0)
    m_out = _mlstm(p, m_rows, m_conv, mlstm_norm_w.reshape(1, gw), col0=4 * h)
    return _out_proj(g_out, m_out, w_out.astype(BF16), x, norm_post_w.reshape(1, d))


def kernel(x, norm_pre_w, w_in, gdn_conv_w, gdn_a_log, gdn_dt_bias, gdn_norm_w,
           mlstm_conv_w, mlstm_gate_bias, mlstm_norm_w, w_out, norm_post_w):
    for layer in range(norm_pre_w.shape[0]):
        x = _layer(x, norm_pre_w[layer], w_in[layer:layer + 1], gdn_conv_w[layer], gdn_a_log[layer],
                   gdn_dt_bias[layer], gdn_norm_w[layer], mlstm_conv_w[layer],
                   mlstm_gate_bias[layer], mlstm_norm_w[layer], w_out[layer], norm_post_w[layer])
    return x
```
